```python
import jax, jax.numpy as jnp
from jax import lax
import numpy as np

D_MODEL = 2048
BATCH = 1
SEQ = 8192
DEPTH = 2
DEC_BATCH = 8
DEC_SEQ = 32
PAST_LEN = 4096

CHUNK = 64
Q_BLOCK = 128
HEAD_DIM = 128
MIX_DIM = D_MODEL
CONV_DIM = MIX_DIM // 4
CONV_W = 3
SB_HEADS = (MIX_DIM - CONV_DIM) // (2 * HEAD_DIM)
FOX_HEADS = SB_HEADS
SB_DIM = SB_HEADS * HEAD_DIM
FOX_DIM = FOX_HEADS * HEAD_DIM
IN_DIM = 3 * CONV_DIM + 3 * SB_DIM + 3 * FOX_DIM + FOX_HEADS
D_FF = ((8 * D_MODEL + 3 * 256 - 1) // (3 * 256)) * 256
NORM_EPS = 1e-5

kernel_name = 'hybrid_stream_conv_stickbreak_fox_step'


def _rmsnorm(x, g):
    xf = x.astype(jnp.float32)
    xf = xf * lax.rsqrt(jnp.mean(xf * xf, axis=-1, keepdims=True) + NORM_EPS)
    return (xf * g.astype(jnp.float32)).astype(x.dtype)


def _split_points():
    c, s, f = CONV_DIM, SB_DIM, FOX_DIM
    return [c, 2 * c, 3 * c, 3 * c + s, 3 * c + 2 * s, 3 * c + 3 * s,
            3 * c + 3 * s + f, 3 * c + 3 * s + 2 * f, 3 * c + 3 * s + 3 * f]


def _stick_breaking(q, k, v, q_pos, k_pos):
    z = jnp.einsum('bqhd,bkhd->bhqk', q, k).astype(jnp.float32) * (HEAD_DIM ** -0.5)
    mask = k_pos[None, :] < q_pos[:, None]
    log_1m = jnp.where(mask, jax.nn.log_sigmoid(-z), 0.0)
    later = lax.cumsum(log_1m, axis=3, reverse=True) - log_1m
    w = jnp.where(mask, jnp.exp(jax.nn.log_sigmoid(z) + later), 0.0)
    return jnp.einsum('bhqk,bkhd->bqhd', w.astype(v.dtype), v)


def _forgetting(q, k, v, cq, ck, q_pos, k_pos):
    s = jnp.einsum('bqhd,bkhd->bhqk', q, k).astype(jnp.float32) * (HEAD_DIM ** -0.5)
    s = s + jnp.transpose(cq, (0, 2, 1))[..., :, None] - jnp.transpose(ck, (0, 2, 1))[..., None, :]
    mask = k_pos[None, :] <= q_pos[:, None]
    p = jax.nn.softmax(jnp.where(mask, s, -jnp.inf), axis=-1)
    return jnp.einsum('bhqk,bkhd->bqhd', p.astype(v.dtype), v)


def _query_blocks(attn, q_side, q_pos):
    n_blk = q_pos.shape[0] // Q_BLOCK
    def split(a):
        return jnp.moveaxis(a.reshape(a.shape[0], n_blk, Q_BLOCK, *a.shape[2:]), 1, 0)
    blocks = tuple(split(a) for a in q_side)
    out = lax.map(lambda xs: attn(*xs[0], xs[1]), (blocks, q_pos.reshape(n_blk, Q_BLOCK)))
    out = jnp.moveaxis(out, 0, 1)
    return out.reshape(out.shape[0], n_blk * Q_BLOCK, *out.shape[3:])


def _mixer(h, w_in, b_f, conv_w, conv_past, sbk_past, sbv_past, fk_past, fv_past, flogf_past, blocked):
    bsz, t_len, _ = h.shape
    past = sbk_past.shape[1]
    proj = h @ w_in
    g_b, g_c, hc, sq, sk, sv, fq, fk, fv, f_logit = jnp.split(proj, _split_points(), axis=-1)

    u = g_c * hc
    u_full = jnp.concatenate([conv_past.astype(u.dtype), u], axis=1)
    conv = conv_w[0] * u_full[:, 0:t_len]
    for i in range(1, CONV_W):
        conv = conv + conv_w[i] * u_full[:, i:i + t_len]
    y_a = g_b * conv
    new_conv = u_full[:, t_len:]

    heads = lambda a: a.reshape(bsz, t_len, -1, HEAD_DIM)
    sq, sk, sv, fq, fk, fv = heads(sq), heads(sk), heads(sv), heads(fq), heads(fk), heads(fv)
    q_pos = past + jnp.arange(t_len)
    k_pos = jnp.arange(past + t_len)
    sk_all = jnp.concatenate([sbk_past.astype(sk.dtype), sk], axis=1)
    sv_all = jnp.concatenate([sbv_past.astype(sv.dtype), sv], axis=1)
    fk_all = jnp.concatenate([fk_past.astype(fk.dtype), fk], axis=1)
    fv_all = jnp.concatenate([fv_past.astype(fv.dtype), fv], axis=1)

    logf = jax.nn.log_sigmoid(f_logit.astype(jnp.float32) + b_f.astype(jnp.float32))
    ck = jnp.cumsum(jnp.concatenate([flogf_past.astype(jnp.float32), logf], axis=1), axis=1)
    cq = ck[:, past:]

    sb_fn = lambda qb, qp: _stick_breaking(qb, sk_all, sv_all, qp, k_pos)
    fox_fn = lambda qb, cqb, qp: _forgetting(qb, fk_all, fv_all, cqb, ck, qp, k_pos)
    if blocked:
        y_sb = _query_blocks(sb_fn, (sq,), q_pos)
        y_fox = _query_blocks(fox_fn, (fq, cq), q_pos)
    else:
        y_sb = sb_fn(sq, q_pos)
        y_fox = fox_fn(fq, cq, q_pos)

    y = jnp.concatenate([y_a, y_sb.reshape(bsz, t_len, SB_DIM), y_fox.reshape(bsz, t_len, FOX_DIM)], axis=-1)
    return y, (new_conv, sk, sv, fk, fv, logf)


def _layer(x, norm1_g, w_in, b_f, conv_w, w_out, norm2_g, w_gate, w_up, w_down, past, blocked):
    y, new = _mixer(_rmsnorm(x, norm1_g), w_in, b_f, conv_w, *past, blocked)
    x = x + y @ w_out
    h = _rmsnorm(x, norm2_g)
    x = x + (jax.nn.silu(h @ w_gate) * (h @ w_up)) @ w_down
    return x, new


def setup_inputs(seed: int = 0) -> dict:
    key = jax.random.key(seed)
    ks = jax.random.split(key, 20)
    nrm = lambda k, shape, scale=1.0: scale * jax.random.normal(k, shape, jnp.float32)
    return {
        'x_prompt': nrm(ks[0], (BATCH, SEQ, D_MODEL)),
        'x_sample': nrm(ks[1], (DEC_BATCH, DEC_SEQ, D_MODEL)),
        'state_conv': nrm(ks[2], (DEPTH, DEC_BATCH, CONV_W - 1, CONV_DIM)),
        'cache_sb_k': nrm(ks[3], (DEPTH, DEC_BATCH, PAST_LEN, SB_HEADS, HEAD_DIM)),
        'cache_sb_v': nrm(ks[4], (DEPTH, DEC_BATCH, PAST_LEN, SB_HEADS, HEAD_DIM)),
        'cache_fox_k': nrm(ks[5], (DEPTH, DEC_BATCH, PAST_LEN, FOX_HEADS, HEAD_DIM)),
        'cache_fox_v': nrm(ks[6], (DEPTH, DEC_BATCH, PAST_LEN, FOX_HEADS, HEAD_DIM)),
        'cache_fox_logf': jax.nn.log_sigmoid(nrm(ks[7], (DEPTH, DEC_BATCH, PAST_LEN, FOX_HEADS))),
        'norm1_g': 1.0 + nrm(ks[8], (DEPTH, D_MODEL), 0.01),
        'w_in': nrm(ks[9], (DEPTH, D_MODEL, IN_DIM), D_MODEL ** -0.5),
        'b_f': nrm(ks[10], (DEPTH, FOX_HEADS), 0.1),
        'conv_w': nrm(ks[11], (DEPTH, CONV_W, CONV_DIM), CONV_W ** -0.5),
        'w_out': nrm(ks[12], (DEPTH, MIX_DIM, D_MODEL), MIX_DIM ** -0.5),
        'norm2_g': 1.0 + nrm(ks[13], (DEPTH, D_MODEL), 0.01),
        'w_gate': nrm(ks[14], (DEPTH, D_MODEL, D_FF), D_MODEL ** -0.5),
        'w_up': nrm(ks[15], (DEPTH, D_MODEL, D_FF), D_MODEL ** -0.5),
        'w_down': nrm(ks[16], (DEPTH, D_FF, D_MODEL), D_FF ** -0.5),
        'final_g': 1.0 + nrm(ks[17], (D_MODEL,), 0.01),
    }


def reference(x_prompt, x_sample, state_conv, cache_sb_k, cache_sb_v, cache_fox_k, cache_fox_v,
              cache_fox_logf, norm1_g, w_in, b_f, conv_w, w_out, norm2_g, w_gate, w_up, w_down, final_g):
    dt = x_prompt.dtype
    bp = x_prompt.shape[0]
    p_past = (jnp.zeros((bp, CONV_W - 1, CONV_DIM), dt),
              jnp.zeros((bp, 0, SB_HEADS, HEAD_DIM), dt), jnp.zeros((bp, 0, SB_HEADS, HEAD_DIM), dt),
              jnp.zeros((bp, 0, FOX_HEADS, HEAD_DIM), dt), jnp.zeros((bp, 0, FOX_HEADS, HEAD_DIM), dt),
              jnp.zeros((bp, 0, FOX_HEADS), jnp.float32))
    xp, xs = x_prompt, x_sample
    p_new, s_new = [], []
    for l in range(DEPTH):
        w = (norm1_g[l], w_in[l], b_f[l], conv_w[l], w_out[l], norm2_g[l], w_gate[l], w_up[l], w_down[l])
        xp, pn = _layer(xp, *w, p_past, True)
        s_past = (state_conv[l], cache_sb_k[l], cache_sb_v[l], cache_fox_k[l], cache_fox_v[l], cache_fox_logf[l])
        xs, sn = _layer(xs, *w, s_past, False)
        p_new.append(pn)
        s_new.append(sn)
    stack = lambda news, i: jnp.stack([n[i] for n in news])
    y_prompt = _rmsnorm(xp, final_g)
    y_sample = _rmsnorm(xs, final_g)
    return (y_prompt, y_sample,
            stack(p_new, 0), stack(p_new, 1), stack(p_new, 2), stack(p_new, 3), stack(p_new, 4), stack(p_new, 5),
            stack(s_new, 0), stack(s_new, 1), stack(s_new, 2), stack(s_new, 3), stack(s_new, 4), stack(s_new, 5))
```

```python
import functools

import jax
import jax.numpy as jnp
from jax import lax
from jax.experimental import pallas as pl
from jax.experimental.pallas import tpu as pltpu

F32 = jnp.float32
BF16 = jnp.bfloat16

NORM_EPS = 1e-5
HEAD_DIM = 128
CONV_W = 3
LANES = 128
VMEM_LIMIT = 56 * 1024 * 1024
NEG_BIG = -1e30


def _params(sem, vmem=VMEM_LIMIT):
    return pltpu.CompilerParams(dimension_semantics=sem, vmem_limit_bytes=vmem)


def _dot(a, b):
    return jnp.dot(a, b, preferred_element_type=F32)


def _dot_nt(a, b):
    return lax.dot_general(a, b, (((1,), (1,)), ((), ())), preferred_element_type=F32)


def _softplus(z):
    return jnp.maximum(z, 0.0) + jnp.log(1.0 + jnp.exp(-jnp.abs(z)))


def _split_bf16(x, parts):
    out = []
    rem = x
    for p in range(parts):
        hi = rem.astype(BF16)
        out.append(hi)
        if p + 1 < parts:
            rem = rem - hi.astype(F32)
    return out


def _tri_incl(n, seg=None):
    j = lax.broadcasted_iota(jnp.int32, (n, n), 0)
    s = lax.broadcasted_iota(jnp.int32, (n, n), 1)
    t = jnp.where(j <= s, 1.0, 0.0)
    if seg is not None:
        t = jnp.where(j // seg == s // seg, t, 0.0)
    return t.astype(BF16)


def _suffix_incl(n):
    j = lax.broadcasted_iota(jnp.int32, (n, n), 0)
    s = lax.broadcasted_iota(jnp.int32, (n, n), 1)
    return jnp.where(j >= s, 1.0, 0.0).astype(BF16)


def _prefix_sum_lanes(x, tri, carry):
    outs = []
    for c in range(x.shape[1] // LANES):
        xc = x[:, c * LANES:(c + 1) * LANES]
        loc = sum(_dot(p, tri) for p in _split_bf16(xc, 3))
        oc = loc if carry is None else loc + carry
        outs.append(oc)
        if carry is not None:
            carry = oc[:, LANES - 1:LANES]
    return jnp.concatenate(outs, axis=1), carry


def _rmsnorm_f32(x, g):
    ms = jnp.mean(x * x, axis=-1, keepdims=True)
    return x * lax.rsqrt(ms + NORM_EPS) * g


def _rmsnorm_kernel(x_ref, g_ref, o_ref):
    o_ref[...] = _rmsnorm_f32(x_ref[...], g_ref[...]).astype(o_ref.dtype)


def _rmsnorm(x, g, tm):
    m, d = x.shape
    return pl.pallas_call(
        _rmsnorm_kernel,
        grid=(m // tm,),
        in_specs=[pl.BlockSpec((tm, d), lambda i: (i, 0)), pl.BlockSpec((1, d), lambda i: (0, 0))],
        out_specs=pl.BlockSpec((tm, d), lambda i: (i, 0)),
        out_shape=jax.ShapeDtypeStruct((m, d), BF16),
        compiler_params=_params(("arbitrary",)),
        name="rmsnorm",
    )(x, g.reshape(1, d))


def _gated_conv(proj, cw, u_m1, u_m2_fix):
    c = proj.shape[1] // 3
    g_b, g_c, hc = proj[:, :c], proj[:, c:2 * c], proj[:, 2 * c:]
    u = g_c * hc
    r1 = pltpu.roll(u, 1, 0)
    r2 = pltpu.roll(u, 2, 0)
    u1 = u_m1(r1)
    u2 = u_m2_fix(r2)
    conv = cw[0:1, :] * u2 + cw[1:2, :] * u1 + cw[2:3, :] * u
    return g_b * conv, u, r2


def _log_sigmoid(x):
    return jnp.minimum(x, 0.0) - jnp.log(1.0 + jnp.exp(-jnp.abs(x)))


def _mix_a_prompt_kernel(h_ref, wa_ref, wf_ref, bf_ref, cw_ref,
                         ya_ref, conv_ref, logf_ref, ck_ref, uprev_ref, carry_ref):
    i = pl.program_id(0)

    @pl.when(i == 0)
    def _():
        uprev_ref[...] = jnp.zeros_like(uprev_ref)
        carry_ref[...] = jnp.zeros_like(carry_ref)

    h = h_ref[...]
    proj = _dot(h, wa_ref[...])
    rows = lax.broadcasted_iota(jnp.int32, (proj.shape[0], proj.shape[1] // 3), 0)
    prev = uprev_ref[...]
    m1 = lambda r1: jnp.where(rows == 0, prev[1:2, :], r1)
    m2 = lambda r2: jnp.where(rows == 0, prev[0:1, :], jnp.where(rows == 1, prev[1:2, :], r2))
    ya, _, r2 = _gated_conv(proj, cw_ref[...], m1, m2)
    ya_ref[...] = ya.astype(ya_ref.dtype)
    tail = r2[0:2, :]
    uprev_ref[0:2, :] = tail
    conv_ref[...] = tail

    logit = _dot_nt(wf_ref[...], h)[0:8, :]
    logf = _log_sigmoid(logit + bf_ref[...])
    logf_ref[...] = logf
    ck, carry = _prefix_sum_lanes(logf, _tri_incl(LANES), carry_ref[...])
    ck_ref[...] = ck
    carry_ref[...] = carry


def _mix_a_prompt(h, wa, wf, bfp, cw, tm):
    t, d = h.shape
    c3 = wa.shape[1]
    c = c3 // 3
    return pl.pallas_call(
        _mix_a_prompt_kernel,
        grid=(t // tm,),
        in_specs=[
            pl.BlockSpec((tm, d), lambda i: (i, 0)),
            pl.BlockSpec((d, c3), lambda i: (0, 0)),
            pl.BlockSpec((16, d), lambda i: (0, 0)),
            pl.BlockSpec((8, 1), lambda i: (0, 0)),
            pl.BlockSpec((CONV_W, c), lambda i: (0, 0)),
        ],
        out_specs=[
            pl.BlockSpec((tm, c), lambda i: (i, 0)),
            pl.BlockSpec((CONV_W - 1, c), lambda i: (0, 0)),
            pl.BlockSpec((8, tm), lambda i: (0, i)),
            pl.BlockSpec((8, tm), lambda i: (0, i)),
        ],
        out_shape=[
            jax.ShapeDtypeStruct((t, c), BF16),
            jax.ShapeDtypeStruct((CONV_W - 1, c), F32),
            jax.ShapeDtypeStruct((8, t), F32),
            jax.ShapeDtypeStruct((8, t), F32),
        ],
        scratch_shapes=[pltpu.VMEM((8, c), F32), pltpu.VMEM((8, 1), F32)],
        compiler_params=_params(("arbitrary",)),
        name="mix_a_prompt",
    )(h, wa, wf, bfp, cw)


def _mix_a_sample_kernel(h_ref, wa_ref, wf_ref, bf_ref, cw_ref, st_ref,
                         ya_ref, conv_ref, logf_ref, ckl_ref, u_ref, *, seq):
    h = h_ref[...]
    proj = _dot(h, wa_ref[...])
    n = proj.shape[0]
    nb = n // seq
    rows = lax.broadcasted_iota(jnp.int32, (n, proj.shape[1] // 3), 0)

    def m1(r1):
        for b in range(nb):
            r1 = jnp.where(rows == b * seq, st_ref[2 * b + 1:2 * b + 2, :], r1)
        return r1

    def m2(r2):
        for b in range(nb):
            r2 = jnp.where(rows == b * seq, st_ref[2 * b:2 * b + 1, :], r2)
            r2 = jnp.where(rows == b * seq + 1, st_ref[2 * b + 1:2 * b + 2, :], r2)
        return r2

    ya, u, _ = _gated_conv(proj, cw_ref[...], m1, m2)
    ya_ref[...] = ya.astype(ya_ref.dtype)
    u_ref[...] = u
    for b in range(nb):
        conv_ref[2 * b:2 * b + 2, :] = u_ref[(b + 1) * seq - 2:(b + 1) * seq, :]

    logit = _dot_nt(wf_ref[...], h)[0:8, :]
    logf = _log_sigmoid(logit + bf_ref[...])
    logf_ref[...] = logf
    ckl, _ = _prefix_sum_lanes(logf, _tri_incl(LANES, seg=seq), None)
    ckl_ref[...] = ckl


def _mix_a_sample(h, wa, wf, bfp, cw, state, seq):
    n, d = h.shape
    c3 = wa.shape[1]
    c = c3 // 3
    nb = n // seq
    full = lambda shape: pl.BlockSpec(shape, lambda i: tuple(0 for _ in shape))
    return pl.pallas_call(
        functools.partial(_mix_a_sample_kernel, seq=seq),
        grid=(1,),
        in_specs=[full((n, d)), full((d, c3)), full((16, d)), full((8, 1)), full((CONV_W, c)),
                  full((nb * (CONV_W - 1), c))],
        out_specs=[full((n, c)), full((nb * (CONV_W - 1), c)), full((8, n)), full((8, n))],
        out_shape=[
            jax.ShapeDtypeStruct((n, c), BF16),
            jax.ShapeDtypeStruct((nb * (CONV_W - 1), c), F32),
            jax.ShapeDtypeStruct((8, n), F32),
            jax.ShapeDtypeStruct((8, n), F32),
        ],
        scratch_shapes=[pltpu.VMEM((n, c), F32)],
        compiler_params=_params(("arbitrary",)),
        name="mix_a_sample",
    )(h, wa, wf, bfp, cw, state)


def _cumsum_rows_kernel(x_ref, o_ref):
    ck, _ = _prefix_sum_lanes(x_ref[...], _tri_incl(LANES), jnp.zeros((x_ref.shape[0], 1), F32))
    o_ref[...] = ck


def _cumsum_rows(x):
    return pl.pallas_call(
        _cumsum_rows_kernel,
        grid=(1,),
        in_specs=[pl.BlockSpec(x.shape, lambda i: (0, 0))],
        out_specs=pl.BlockSpec(x.shape, lambda i: (0, 0)),
        out_shape=jax.ShapeDtypeStruct(x.shape, F32),
        compiler_params=_params(("arbitrary",)),
        name="cumsum_rows",
    )(x)


def _qkv_kernel(h_ref, w_ref, qkv_ref, sbk_ref, sbv_ref, fk_ref, fv_ref):
    j = pl.program_id(1)
    acc = _dot(h_ref[...], w_ref[...])
    qkv_ref[...] = acc.astype(qkv_ref.dtype)
    for idx, ref in ((1, sbk_ref), (2, sbv_ref), (4, fk_ref), (5, fv_ref)):
        @pl.when(j == idx)
        def _(ref=ref):
            ref[...] = acc


def _qkv_proj(h, w, tm):
    m, d = h.shape
    n = w.shape[1]
    tn = n // 6
    cache = jax.ShapeDtypeStruct((m, tn), F32)
    cache_spec = pl.BlockSpec((tm, tn), lambda i, j: (i, 0))
    return pl.pallas_call(
        _qkv_kernel,
        grid=(m // tm, 6),
        in_specs=[pl.BlockSpec((tm, d), lambda i, j: (i, 0)), pl.BlockSpec((d, tn), lambda i, j: (0, j))],
        out_specs=[pl.BlockSpec((tm, tn), lambda i, j: (i, j))] + [cache_spec] * 4,
        out_shape=[jax.ShapeDtypeStruct((m, n), BF16)] + [cache] * 4,
        compiler_params=_params(("arbitrary", "arbitrary")),
        name="qkv_proj",
    )(h, w)


def _sb_tile(q, kb, vb, suffix, mask, acc_ref, r_ref):
    z = _dot_nt(q, kb) * (HEAD_DIM ** -0.5)
    sp = _softplus(z)
    if mask is not None:
        sp = jnp.where(mask, sp, 0.0)
    s_incl = sum(_dot(p, suffix) for p in _split_bf16(sp, 2))
    w = jnp.exp(z - s_incl)
    if mask is not None:
        w = jnp.where(mask, w, 0.0)
    pv = _dot(w.astype(BF16), vb)
    r = r_ref[...]
    acc_ref[...] += jnp.exp(-r) * pv
    r_ref[...] = r + s_incl[:, 0:1]


def _sb_prompt_kernel(q_ref, k_ref, v_ref, o_ref, acc_ref, r_ref, *, tq):
    qi = pl.program_id(1)
    q = q_ref[...]
    acc_ref[...] = jnp.zeros_like(acc_ref)
    r_ref[...] = jnp.zeros_like(r_ref)
    suffix = _suffix_incl(tq)
    row = lax.broadcasted_iota(jnp.int32, (tq, tq), 0)
    col = lax.broadcasted_iota(jnp.int32, (tq, tq), 1)
    start = pl.multiple_of(qi * tq, tq)
    _sb_tile(q, k_ref[pl.ds(start, tq), :], v_ref[pl.ds(start, tq), :], suffix, col < row, acc_ref, r_ref)

    def body(jj, c):
        st = pl.multiple_of((qi - 1 - jj) * tq, tq)
        _sb_tile(q, k_ref[pl.ds(st, tq), :], v_ref[pl.ds(st, tq), :], suffix, None, acc_ref, r_ref)
        return c

    lax.fori_loop(0, qi, body, 0)
    o_ref[...] = acc_ref[...].astype(o_ref.dtype)


def _sb_prompt(qkv, n_heads, tq):
    t = qkv.shape[0]
    return pl.pallas_call(
        functools.partial(_sb_prompt_kernel, tq=tq),
        grid=(n_heads, t // tq),
        in_specs=[
            pl.BlockSpec((tq, HEAD_DIM), lambda h, i: (i, h)),
            pl.BlockSpec((t, HEAD_DIM), lambda h, i: (0, n_heads + h)),
            pl.BlockSpec((t, HEAD_DIM), lambda h, i: (0, 2 * n_heads + h)),
        ],
        out_specs=pl.BlockSpec((tq, HEAD_DIM), lambda h, i: (i, h)),
        out_shape=jax.ShapeDtypeStruct((t, n_heads * HEAD_DIM), BF16),
        scratch_shapes=[pltpu.VMEM((tq, HEAD_DIM), F32), pltpu.VMEM((tq, HEAD_DIM), F32)],
        compiler_params=_params(("arbitrary", "arbitrary")),
        name="sb_prompt",
    )(qkv, qkv, qkv)


def _sb_sample_kernel(q_ref, kn_ref, vn_ref, kc_ref, vc_ref, o_ref, acc_ref, r_ref, *, tk):
    q = q_ref[...]
    seq = q.shape[0]
    past = kc_ref.shape[0]
    acc_ref[...] = jnp.zeros_like(acc_ref)
    r_ref[...] = jnp.zeros_like(r_ref)
    row = lax.broadcasted_iota(jnp.int32, (seq, seq), 0)
    col = lax.broadcasted_iota(jnp.int32, (seq, seq), 1)
    _sb_tile(q, kn_ref[...], vn_ref[...], _suffix_incl(seq), col < row, acc_ref, r_ref)
    suffix = _suffix_incl(tk)
    nblk = past // tk

    def body(jj, c):
        st = pl.multiple_of((nblk - 1 - jj) * tk, tk)
        kb = kc_ref[pl.ds(st, tk), :].astype(BF16)
        vb = vc_ref[pl.ds(st, tk), :].astype(BF16)
        _sb_tile(q, kb, vb, suffix, None, acc_ref, r_ref)
        return c

    lax.fori_loop(0, nblk, body, 0)
    o_ref[...] = acc_ref[...].astype(o_ref.dtype)


def _sb_sample(qkv, cache_k, cache_v, layer, n_heads, seq, tk):
    n = qkv.shape[0]
    nb = n // seq
    past = cache_k.shape[2]
    cache_spec = pl.BlockSpec((None, None, past, HEAD_DIM), lambda b, h: (layer, b, 0, h))
    return pl.pallas_call(
        functools.partial(_sb_sample_kernel, tk=tk),
        grid=(nb, n_heads),
        in_specs=[
            pl.BlockSpec((seq, HEAD_DIM), lambda b, h: (b, h)),
            pl.BlockSpec((seq, HEAD_DIM), lambda b, h: (b, n_heads + h)),
            pl.BlockSpec((seq, HEAD_DIM), lambda b, h: (b, 2 * n_heads + h)),
            cache_spec, cache_spec,
        ],
        out_specs=pl.BlockSpec((seq, HEAD_DIM), lambda b, h: (b, h)),
        out_shape=jax.ShapeDtypeStruct((n, n_heads * HEAD_DIM), BF16),
        scratch_shapes=[pltpu.VMEM((seq, HEAD_DIM), F32), pltpu.VMEM((seq, HEAD_DIM), F32)],
        compiler_params=_params(("arbitrary", "arbitrary")),
        name="sb_sample",
    )(qkv, qkv, qkv, cache_k, cache_v)


def _fox_tile(q, kb, vb, bias, mask, m_ref, l_ref, acc_ref):
    s = _dot_nt(q, kb) * (HEAD_DIM ** -0.5) + bias
    if mask is not None:
        s = jnp.where(mask, s, NEG_BIG)
    m_prev = m_ref[...]
    m_new = jnp.maximum(m_prev, jnp.max(s, axis=1, keepdims=True))
    alpha = jnp.exp(m_prev - m_new)
    p = jnp.exp(s - m_new)
    l_ref[...] = alpha * l_ref[...] + jnp.sum(p, axis=1, keepdims=True)
    acc_ref[...] = alpha * acc_ref[...] + _dot(p.astype(BF16), vb)
    m_ref[...] = m_new


def _fox_init(m_ref, l_ref, acc_ref):
    m_ref[...] = jnp.full(m_ref.shape, NEG_BIG, F32)
    l_ref[...] = jnp.zeros_like(l_ref)
    acc_ref[...] = jnp.zeros_like(acc_ref)


def _fox_prompt_kernel(q_ref, k_ref, v_ref, ck_ref, o_ref, m_ref, l_ref, acc_ref, *, tq):
    h = pl.program_id(0)
    qi = pl.program_id(1)
    q = q_ref[...]
    _fox_init(m_ref, l_ref, acc_ref)
    row = lax.broadcasted_iota(jnp.int32, (tq, tq), 0)
    col = lax.broadcasted_iota(jnp.int32, (tq, tq), 1)
    start = pl.multiple_of(qi * tq, tq)
    ck_diag = ck_ref[pl.ds(h, 1), pl.ds(start, tq)]
    c0 = ck_diag[:, 0:1]
    _fox_tile(q, k_ref[pl.ds(start, tq), :], v_ref[pl.ds(start, tq), :], c0 - ck_diag, col <= row,
              m_ref, l_ref, acc_ref)

    def body(jj, c):
        st = pl.multiple_of((qi - 1 - jj) * tq, tq)
        bias = c0 - ck_ref[pl.ds(h, 1), pl.ds(st, tq)]
        _fox_tile(q, k_ref[pl.ds(st, tq), :], v_ref[pl.ds(st, tq), :], bias, None, m_ref, l_ref, acc_ref)
        return c

    lax.fori_loop(0, qi, body, 0)
    o_ref[...] = (acc_ref[...] / l_ref[...]).astype(o_ref.dtype)


def _fox_prompt(qkv, ck, n_heads, tq):
    t = qkv.shape[0]
    return pl.pallas_call(
        functools.partial(_fox_prompt_kernel, tq=tq),
        grid=(n_heads, t // tq),
        in_specs=[
            pl.BlockSpec((tq, HEAD_DIM), lambda h, i: (i, 3 * n_heads + h)),
            pl.BlockSpec((t, HEAD_DIM), lambda h, i: (0, 4 * n_heads + h)),
            pl.BlockSpec((t, HEAD_DIM), lambda h, i: (0, 5 * n_heads + h)),
            pl.BlockSpec((8, t), lambda h, i: (0, 0)),
        ],
        out_specs=pl.BlockSpec((tq, HEAD_DIM), lambda h, i: (i, h)),
        out_shape=jax.ShapeDtypeStruct((t, n_heads * HEAD_DIM), BF16),
        scratch_shapes=[pltpu.VMEM((tq, 1), F32), pltpu.VMEM((tq, 1), F32), pltpu.VMEM((tq, HEAD_DIM), F32)],
        compiler_params=_params(("arbitrary", "arbitrary")),
        name="fox_prompt",
    )(qkv, qkv, qkv, ck)


def _fox_sample_kernel(q_ref, kn_ref, vn_ref, kc_ref, vc_ref, ckp_ref, ckl_ref, o_ref,
                       m_ref, l_ref, acc_ref, *, tk):
    q = q_ref[...]
    seq = q.shape[0]
    past = kc_ref.shape[0]
    _fox_init(m_ref, l_ref, acc_ref)
    row = lax.broadcasted_iota(jnp.int32, (seq, seq), 0)
    col = lax.broadcasted_iota(jnp.int32, (seq, seq), 1)
    _fox_tile(q, kn_ref[...], vn_ref[...], -ckl_ref[...], col <= row, m_ref, l_ref, acc_ref)
    c0 = ckp_ref[:, past - 1:past]
    nblk = past // tk

    def body(jj, c):
        st = pl.multiple_of((nblk - 1 - jj) * tk, tk)
        kb = kc_ref[pl.ds(st, tk), :].astype(BF16)
        vb = vc_ref[pl.ds(st, tk), :].astype(BF16)
        _fox_tile(q, kb, vb, c0 - ckp_ref[:, pl.ds(st, tk)], None, m_ref, l_ref, acc_ref)
        return c

    lax.fori_loop(0, nblk, body, 0)
    o_ref[...] = (acc_ref[...] / l_ref[...]).astype(o_ref.dtype)


def _fox_sample(qkv, cache_k, cache_v, ck_past, ck_local, layer, n_heads, seq, tk):
    n = qkv.shape[0]
    nb = n // seq
    past = cache_k.shape[2]
    cache_spec = pl.BlockSpec((None, None, past, HEAD_DIM), lambda b, h: (layer, b, 0, h))
    return pl.pallas_call(
        functools.partial(_fox_sample_kernel, tk=tk),
        grid=(nb, n_heads),
        in_specs=[
            pl.BlockSpec((seq, HEAD_DIM), lambda b, h: (b, 3 * n_heads + h)),
            pl.BlockSpec((seq, HEAD_DIM), lambda b, h: (b, 4 * n_heads + h)),
            pl.BlockSpec((seq, HEAD_DIM), lambda b, h: (b, 5 * n_heads + h)),
            cache_spec, cache_spec,
            pl.BlockSpec((None, None, 1, past), lambda b, h: (b, h, 0, 0)),
            pl.BlockSpec((None, None, 1, seq), lambda b, h: (b, h, 0, 0)),
        ],
        out_specs=pl.BlockSpec((seq, HEAD_DIM), lambda b, h: (b, h)),
        out_shape=jax.ShapeDtypeStruct((n, n_heads * HEAD_DIM), BF16),
        scratch_shapes=[pltpu.VMEM((seq, 1), F32), pltpu.VMEM((seq, 1), F32), pltpu.VMEM((seq, HEAD_DIM), F32)],
        compiler_params=_params(("arbitrary", "arbitrary")),
        name="fox_sample",
    )(qkv, qkv, qkv, cache_k, cache_v, ck_past, ck_local)


def _out_proj_kernel(ya_ref, ysb_ref, yfox_ref, w_ref, x_ref, g_ref, xo_ref, h_ref):
    c = ya_ref.shape[1]
    s = ysb_ref.shape[1]
    acc = _dot(ya_ref[...], w_ref[0:c, :])
    acc += _dot(ysb_ref[...], w_ref[c:c + s, :])
    acc += _dot(yfox_ref[...], w_ref[c + s:, :])
    x = x_ref[...] + acc
    xo_ref[...] = x
    h_ref[...] = _rmsnorm_f32(x, g_ref[...]).astype(h_ref.dtype)


def _out_proj(ya, ysb, yfox, w, x, g, tm):
    m, d = x.shape
    row = lambda width: pl.BlockSpec((tm, width), lambda i: (i, 0))
    return pl.pallas_call(
        _out_proj_kernel,
        grid=(m // tm,),
        in_specs=[row(ya.shape[1]), row(ysb.shape[1]), row(yfox.shape[1]),
                  pl.BlockSpec(w.shape, lambda i: (0, 0)), row(d), pl.BlockSpec((1, d), lambda i: (0, 0))],
        out_specs=[row(d), row(d)],
        out_shape=[jax.ShapeDtypeStruct((m, d), F32), jax.ShapeDtypeStruct((m, d), BF16)],
        compiler_params=_params(("arbitrary",)),
        name="out_proj",
    )(ya, ysb, yfox, w, x, g.reshape(1, d))


def _ffn_kernel(h_ref, wg_ref, wu_ref, wd_ref, x_ref, g_ref, *rest, emit_x):
    out_refs, acc_ref = rest[:-1], rest[-1]
    f = pl.program_id(1)

    @pl.when(f == 0)
    def _():
        acc_ref[...] = jnp.zeros_like(acc_ref)

    h = h_ref[...]
    gate = _dot(h, wg_ref[...])
    up = _dot(h, wu_ref[...])
    act = gate * jax.nn.sigmoid(gate) * up
    acc_ref[...] += _dot(act.astype(BF16), wd_ref[...])

    @pl.when(f == pl.num_programs(1) - 1)
    def _():
        x = x_ref[...] + acc_ref[...]
        normed = _rmsnorm_f32(x, g_ref[...])
        if emit_x:
            out_refs[0][...] = x
            out_refs[1][...] = normed.astype(out_refs[1].dtype)
        else:
            out_refs[0][...] = normed


def _ffn(h, wg, wu, wd, x, g, tm, tf, emit_x):
    m, d = x.shape
    ff = wg.shape[1]
    row = pl.BlockSpec((tm, d), lambda i, f: (i, 0))
    if emit_x:
        out_specs = [row, row]
        out_shape = [jax.ShapeDtypeStruct((m, d), F32), jax.ShapeDtypeStruct((m, d), BF16)]
    else:
        out_specs = [row]
        out_shape = [jax.ShapeDtypeStruct((m, d), F32)]
    return pl.pallas_call(
        functools.partial(_ffn_kernel, emit_x=emit_x),
        grid=(m // tm, ff // tf),
        in_specs=[row,
                  pl.BlockSpec((d, tf), lambda i, f: (0, f)),
                  pl.BlockSpec((d, tf), lambda i, f: (0, f)),
                  pl.BlockSpec((tf, d), lambda i, f: (f, 0)),
                  row,
                  pl.BlockSpec((1, d), lambda i, f: (0, 0))],
        out_specs=out_specs,
        out_shape=out_shape,
        scratch_shapes=[pltpu.VMEM((tm, d), F32)],
        compiler_params=_params(("arbitrary", "arbitrary")),
        name="ffn",
    )(h, wg, wu, wd, x, g.reshape(1, d))


def kernel(x_prompt, x_sample, state_conv, cache_sb_k, cache_sb_v, cache_fox_k, cache_fox_v, cache_fox_logf,
           norm1_g, w_in, b_f, conv_w, w_out, norm2_g, w_gate, w_up, w_down, final_g):
    depth = w_in.shape[0]
    bp, t, d = x_prompt.shape
    nb, seq, _ = x_sample.shape
    assert bp == 1, "prompt kernels carry conv rows and forget sums across row tiles of one stream"
    c = conv_w.shape[2]
    n_sb = cache_sb_k.shape[3]
    n_fox = cache_fox_k.shape[3]
    assert n_sb == n_fox and n_fox <= 8
    past = cache_sb_k.shape[2]
    sb_dim = n_sb * HEAD_DIM
    qkv0 = 3 * c
    qkv1 = qkv0 + 3 * sb_dim + 3 * n_fox * HEAD_DIM
    n_s = nb * seq

    xp = x_prompt.reshape(t, d)
    xs = x_sample.reshape(n_s, d)
    csk = cache_sb_k.reshape(depth, nb, past, sb_dim)
    csv = cache_sb_v.reshape(depth, nb, past, sb_dim)
    cfk = cache_fox_k.reshape(depth, nb, past, sb_dim)
    cfv = cache_fox_v.reshape(depth, nb, past, sb_dim)

    hp = _rmsnorm(xp, norm1_g[0], 512)
    hs = _rmsnorm(xs, norm1_g[0], n_s)
    p_new, s_new = [], []
    for l in range(depth):
        wl = w_in[l]
        wa = wl[:, :qkv0].astype(BF16)
        wqkv = wl[:, qkv0:qkv1].astype(BF16)
        wf = jnp.zeros((16, d), BF16).at[:n_fox].set(wl[:, qkv1:].T.astype(BF16))
        bfp = jnp.zeros((8, 1), F32).at[:n_fox, 0].set(b_f[l])
        wo = w_out[l].astype(BF16)
        wg = w_gate[l].astype(BF16)
        wu = w_up[l].astype(BF16)
        wd = w_down[l].astype(BF16)
        last = l == depth - 1
        g_next = final_g if last else norm1_g[l + 1]

        ya, conv_p, logf_p, ck_p = _mix_a_prompt(hp, wa, wf, bfp, conv_w[l], 512)
        qkv, sbk, sbv, fk, fv = _qkv_proj(hp, wqkv, 1024)
        ysb = _sb_prompt(qkv, n_sb, 256)
        yfox = _fox_prompt(qkv, ck_p, n_fox, 256)
        xp, h2 = _out_proj(ya, ysb, yfox, wo, xp, norm2_g[l], 512)
        outs = _ffn(h2, wg, wu, wd, xp, g_next, 512, 512, not last)
        if last:
            y_prompt = outs[0]
        else:
            xp, hp = outs
        p_new.append((conv_p, sbk, sbv, fk, fv, logf_p[:n_fox].T))

        logf_past = jnp.transpose(cache_fox_logf[l], (0, 2, 1)).reshape(nb * n_fox, past)
        ck_past = _cumsum_rows(logf_past).reshape(nb, n_fox, 1, past)
        ya, conv_s, logf_s, ckl = _mix_a_sample(hs, wa, wf, bfp, conv_w[l],
                                                state_conv[l].reshape(nb * (CONV_W - 1), c), seq)
        ckl = jnp.transpose(ckl[:n_fox].reshape(n_fox, nb, 1, seq), (1, 0, 2, 3))
        qkv, sbk, sbv, fk, fv = _qkv_proj(hs, wqkv, n_s)
        ysb = _sb_sample(qkv, csk, csv, l, n_sb, seq, 256)
        yfox = _fox_sample(qkv, cfk, cfv, ck_past, ckl, l, n_fox, seq, 256)
        xs, h2 = _out_proj(ya, ysb, yfox, wo, xs, norm2_g[l], n_s)
        outs = _ffn(h2, wg, wu, wd, xs, g_next, n_s, 512, not last)
        if last:
            y_sample = outs[0]
        else:
            xs, hs = outs
        logf_s = jnp.transpose(logf_s[:n_fox].reshape(n_fox, nb, seq), (1, 2, 0))
        s_new.append((conv_s, sbk, sbv, fk, fv, logf_s))

    stack = lambda news, i, shape: jnp.stack([n[i] for n in news]).reshape((depth,) + shape)
    return (
        y_prompt.reshape(bp, t, d),
        y_sample.reshape(nb, seq, d),
        stack(p_new, 0, (bp, CONV_W - 1, c)),
        stack(p_new, 1, (bp, t, n_sb, HEAD_DIM)), stack(p_new, 2, (bp, t, n_sb, HEAD_DIM)),
        stack(p_new, 3, (bp, t, n_fox, HEAD_DIM)), stack(p_new, 4, (bp, t, n_fox, HEAD_DIM)),
        stack(p_new, 5, (bp, t, n_fox)),
        stack(s_new, 0, (nb, CONV_W - 1, c)),
        stack(s_new, 1, (nb, seq, n_sb, HEAD_DIM)), stack(s_new, 2, (nb, seq, n_sb, HEAD_DIM)),
        stack(s_new, 3, (nb, seq, n_fox, HEAD_DIM)), stack(s_new, 4, (nb, seq, n_fox, HEAD_DIM)),
        stack(s_new, 5, (nb, seq, n_fox)),
    )
```

```python
import functools

import jax
import jax.numpy as jnp
from jax import lax
from jax.experimental import pallas as pl
from jax.experimental.pallas import tpu as pltpu

F32 = jnp.float32
BF16 = jnp.bfloat16

NORM_EPS = 1e-5
HEAD_DIM = 128
CONV_W = 3
LANES = 128
VMEM_LIMIT = 56 * 1024 * 1024
NEG_BIG = -1e30
LOG2E = 1.4426950408889634
QK_SCALE = HEAD_DIM ** -0.5 * LOG2E


def _params(sem, vmem=VMEM_LIMIT):
    return pltpu.CompilerParams(dimension_semantics=sem, vmem_limit_bytes=vmem)


def _dot(a, b):
    return jnp.dot(a, b, preferred_element_type=F32)


def _dot_nt(a, b):
    return lax.dot_general(a, b, (((1,), (1,)), ((), ())), preferred_element_type=F32)


def _neg_abs(x):
    return pltpu.bitcast(pltpu.bitcast(x, jnp.uint32) | jnp.uint32(0x80000000), F32)


def _softplus2(z2):
    return jnp.maximum(z2, 0.0) + jnp.log(1.0 + jnp.exp2(_neg_abs(z2))) * LOG2E


def _log_sigmoid(x):
    return jnp.minimum(x, 0.0) - jnp.log(1.0 + jnp.exp(-jnp.abs(x)))


def _split_bf16(x, parts):
    out = []
    rem = x
    for p in range(parts):
        hi = rem.astype(BF16)
        out.append(hi)
        if p + 1 < parts:
            rem = rem - hi.astype(F32)
    return out


def _iota2(n, m):
    return lax.broadcasted_iota(jnp.int32, (n, m), 0), lax.broadcasted_iota(jnp.int32, (n, m), 1)


def _tri_incl(n, seg=None):
    j, s = _iota2(n, n)
    t = jnp.where(j <= s, 1.0, 0.0)
    if seg is not None:
        t = jnp.where(j // seg == s // seg, t, 0.0)
    return t.astype(BF16)


def _suffix_incl(n):
    j, s = _iota2(n, n)
    return jnp.where(j >= s, 1.0, 0.0).astype(BF16)


def _prefix_sum_lanes(x, tri, carry):
    outs = []
    for c in range(x.shape[1] // LANES):
        xc = x[:, c * LANES:(c + 1) * LANES]
        loc = sum(_dot(p, tri) for p in _split_bf16(xc, 3))
        oc = loc if carry is None else loc + carry
        outs.append(oc)
        if carry is not None:
            carry = oc[:, LANES - 1:LANES]
    return jnp.concatenate(outs, axis=1), carry


def _rmsnorm_f32(x, g):
    ms = jnp.mean(x * x, axis=-1, keepdims=True)
    return x * lax.rsqrt(ms + NORM_EPS) * g


def _rmsnorm_kernel(x_ref, g_ref, o_ref):
    o_ref[...] = _rmsnorm_f32(x_ref[...], g_ref[...]).astype(o_ref.dtype)


def _rmsnorm(x, g, tm):
    m, d = x.shape
    return pl.pallas_call(
        _rmsnorm_kernel,
        grid=(m // tm,),
        in_specs=[pl.BlockSpec((tm, d), lambda i: (i, 0)), pl.BlockSpec((1, d), lambda i: (0, 0))],
        out_specs=pl.BlockSpec((tm, d), lambda i: (i, 0)),
        out_shape=jax.ShapeDtypeStruct((m, d), BF16),
        compiler_params=_params(("arbitrary",)),
        name="rmsnorm",
    )(x, g.reshape(1, d))


def _gated_conv(proj, cw, u_m1, u_m2):
    c = proj.shape[1] // 3
    g_b, g_c, hc = proj[:, :c], proj[:, c:2 * c], proj[:, 2 * c:]
    u = g_c * hc
    r2 = pltpu.roll(u, 2, 0)
    u1 = u_m1(pltpu.roll(u, 1, 0))
    u2 = u_m2(r2)
    conv = cw[0:1, :] * u2 + cw[1:2, :] * u1 + cw[2:3, :] * u
    return g_b * conv, u, r2


def _mix_a_prompt_kernel(h_ref, wa_ref, wf_ref, bf_ref, cw_ref,
                         ya_ref, conv_ref, logf_ref, ck_ref, uprev_ref, carry_ref):
    i = pl.program_id(0)

    @pl.when(i == 0)
    def _():
        uprev_ref[...] = jnp.zeros_like(uprev_ref)
        carry_ref[...] = jnp.zeros_like(carry_ref)

    h = h_ref[...]
    proj = _dot(h, wa_ref[...])
    tm = proj.shape[0]
    rows = lax.broadcasted_iota(jnp.int32, (tm, proj.shape[1] // 3), 0)
    prev = uprev_ref[...]
    m1 = lambda r1: jnp.where(rows == 0, prev[1:2, :], r1)
    m2 = lambda r2: jnp.where(rows == 0, prev[0:1, :], jnp.where(rows == 1, prev[1:2, :], r2))
    ya, _, r2 = _gated_conv(proj, cw_ref[...], m1, m2)
    ya_ref[...] = ya.astype(ya_ref.dtype)
    tail = r2[0:2, :]
    uprev_ref[0:2, :] = tail
    conv_ref[...] = tail

    logf = _log_sigmoid(_dot(h, wf_ref[...]) + bf_ref[...])
    logf_ref[...] = logf
    lower = _suffix_incl(tm)
    ck = sum(_dot(lower, p) for p in _split_bf16(logf, 3)) + carry_ref[...]
    ck_ref[...] = ck
    carry_ref[...] = ck[tm - 1:tm, :]


def _mix_a_prompt(h, wa, wf, bfp, cw, tm):
    t, d = h.shape
    c3 = wa.shape[1]
    c = c3 // 3
    return pl.pallas_call(
        _mix_a_prompt_kernel,
        grid=(t // tm,),
        in_specs=[
            pl.BlockSpec((tm, d), lambda i: (i, 0)),
            pl.BlockSpec((d, c3), lambda i: (0, 0)),
            pl.BlockSpec((d, LANES), lambda i: (0, 0)),
            pl.BlockSpec((1, LANES), lambda i: (0, 0)),
            pl.BlockSpec((CONV_W, c), lambda i: (0, 0)),
        ],
        out_specs=[
            pl.BlockSpec((tm, c), lambda i: (i, 0)),
            pl.BlockSpec((CONV_W - 1, c), lambda i: (0, 0)),
            pl.BlockSpec((tm, LANES), lambda i: (i, 0)),
            pl.BlockSpec((tm, LANES), lambda i: (i, 0)),
        ],
        out_shape=[
            jax.ShapeDtypeStruct((t, c), BF16),
            jax.ShapeDtypeStruct((CONV_W - 1, c), F32),
            jax.ShapeDtypeStruct((t, LANES), F32),
            jax.ShapeDtypeStruct((t, LANES), F32),
        ],
        scratch_shapes=[pltpu.VMEM((8, c), F32), pltpu.VMEM((1, LANES), F32)],
        compiler_params=_params(("arbitrary",)),
        name="mix_a_prompt",
    )(h, wa, wf, bfp, cw)


def _mix_a_sample_kernel(h_ref, wa_ref, wf_ref, bf_ref, cw_ref, st_ref,
                         ya_ref, conv_ref, logf_ref, ckl_ref, u_ref, *, seq):
    h = h_ref[...]
    proj = _dot(h, wa_ref[...])
    n = proj.shape[0]
    nb = n // seq
    rows = lax.broadcasted_iota(jnp.int32, (n, proj.shape[1] // 3), 0)

    def m1(r1):
        for b in range(nb):
            r1 = jnp.where(rows == b * seq, st_ref[2 * b + 1:2 * b + 2, :], r1)
        return r1

    def m2(r2):
        for b in range(nb):
            r2 = jnp.where(rows == b * seq, st_ref[2 * b:2 * b + 1, :], r2)
            r2 = jnp.where(rows == b * seq + 1, st_ref[2 * b + 1:2 * b + 2, :], r2)
        return r2

    ya, u, _ = _gated_conv(proj, cw_ref[...], m1, m2)
    ya_ref[...] = ya.astype(ya_ref.dtype)
    u_ref[...] = u
    for b in range(nb):
        conv_ref[2 * b:2 * b + 2, :] = u_ref[(b + 1) * seq - 2:(b + 1) * seq, :]

    logit = _dot_nt(wf_ref[...], h)[0:8, :]
    logf = _log_sigmoid(logit + bf_ref[...])
    logf_ref[...] = logf
    ckl, _ = _prefix_sum_lanes(logf, _tri_incl(LANES, seg=seq), None)
    ckl_ref[...] = ckl


def _mix_a_sample(h, wa, wf, bfp, cw, state, seq):
    n, d = h.shape
    c3 = wa.shape[1]
    c = c3 // 3
    nb = n // seq
    full = lambda shape: pl.BlockSpec(shape, lambda i: tuple(0 for _ in shape))
    return pl.pallas_call(
        functools.partial(_mix_a_sample_kernel, seq=seq),
        grid=(1,),
        in_specs=[full((n, d)), full((d, c3)), full((16, d)), full((8, 1)), full((CONV_W, c)),
                  full((nb * (CONV_W - 1), c))],
        out_specs=[full((n, c)), full((nb * (CONV_W - 1), c)), full((8, n)), full((8, n))],
        out_shape=[
            jax.ShapeDtypeStruct((n, c), BF16),
            jax.ShapeDtypeStruct((nb * (CONV_W - 1), c), F32),
            jax.ShapeDtypeStruct((8, n), F32),
            jax.ShapeDtypeStruct((8, n), F32),
        ],
        scratch_shapes=[pltpu.VMEM((n, c), F32)],
        compiler_params=_params(("arbitrary",)),
        name="mix_a_sample",
    )(h, wa, wf, bfp, cw, state)


def _cumsum_rows_kernel(x_ref, o_ref):
    ck, _ = _prefix_sum_lanes(x_ref[...], _tri_incl(LANES), jnp.zeros((x_ref.shape[0], 1), F32))
    o_ref[...] = ck


def _cumsum_rows(x):
    return pl.pallas_call(
        _cumsum_rows_kernel,
        grid=(1,),
        in_specs=[pl.BlockSpec(x.shape, lambda i: (0, 0))],
        out_specs=pl.BlockSpec(x.shape, lambda i: (0, 0)),
        out_shape=jax.ShapeDtypeStruct(x.shape, F32),
        compiler_params=_params(("arbitrary",)),
        name="cumsum_rows",
    )(x)


def _store_heads(ref, acc):
    n_heads = ref.shape[-2]
    for hd in range(n_heads):
        part = acc[:, hd * HEAD_DIM:(hd + 1) * HEAD_DIM]
        if len(ref.shape) == 3:
            ref[:, hd, :] = part
        else:
            seq = ref.shape[1]
            for b in range(ref.shape[0]):
                ref[b, :, hd, :] = part[b * seq:(b + 1) * seq, :]


def _qkv_kernel(*refs, n_prev):
    h_ref, w_ref = refs[:2]
    qkv_ref, sbk_ref, sbv_ref, fk_ref, fv_ref = refs[2 + n_prev:]
    j = pl.program_id(1)
    acc = _dot(h_ref[...], w_ref[...])
    is_q = jnp.logical_or(j == 0, j == 3)
    qkv_ref[...] = (acc * jnp.where(is_q, QK_SCALE, 1.0)).astype(qkv_ref.dtype)
    for idx, ref in ((1, sbk_ref), (2, sbv_ref), (4, fk_ref), (5, fv_ref)):
        @pl.when(j == idx)
        def _(ref=ref):
            _store_heads(ref, acc)


def _qkv_proj(h, w, tm, layer, cache_shape, prev):
    m, d = h.shape
    n = w.shape[1]
    tn = n // 6
    _, batch, seq, n_heads, _ = cache_shape
    if m // tm == 1:
        cache_spec = pl.BlockSpec((None, batch, seq, n_heads, HEAD_DIM), lambda i, j: (layer, 0, 0, 0, 0))
    else:
        assert batch == 1
        cache_spec = pl.BlockSpec((None, None, tm, n_heads, HEAD_DIM), lambda i, j: (layer, 0, i, 0, 0))
    cache = jax.ShapeDtypeStruct(cache_shape, F32)
    prev = () if prev is None else tuple(prev)
    return pl.pallas_call(
        functools.partial(_qkv_kernel, n_prev=len(prev)),
        grid=(m // tm, 6),
        in_specs=[pl.BlockSpec((tm, d), lambda i, j: (i, 0)), pl.BlockSpec((d, tn), lambda i, j: (0, j))]
        + [pl.BlockSpec(memory_space=pl.ANY)] * len(prev),
        out_specs=[pl.BlockSpec((tm, tn), lambda i, j: (i, j))] + [cache_spec] * 4,
        out_shape=[jax.ShapeDtypeStruct((m, n), BF16)] + [cache] * 4,
        input_output_aliases={2 + k: 1 + k for k in range(len(prev))},
        compiler_params=_params(("arbitrary", "arbitrary")),
        name="qkv_proj",
    )(h, w, *prev)


def _sb_tile(q, kb, vb, suffix, mask):
    z = _dot_nt(q, kb)
    sp = _softplus2(z)
    if mask is not None:
        sp = jnp.where(mask, sp, 0.0)
    s_incl = _dot(sp.astype(BF16), suffix)
    w = jnp.exp2(z - s_incl)
    if mask is not None:
        w = jnp.where(mask, w, 0.0)
    return _dot(w.astype(BF16), vb), s_incl[:, 0:1]


def _sb_accumulate(acc_ref, r_ref, pv, tot):
    r = r_ref[...]
    acc_ref[...] += jnp.exp2(-r) * pv
    r_ref[...] = r + tot


def _for_past_tiles(n_tiles, pair_fn, single_fn):
    def body(jj, c):
        pair_fn(n_tiles - 1 - 2 * jj)
        return c

    lax.fori_loop(0, n_tiles // 2, body, 0)

    @pl.when(n_tiles % 2 == 1)
    def _():
        single_fn(0)


def _sb_prompt_kernel(q_ref, k_ref, v_ref, o_ref, acc_ref, r_ref, *, tq):
    qi = pl.program_id(1)
    q = q_ref[...]
    acc_ref[...] = jnp.zeros_like(acc_ref)
    r_ref[...] = jnp.zeros_like(r_ref)
    suffix = _suffix_incl(tq)
    row, col = _iota2(tq, tq)

    def kv(j):
        st = pl.multiple_of(j * tq, tq)
        return k_ref[pl.ds(st, tq), :], v_ref[pl.ds(st, tq), :]

    _sb_accumulate(acc_ref, r_ref, *_sb_tile(q, *kv(qi), suffix, col < row))

    def single(j):
        _sb_accumulate(acc_ref, r_ref, *_sb_tile(q, *kv(j), suffix, None))

    def pair(j):
        pv_a, tot_a = _sb_tile(q, *kv(j), suffix, None)
        pv_b, tot_b = _sb_tile(q, *kv(j - 1), suffix, None)
        _sb_accumulate(acc_ref, r_ref, pv_a + jnp.exp2(-tot_a) * pv_b, tot_a + tot_b)

    _for_past_tiles(qi, pair, single)
    o_ref[...] = acc_ref[...].astype(o_ref.dtype)


def _sb_prompt(qkv, n_heads, tq):
    t = qkv.shape[0]
    return pl.pallas_call(
        functools.partial(_sb_prompt_kernel, tq=tq),
        grid=(n_heads, t // tq),
        in_specs=[
            pl.BlockSpec((tq, HEAD_DIM), lambda h, i: (i, h)),
            pl.BlockSpec((t, HEAD_DIM), lambda h, i: (0, n_heads + h)),
            pl.BlockSpec((t, HEAD_DIM), lambda h, i: (0, 2 * n_heads + h)),
        ],
        out_specs=pl.BlockSpec((tq, HEAD_DIM), lambda h, i: (i, h)),
        out_shape=jax.ShapeDtypeStruct((t, n_heads * HEAD_DIM), BF16),
        scratch_shapes=[pltpu.VMEM((tq, HEAD_DIM), F32), pltpu.VMEM((tq, HEAD_DIM), F32)],
        compiler_params=_params(("arbitrary", "arbitrary")),
        name="sb_prompt",
    )(qkv, qkv, qkv)


def _head_cols(hd):
    return slice(hd * HEAD_DIM, (hd + 1) * HEAD_DIM)


def _sb_sample_kernel(q_ref, kn_ref, vn_ref, kc_ref, vc_ref, o_ref, acc_ref, r_ref, *, tk):
    ci = pl.program_id(1)
    seq = q_ref.shape[0]
    tkc, n_heads = kc_ref.shape[0], kc_ref.shape[1]

    @pl.when(ci == 0)
    def _():
        acc_ref[...] = jnp.zeros_like(acc_ref)
        r_ref[...] = jnp.zeros_like(r_ref)
        row, col = _iota2(seq, seq)
        suffix = _suffix_incl(seq)
        for hd in range(n_heads):
            cols = _head_cols(hd)
            _sb_accumulate(acc_ref.at[hd], r_ref.at[hd],
                           *_sb_tile(q_ref[:, cols], kn_ref[:, cols], vn_ref[:, cols], suffix, col < row))

    suffix = _suffix_incl(tk)
    nsub = tkc // tk
    for hd in range(n_heads):
        q = q_ref[:, _head_cols(hd)]

        def body(jj, c, hd=hd, q=q):
            st = pl.multiple_of((nsub - 1 - jj) * tk, tk)
            kb = kc_ref[pl.ds(st, tk), hd, :].astype(BF16)
            vb = vc_ref[pl.ds(st, tk), hd, :].astype(BF16)
            _sb_accumulate(acc_ref.at[hd], r_ref.at[hd], *_sb_tile(q, kb, vb, suffix, None))
            return c

        lax.fori_loop(0, nsub, body, 0)

    @pl.when(ci == pl.num_programs(1) - 1)
    def _():
        for hd in range(n_heads):
            o_ref[:, _head_cols(hd)] = acc_ref[hd].astype(o_ref.dtype)


def _sb_sample(qkv, cache_k, cache_v, layer, seq, tkc, tk):
    n = qkv.shape[0]
    nb = n // seq
    _, _, past, n_heads, _ = cache_k.shape
    nch = past // tkc
    width = n_heads * HEAD_DIM
    cache_spec = pl.BlockSpec((None, None, tkc, n_heads, HEAD_DIM), lambda b, c: (layer, b, nch - 1 - c, 0, 0))
    return pl.pallas_call(
        functools.partial(_sb_sample_kernel, tk=tk),
        grid=(nb, nch),
        in_specs=[
            pl.BlockSpec((seq, width), lambda b, c: (b, 0)),
            pl.BlockSpec((seq, width), lambda b, c: (b, 1)),
            pl.BlockSpec((seq, width), lambda b, c: (b, 2)),
            cache_spec, cache_spec,
        ],
        out_specs=pl.BlockSpec((seq, width), lambda b, c: (b, 0)),
        out_shape=jax.ShapeDtypeStruct((n, width), BF16),
        scratch_shapes=[pltpu.VMEM((n_heads, seq, HEAD_DIM), F32), pltpu.VMEM((n_heads, seq, HEAD_DIM), F32)],
        compiler_params=_params(("arbitrary", "arbitrary")),
        name="sb_sample",
    )(qkv, qkv, qkv, cache_k, cache_v)


def _fox_init(m_ref, l_ref, acc_ref):
    m_ref[...] = jnp.full(m_ref.shape, NEG_BIG, F32)
    l_ref[...] = jnp.zeros_like(l_ref)
    acc_ref[...] = jnp.zeros_like(acc_ref)


def _fox_tile_t(qt, kb, vt, bias, mask, m_ref, l_ref, acc_ref):
    s = _dot(kb, qt) + jnp.concatenate([bias] * (qt.shape[1] // LANES), axis=1)
    if mask is not None:
        s = jnp.where(mask, s, NEG_BIG)
    m_prev = m_ref[...]
    m_new = jnp.maximum(m_prev, jnp.max(s, axis=0, keepdims=True))
    alpha = jnp.exp2(m_prev - m_new)
    p = jnp.exp2(s - m_new)
    l_ref[...] = alpha * l_ref[...] + jnp.sum(p, axis=0, keepdims=True)
    acc_ref[...] = alpha * acc_ref[...] + _dot(vt, p.astype(BF16))
    m_ref[...] = m_new


def _fox_prompt_kernel(q_ref, k_ref, v_ref, ck_ref, o_ref, vt_ref, nb_ref, m_ref, l_ref, acc_ref, *, tq, chunk):
    h = pl.program_id(0)
    qi = pl.program_id(1)
    t = k_ref.shape[0]

    @pl.when(qi == 0)
    def _():
        sel_r, _ = _iota2(LANES, LANES)
        sel = jnp.where(sel_r == h, 1.0, 0.0).astype(BF16)

        def body(c, carry):
            st = pl.multiple_of(c * chunk, chunk)
            vt_ref[:, pl.ds(st, chunk)] = v_ref[pl.ds(st, chunk), :].astype(F32).T.astype(BF16)
            ck_h = sum(_dot(p, sel) for p in _split_bf16(ck_ref[pl.ds(st, chunk), :], 3))
            nb_ref[pl.ds(st, chunk), :] = ck_h * (-LOG2E)
            return carry

        lax.fori_loop(0, t // chunk, body, 0)

    qt = q_ref[...].astype(F32).T.astype(BF16)
    _fox_init(m_ref, l_ref, acc_ref)
    qs = pl.multiple_of(qi * tq, tq)
    c0 = nb_ref[pl.ds(qs, 1), :]

    def tile(st, tk, mask):
        bias = nb_ref[pl.ds(st, tk), :] - c0
        _fox_tile_t(qt, k_ref[pl.ds(st, tk), :], vt_ref[:, pl.ds(st, tk)], bias, mask, m_ref, l_ref, acc_ref)

    krow, qcol = _iota2(tq, tq)
    tile(qs, tq, krow <= qcol)
    _for_past_tiles(qi,
                    lambda j: tile(pl.multiple_of((j - 1) * tq, tq), 2 * tq, None),
                    lambda j: tile(pl.multiple_of(j * tq, tq), tq, None))
    o_ref[...] = (acc_ref[...] / l_ref[...]).T.astype(o_ref.dtype)


def _fox_prompt(qkv, ck, n_heads, tq):
    t = qkv.shape[0]
    return pl.pallas_call(
        functools.partial(_fox_prompt_kernel, tq=tq, chunk=512),
        grid=(n_heads, t // tq),
        in_specs=[
            pl.BlockSpec((tq, HEAD_DIM), lambda h, i: (i, 3 * n_heads + h)),
            pl.BlockSpec((t, HEAD_DIM), lambda h, i: (0, 4 * n_heads + h)),
            pl.BlockSpec((t, HEAD_DIM), lambda h, i: (0, 5 * n_heads + h)),
            pl.BlockSpec((t, LANES), lambda h, i: (0, 0)),
        ],
        out_specs=pl.BlockSpec((tq, HEAD_DIM), lambda h, i: (i, h)),
        out_shape=jax.ShapeDtypeStruct((t, n_heads * HEAD_DIM), BF16),
        scratch_shapes=[pltpu.VMEM((HEAD_DIM, t), BF16), pltpu.VMEM((t, LANES), F32),
                        pltpu.VMEM((1, tq), F32), pltpu.VMEM((1, tq), F32), pltpu.VMEM((HEAD_DIM, tq), F32)],
        compiler_params=_params(("arbitrary", "arbitrary")),
        name="fox_prompt",
    )(qkv, qkv, qkv, ck)


def _fox_tile(q, kb, vb, bias, mask, m_ref, l_ref, acc_ref):
    s = _dot_nt(q, kb) + bias
    if mask is not None:
        s = jnp.where(mask, s, NEG_BIG)
    m_prev = m_ref[...]
    m_new = jnp.maximum(m_prev, jnp.max(s, axis=1, keepdims=True))
    alpha = jnp.exp2(m_prev - m_new)
    p = jnp.exp2(s - m_new)
    l_ref[...] = alpha * l_ref[...] + jnp.sum(p, axis=1, keepdims=True)
    acc_ref[...] = alpha * acc_ref[...] + _dot(p.astype(BF16), vb)
    m_ref[...] = m_new


def _fox_sample_kernel(q_ref, kn_ref, vn_ref, kc_ref, vc_ref, ckp_ref, ckl_ref, o_ref,
                       m_ref, l_ref, acc_ref, *, tk):
    ci = pl.program_id(1)
    nch = pl.num_programs(1)
    seq = q_ref.shape[0]
    tkc, n_heads = kc_ref.shape[0], kc_ref.shape[1]
    past = ckp_ref.shape[2]

    @pl.when(ci == 0)
    def _():
        _fox_init(m_ref, l_ref, acc_ref)
        row, col = _iota2(seq, seq)
        for hd in range(n_heads):
            cols = _head_cols(hd)
            _fox_tile(q_ref[:, cols], kn_ref[:, cols], vn_ref[:, cols], ckl_ref[hd] * (-LOG2E), col <= row,
                      m_ref.at[hd], l_ref.at[hd], acc_ref.at[hd])

    nsub = tkc // tk
    base = (nch - 1 - ci) * tkc
    for hd in range(n_heads):
        q = q_ref[:, _head_cols(hd)]
        c0 = ckp_ref[hd, :, past - 1:past]

        def body(jj, c, hd=hd, q=q, c0=c0):
            st = pl.multiple_of((nsub - 1 - jj) * tk, tk)
            kb = kc_ref[pl.ds(st, tk), hd, :].astype(BF16)
            vb = vc_ref[pl.ds(st, tk), hd, :].astype(BF16)
            bias = (c0 - ckp_ref[hd, :, pl.ds(pl.multiple_of(base + st, tk), tk)]) * LOG2E
            _fox_tile(q, kb, vb, bias, None, m_ref.at[hd], l_ref.at[hd], acc_ref.at[hd])
            return c

        lax.fori_loop(0, nsub, body, 0)

    @pl.when(ci == nch - 1)
    def _():
        for hd in range(n_heads):
            o_ref[:, _head_cols(hd)] = (acc_ref[hd] / l_ref[hd]).astype(o_ref.dtype)


def _fox_sample(qkv, cache_k, cache_v, ck_past, ck_local, layer, seq, tkc, tk):
    n = qkv.shape[0]
    nb = n // seq
    _, _, past, n_heads, _ = cache_k.shape
    nch = past // tkc
    width = n_heads * HEAD_DIM
    cache_spec = pl.BlockSpec((None, None, tkc, n_heads, HEAD_DIM), lambda b, c: (layer, b, nch - 1 - c, 0, 0))
    return pl.pallas_call(
        functools.partial(_fox_sample_kernel, tk=tk),
        grid=(nb, nch),
        in_specs=[
            pl.BlockSpec((seq, width), lambda b, c: (b, 3)),
            pl.BlockSpec((seq, width), lambda b, c: (b, 4)),
            pl.BlockSpec((seq, width), lambda b, c: (b, 5)),
            cache_spec, cache_spec,
            pl.BlockSpec((None, n_heads, 1, past), lambda b, c: (b, 0, 0, 0)),
            pl.BlockSpec((None, n_heads, 1, seq), lambda b, c: (b, 0, 0, 0)),
        ],
        out_specs=pl.BlockSpec((seq, width), lambda b, c: (b, 0)),
        out_shape=jax.ShapeDtypeStruct((n, width), BF16),
        scratch_shapes=[pltpu.VMEM((n_heads, seq, 1), F32), pltpu.VMEM((n_heads, seq, 1), F32),
                        pltpu.VMEM((n_heads, seq, HEAD_DIM), F32)],
        compiler_params=_params(("arbitrary", "arbitrary")),
        name="fox_sample",
    )(qkv, qkv, qkv, cache_k, cache_v, ck_past, ck_local)


def _out_proj_kernel(ya_ref, ysb_ref, yfox_ref, w_ref, x_ref, g_ref, xo_ref, h_ref):
    c = ya_ref.shape[1]
    s = ysb_ref.shape[1]
    acc = _dot(ya_ref[...], w_ref[0:c, :])
    acc += _dot(ysb_ref[...], w_ref[c:c + s, :])
    acc += _dot(yfox_ref[...], w_ref[c + s:, :])
    x = x_ref[...] + acc
    xo_ref[...] = x
    h_ref[...] = _rmsnorm_f32(x, g_ref[...]).astype(h_ref.dtype)


def _out_proj(ya, ysb, yfox, w, x, g, tm):
    m, d = x.shape
    row = lambda width: pl.BlockSpec((tm, width), lambda i: (i, 0))
    return pl.pallas_call(
        _out_proj_kernel,
        grid=(m // tm,),
        in_specs=[row(ya.shape[1]), row(ysb.shape[1]), row(yfox.shape[1]),
                  pl.BlockSpec(w.shape, lambda i: (0, 0)), row(d), pl.BlockSpec((1, d), lambda i: (0, 0))],
        out_specs=[row(d), row(d)],
        out_shape=[jax.ShapeDtypeStruct((m, d), F32), jax.ShapeDtypeStruct((m, d), BF16)],
        compiler_params=_params(("arbitrary",)),
        name="out_proj",
    )(ya, ysb, yfox, w, x, g.reshape(1, d))


def _ffn_kernel(h_ref, wg_ref, wu_ref, wd_ref, x_ref, g_ref, *rest, emit_x):
    out_refs, acc_ref = rest[:-1], rest[-1]
    f = pl.program_id(1)

    @pl.when(f == 0)
    def _():
        acc_ref[...] = jnp.zeros_like(acc_ref)

    h = h_ref[...]
    gate = _dot(h, wg_ref[...])
    up = _dot(h, wu_ref[...])
    act = gate * jax.nn.sigmoid(gate) * up
    acc_ref[...] += _dot(act.astype(BF16), wd_ref[...])

    @pl.when(f == pl.num_programs(1) - 1)
    def _():
        x = x_ref[...] + acc_ref[...]
        normed = _rmsnorm_f32(x, g_ref[...])
        if emit_x:
            out_refs[0][...] = x
            out_refs[1][...] = normed.astype(out_refs[1].dtype)
        else:
            out_refs[0][...] = normed


def _ffn(h, wg, wu, wd, x, g, tm, tf, emit_x):
    m, d = x.shape
    ff = wg.shape[1]
    row = pl.BlockSpec((tm, d), lambda i, f: (i, 0))
    if emit_x:
        out_specs = [row, row]
        out_shape = [jax.ShapeDtypeStruct((m, d), F32), jax.ShapeDtypeStruct((m, d), BF16)]
    else:
        out_specs = [row]
        out_shape = [jax.ShapeDtypeStruct((m, d), F32)]
    return pl.pallas_call(
        functools.partial(_ffn_kernel, emit_x=emit_x),
        grid=(m // tm, ff // tf),
        in_specs=[row,
                  pl.BlockSpec((d, tf), lambda i, f: (0, f)),
                  pl.BlockSpec((d, tf), lambda i, f: (0, f)),
                  pl.BlockSpec((tf, d), lambda i, f: (f, 0)),
                  row,
                  pl.BlockSpec((1, d), lambda i, f: (0, 0))],
        out_specs=out_specs,
        out_shape=out_shape,
        scratch_shapes=[pltpu.VMEM((tm, d), F32)],
        compiler_params=_params(("arbitrary", "arbitrary")),
        name="ffn",
    )(h, wg, wu, wd, x, g.reshape(1, d))


def kernel(x_prompt, x_sample, state_conv, cache_sb_k, cache_sb_v, cache_fox_k, cache_fox_v, cache_fox_logf,
           norm1_g, w_in, b_f, conv_w, w_out, norm2_g, w_gate, w_up, w_down, final_g):
    depth = w_in.shape[0]
    bp, t, d = x_prompt.shape
    nb, seq, _ = x_sample.shape
    assert bp == 1, "prompt kernels carry conv rows and forget sums across row tiles of one stream"
    c = conv_w.shape[2]
    n_sb = cache_sb_k.shape[3]
    n_fox = cache_fox_k.shape[3]
    assert n_sb == n_fox and n_fox <= 8
    past = cache_sb_k.shape[2]
    sb_dim = n_sb * HEAD_DIM
    qkv0 = 3 * c
    qkv1 = qkv0 + 3 * sb_dim + 3 * n_fox * HEAD_DIM
    n_s = nb * seq
    p_cache_shape = (depth, bp, t, n_sb, HEAD_DIM)
    s_cache_shape = (depth, nb, seq, n_sb, HEAD_DIM)

    xp = x_prompt.reshape(t, d)
    xs = x_sample.reshape(n_s, d)
    hp = _rmsnorm(xp, norm1_g[0], 512)
    hs = _rmsnorm(xs, norm1_g[0], n_s)
    p_small, s_small = [], []
    p_caches = s_caches = None
    for l in range(depth):
        wl = w_in[l]
        wa = wl[:, :qkv0].astype(BF16)
        wqkv = wl[:, qkv0:qkv1].astype(BF16)
        w_tail = wl[:, qkv1:].astype(BF16)
        wf_cols = jnp.pad(w_tail, ((0, 0), (0, LANES - n_fox)))
        wf_rows = jnp.pad(w_tail.T, ((0, 16 - n_fox), (0, 0)))
        bf_row = jnp.pad(b_f[l], (0, LANES - n_fox)).reshape(1, LANES)
        bf_col = jnp.pad(b_f[l], (0, 8 - n_fox)).reshape(8, 1)
        wo = w_out[l].astype(BF16)
        wg = w_gate[l].astype(BF16)
        wu = w_up[l].astype(BF16)
        wd = w_down[l].astype(BF16)
        last = l == depth - 1
        g_next = final_g if last else norm1_g[l + 1]

        ya, conv_p, logf_p, ck_p = _mix_a_prompt(hp, wa, wf_cols, bf_row, conv_w[l], 512)
        qkv, *p_caches = _qkv_proj(hp, wqkv, 1024, l, p_cache_shape, p_caches)
        ysb = _sb_prompt(qkv, n_sb, 256)
        yfox = _fox_prompt(qkv, ck_p, n_fox, 256)
        xp, h2 = _out_proj(ya, ysb, yfox, wo, xp, norm2_g[l], 512)
        outs = _ffn(h2, wg, wu, wd, xp, g_next, 512, 512, not last)
        if last:
            y_prompt = outs[0]
        else:
            xp, hp = outs
        p_small.append((conv_p, logf_p[:, :n_fox]))

        logf_past = jnp.transpose(cache_fox_logf[l], (0, 2, 1)).reshape(nb * n_fox, past)
        ck_past = _cumsum_rows(logf_past).reshape(nb, n_fox, 1, past)
        ya, conv_s, logf_s, ckl = _mix_a_sample(hs, wa, wf_rows, bf_col, conv_w[l],
                                                state_conv[l].reshape(nb * (CONV_W - 1), c), seq)
        ckl = jnp.transpose(ckl[:n_fox].reshape(n_fox, nb, 1, seq), (1, 0, 2, 3))
        qkv, *s_caches = _qkv_proj(hs, wqkv, n_s, l, s_cache_shape, s_caches)
        ysb = _sb_sample(qkv, cache_sb_k, cache_sb_v, l, seq, 1024, 256)
        yfox = _fox_sample(qkv, cache_fox_k, cache_fox_v, ck_past, ckl, l, seq, 1024, 256)
        xs, h2 = _out_proj(ya, ysb, yfox, wo, xs, norm2_g[l], n_s)
        outs = _ffn(h2, wg, wu, wd, xs, g_next, n_s, 512, not last)
        if last:
            y_sample = outs[0]
        else:
            xs, hs = outs
        logf_s = jnp.transpose(logf_s[:n_fox].reshape(n_fox, nb, seq), (1, 2, 0))
        s_small.append((conv_s, logf_s))

    stack = lambda news, i, shape: jnp.stack([n[i] for n in news]).reshape((depth,) + shape)
    return (
        y_prompt.reshape(bp, t, d),
        y_sample.reshape(nb, seq, d),
        stack(p_small, 0, (bp, CONV_W - 1, c)),
        *p_caches,
        stack(p_small, 1, (bp, t, n_fox)),
        stack(s_small, 0, (nb, CONV_W - 1, c)),
        *s_caches,
        stack(s_small, 1, (nb, seq, n_fox)),
    )
```

```python
import functools

import jax
import jax.numpy as jnp
from jax import lax
from jax.experimental import pallas as pl
from jax.experimental.pallas import tpu as pltpu

F32 = jnp.float32
BF16 = jnp.bfloat16

NORM_EPS = 1e-5
HEAD_DIM = 128
CONV_W = 3
LANES = 128
VMEM_LIMIT = 56 * 1024 * 1024
NEG_BIG = -1e30
FOX_DEAD = 160.0
LOG2E = 1.4426950408889634
QK_SCALE = HEAD_DIM ** -0.5 * LOG2E


def _params(sem, vmem=VMEM_LIMIT):
    return pltpu.CompilerParams(dimension_semantics=sem, vmem_limit_bytes=vmem)


def _dot(a, b):
    return jnp.dot(a, b, preferred_element_type=F32)


def _dot_nt(a, b):
    return lax.dot_general(a, b, (((1,), (1,)), ((), ())), preferred_element_type=F32)


def _softplus2(z2):
    return jnp.maximum(z2, 0.0) + jnp.log(1.0 + jnp.exp2(-jnp.abs(z2))) * LOG2E


def _log_sigmoid(x):
    return jnp.minimum(x, 0.0) - jnp.log(1.0 + jnp.exp(-jnp.abs(x)))


def _split_bf16(x, parts):
    out = []
    rem = x
    for p in range(parts):
        hi = rem.astype(BF16)
        out.append(hi)
        if p + 1 < parts:
            rem = rem - hi.astype(F32)
    return out


def _iota2(n, m):
    return lax.broadcasted_iota(jnp.int32, (n, m), 0), lax.broadcasted_iota(jnp.int32, (n, m), 1)


def _tri_incl(n, seg=None):
    j, s = _iota2(n, n)
    t = jnp.where(j <= s, 1.0, 0.0)
    if seg is not None:
        t = jnp.where(j // seg == s // seg, t, 0.0)
    return t.astype(BF16)


def _suffix_incl(n):
    j, s = _iota2(n, n)
    return jnp.where(j >= s, 1.0, 0.0).astype(BF16)


def _prefix_sum_lanes(x, tri, carry):
    outs = []
    for c in range(x.shape[1] // LANES):
        xc = x[:, c * LANES:(c + 1) * LANES]
        loc = sum(_dot(p, tri) for p in _split_bf16(xc, 3))
        oc = loc if carry is None else loc + carry
        outs.append(oc)
        if carry is not None:
            carry = oc[:, LANES - 1:LANES]
    return jnp.concatenate(outs, axis=1), carry


def _rmsnorm_f32(x, g):
    ms = jnp.mean(x * x, axis=-1, keepdims=True)
    return x * lax.rsqrt(ms + NORM_EPS) * g


def _rmsnorm_kernel(x_ref, g_ref, o_ref):
    o_ref[...] = _rmsnorm_f32(x_ref[...], g_ref[...]).astype(o_ref.dtype)


def _rmsnorm(x, g, tm):
    m, d = x.shape
    return pl.pallas_call(
        _rmsnorm_kernel,
        grid=(m // tm,),
        in_specs=[pl.BlockSpec((tm, d), lambda i: (i, 0)), pl.BlockSpec((1, d), lambda i: (0, 0))],
        out_specs=pl.BlockSpec((tm, d), lambda i: (i, 0)),
        out_shape=jax.ShapeDtypeStruct((m, d), BF16),
        compiler_params=_params(("arbitrary",)),
        name="rmsnorm",
    )(x, g.reshape(1, d))


def _gated_conv(proj, cw, u_m1, u_m2):
    c = proj.shape[1] // 3
    g_b, g_c, hc = proj[:, :c], proj[:, c:2 * c], proj[:, 2 * c:]
    u = g_c * hc
    r2 = pltpu.roll(u, 2, 0)
    u1 = u_m1(pltpu.roll(u, 1, 0))
    u2 = u_m2(r2)
    conv = cw[0:1, :] * u2 + cw[1:2, :] * u1 + cw[2:3, :] * u
    return g_b * conv, u, r2


def _mix_a_prompt_kernel(h_ref, wa_ref, wtail_ref, bf_ref, cw_ref,
                         ya_ref, conv_ref, logf_ref, ck_ref, uprev_ref, carry_ref, wrows_ref, wf_ref, *, layer):
    i = pl.program_id(0)

    @pl.when(i == 0)
    def _():
        uprev_ref[...] = jnp.zeros_like(uprev_ref)
        carry_ref[...] = jnp.zeros_like(carry_ref)
        wrows_ref[...] = jnp.zeros_like(wrows_ref)
        wrows_ref[0:wtail_ref.shape[0], :] = wtail_ref[:, layer, :]
        wf_ref[...] = wrows_ref[...].T.astype(wf_ref.dtype)

    h = h_ref[...]
    proj = _dot_nt(h, wa_ref[...])
    tm = proj.shape[0]
    rows = lax.broadcasted_iota(jnp.int32, (tm, proj.shape[1] // 3), 0)
    prev = uprev_ref[...]
    m1 = lambda r1: jnp.where(rows == 0, prev[1:2, :], r1)
    m2 = lambda r2: jnp.where(rows == 0, prev[0:1, :], jnp.where(rows == 1, prev[1:2, :], r2))
    ya, _, r2 = _gated_conv(proj, cw_ref[...], m1, m2)
    ya_ref[...] = ya.astype(ya_ref.dtype)
    tail = r2[0:2, :]
    uprev_ref[0:2, :] = tail
    conv_ref[...] = tail

    logf = _log_sigmoid(_dot(h, wf_ref[...]) + bf_ref[...])
    logf_ref[...] = logf
    lower = _suffix_incl(tm)
    ck = sum(_dot(lower, p) for p in _split_bf16(logf, 3)) + carry_ref[...]
    ck_ref[...] = ck
    carry_ref[...] = ck[tm - 1:tm, :]


def _tail_spec(w_ndk, first_row):
    n, depth, d = w_ndk.shape
    assert first_row % (n - first_row) == 0
    return pl.BlockSpec((n - first_row, depth, d), lambda i: (first_row // (n - first_row), 0, 0))


def _mix_a_prompt(h, wt, w_ndk, tail_row, layer, bfp, cw, tm):
    t, d = h.shape
    c = cw.shape[1]
    c3 = 3 * c
    return pl.pallas_call(
        functools.partial(_mix_a_prompt_kernel, layer=layer),
        grid=(t // tm,),
        in_specs=[
            pl.BlockSpec((tm, d), lambda i: (i, 0)),
            pl.BlockSpec((None, c3, d), lambda i: (layer, 0, 0)),
            _tail_spec(w_ndk, tail_row),
            pl.BlockSpec((1, LANES), lambda i: (0, 0)),
            pl.BlockSpec((CONV_W, c), lambda i: (0, 0)),
        ],
        out_specs=[
            pl.BlockSpec((tm, c), lambda i: (i, 0)),
            pl.BlockSpec((CONV_W - 1, c), lambda i: (0, 0)),
            pl.BlockSpec((tm, LANES), lambda i: (i, 0)),
            pl.BlockSpec((tm, LANES), lambda i: (i, 0)),
        ],
        out_shape=[
            jax.ShapeDtypeStruct((t, c), BF16),
            jax.ShapeDtypeStruct((CONV_W - 1, c), F32),
            jax.ShapeDtypeStruct((t, LANES), F32),
            jax.ShapeDtypeStruct((t, LANES), F32),
        ],
        scratch_shapes=[pltpu.VMEM((8, c), F32), pltpu.VMEM((1, LANES), F32),
                        pltpu.VMEM((LANES, d), F32), pltpu.VMEM((d, LANES), BF16)],
        compiler_params=_params(("arbitrary",)),
        name="mix_a_prompt",
    )(h, wt, w_ndk, bfp, cw)


def _mix_a_sample_kernel(h_ref, wa_ref, wtail_ref, bf_ref, cw_ref, st_ref,
                         ya_ref, conv_ref, logf_ref, ckl_ref, u_ref, wrows_ref, *, seq, layer):
    h = h_ref[...]
    proj = _dot_nt(h, wa_ref[...])
    n = proj.shape[0]
    nb = n // seq
    rows = lax.broadcasted_iota(jnp.int32, (n, proj.shape[1] // 3), 0)

    def m1(r1):
        for b in range(nb):
            r1 = jnp.where(rows == b * seq, st_ref[2 * b + 1:2 * b + 2, :], r1)
        return r1

    def m2(r2):
        for b in range(nb):
            r2 = jnp.where(rows == b * seq, st_ref[2 * b:2 * b + 1, :], r2)
            r2 = jnp.where(rows == b * seq + 1, st_ref[2 * b + 1:2 * b + 2, :], r2)
        return r2

    ya, u, _ = _gated_conv(proj, cw_ref[...], m1, m2)
    ya_ref[...] = ya.astype(ya_ref.dtype)
    u_ref[...] = u
    for b in range(nb):
        conv_ref[2 * b:2 * b + 2, :] = u_ref[(b + 1) * seq - 2:(b + 1) * seq, :]

    wrows_ref[...] = jnp.zeros_like(wrows_ref)
    wrows_ref[0:wtail_ref.shape[0], :] = wtail_ref[:, layer, :]
    logit = _dot_nt(wrows_ref[...].astype(BF16), h)[0:8, :]
    logf = _log_sigmoid(logit + bf_ref[...])
    logf_ref[...] = logf
    ckl, _ = _prefix_sum_lanes(logf, _tri_incl(LANES, seg=seq), None)
    ckl_ref[...] = ckl


def _mix_a_sample(h, wt, w_ndk, tail_row, layer, bfp, cw, state, seq):
    n, d = h.shape
    c = cw.shape[1]
    c3 = 3 * c
    nb = n // seq
    full = lambda shape: pl.BlockSpec(shape, lambda i: tuple(0 for _ in shape))
    return pl.pallas_call(
        functools.partial(_mix_a_sample_kernel, seq=seq, layer=layer),
        grid=(1,),
        in_specs=[full((n, d)), pl.BlockSpec((None, c3, d), lambda i: (layer, 0, 0)),
                  _tail_spec(w_ndk, tail_row), full((8, 1)), full((CONV_W, c)),
                  full((nb * (CONV_W - 1), c))],
        out_specs=[full((n, c)), full((nb * (CONV_W - 1), c)), full((8, n)), full((8, n))],
        out_shape=[
            jax.ShapeDtypeStruct((n, c), BF16),
            jax.ShapeDtypeStruct((nb * (CONV_W - 1), c), F32),
            jax.ShapeDtypeStruct((8, n), F32),
            jax.ShapeDtypeStruct((8, n), F32),
        ],
        scratch_shapes=[pltpu.VMEM((n, c), F32), pltpu.VMEM((16, d), F32)],
        compiler_params=_params(("arbitrary",)),
        name="mix_a_sample",
    )(h, wt, w_ndk, bfp, cw, state)


def _cumsum_rows_kernel(x_ref, o_ref):
    ck, _ = _prefix_sum_lanes(x_ref[...], _tri_incl(LANES), jnp.zeros((x_ref.shape[0], 1), F32))
    o_ref[...] = ck


def _cumsum_rows(x):
    return pl.pallas_call(
        _cumsum_rows_kernel,
        grid=(1,),
        in_specs=[pl.BlockSpec(x.shape, lambda i: (0, 0))],
        out_specs=pl.BlockSpec(x.shape, lambda i: (0, 0)),
        out_shape=jax.ShapeDtypeStruct(x.shape, F32),
        compiler_params=_params(("arbitrary",)),
        name="cumsum_rows",
    )(x)


def _wprep_kernel(w_ref, o_ref):
    for layer in range(o_ref.shape[0]):
        o_ref[layer] = w_ref[:, layer, :].astype(o_ref.dtype)


def _wprep(w_ndk, n_rows, tn):
    _, depth, d = w_ndk.shape
    return pl.pallas_call(
        _wprep_kernel,
        grid=(n_rows // tn,),
        in_specs=[pl.BlockSpec((tn, depth, d), lambda i: (i, 0, 0))],
        out_specs=pl.BlockSpec((depth, tn, d), lambda i: (0, i, 0)),
        out_shape=jax.ShapeDtypeStruct((depth, n_rows, d), BF16),
        compiler_params=_params(("arbitrary",)),
        name="wprep",
    )(w_ndk)


def _store_heads(ref, acc):
    n_heads = ref.shape[-3]
    for hd in range(n_heads):
        part = acc[:, hd * HEAD_DIM:(hd + 1) * HEAD_DIM]
        if len(ref.shape) == 3:
            ref[hd] = part
        else:
            seq = ref.shape[2]
            for b in range(ref.shape[0]):
                ref[b, hd] = part[b * seq:(b + 1) * seq, :]


def _qkv_kernel(h_ref, w_ref, qkv_ref, sbk_ref, sbv_ref, fk_ref, fv_ref):
    j = pl.program_id(1)
    acc = _dot_nt(h_ref[...], w_ref[...])
    is_q = jnp.logical_or(j == 0, j == 3)
    qkv_ref[...] = (acc * jnp.where(is_q, QK_SCALE, 1.0)).astype(qkv_ref.dtype)
    for idx, ref in ((1, sbk_ref), (2, sbv_ref), (4, fk_ref), (5, fv_ref)):
        @pl.when(j == idx)
        def _(ref=ref):
            _store_heads(ref, acc)


def _qkv_proj(h, wt, first_row, tm, layer, batch, n_heads):
    m, d = h.shape
    seq = m // batch
    tn = n_heads * HEAD_DIM
    assert first_row % tn == 0
    blk0 = first_row // tn
    if m // tm == 1:
        cache_spec = pl.BlockSpec((batch, n_heads, seq, HEAD_DIM), lambda i, j: (0, 0, 0, 0))
    else:
        assert batch == 1
        cache_spec = pl.BlockSpec((None, n_heads, tm, HEAD_DIM), lambda i, j: (0, 0, i, 0))
    cache = jax.ShapeDtypeStruct((batch, n_heads, seq, HEAD_DIM), F32)
    return pl.pallas_call(
        _qkv_kernel,
        grid=(m // tm, 6),
        in_specs=[pl.BlockSpec((tm, d), lambda i, j: (i, 0)),
                  pl.BlockSpec((None, tn, d), lambda i, j: (layer, blk0 + j, 0))],
        out_specs=[pl.BlockSpec((tm, tn), lambda i, j: (i, j))] + [cache_spec] * 4,
        out_shape=[jax.ShapeDtypeStruct((m, 6 * tn), BF16)] + [cache] * 4,
        compiler_params=_params(("arbitrary", "arbitrary")),
        name="qkv_proj",
    )(h, wt)


def _sb_tile(q, kb, vb, suffix, mask):
    z = _dot_nt(q, kb)
    sp = _softplus2(z)
    if mask is not None:
        sp = jnp.where(mask, sp, 0.0)
    s_incl = _dot(sp.astype(BF16), suffix)
    w = jnp.exp2(z - s_incl)
    if mask is not None:
        w = jnp.where(mask, w, 0.0)
    return _dot(w.astype(BF16), vb), s_incl[:, 0:1]


def _sb_accumulate(acc_ref, r_ref, pv, tot):
    r = r_ref[...]
    acc_ref[...] += jnp.exp2(-r) * pv
    r_ref[...] = r + tot


def _for_past_tiles(n_tiles, pair_fn, single_fn, live_fn=None):
    if live_fn is None:
        def body(jj, c):
            pair_fn(n_tiles - 1 - 2 * jj)
            return c

        lax.fori_loop(0, n_tiles // 2, body, 0)
        live = True
    else:
        def step(c):
            pair_fn(n_tiles - 1 - 2 * c[0])
            return c[0] + 1, live_fn()

        _, live = lax.while_loop(lambda c: jnp.logical_and(c[0] < n_tiles // 2, c[1] > 0), step,
                                 (jnp.int32(0), live_fn()))
        live = live > 0
    odd = n_tiles % 2 == 1
    if isinstance(odd, bool) and live is True:
        if odd:
            single_fn(0)
    else:
        pl.when(jnp.logical_and(odd, live))(lambda: single_fn(0))


def _sb_prompt_kernel(q_ref, k_ref, v_ref, o_ref, acc_ref, r_ref, *, tq):
    qi = pl.program_id(1)
    q = q_ref[...]
    acc_ref[...] = jnp.zeros_like(acc_ref)
    r_ref[...] = jnp.zeros_like(r_ref)
    suffix = _suffix_incl(tq)
    row, col = _iota2(tq, tq)

    def kv(j):
        st = pl.multiple_of(j * tq, tq)
        return k_ref[pl.ds(st, tq), :], v_ref[pl.ds(st, tq), :]

    _sb_accumulate(acc_ref, r_ref, *_sb_tile(q, *kv(qi), suffix, col < row))

    def single(j):
        _sb_accumulate(acc_ref, r_ref, *_sb_tile(q, *kv(j), suffix, None))

    def pair(j):
        pv_a, tot_a = _sb_tile(q, *kv(j), suffix, None)
        pv_b, tot_b = _sb_tile(q, *kv(j - 1), suffix, None)
        _sb_accumulate(acc_ref, r_ref, pv_a + jnp.exp2(-tot_a) * pv_b, tot_a + tot_b)

    def live():
        r_min = jnp.min(r_ref[...], axis=0, keepdims=True)
        return (jnp.max(jnp.exp2(-r_min)) > 0.0).astype(jnp.int32)

    _for_past_tiles(qi, pair, single, live)
    o_ref[...] = acc_ref[...].astype(o_ref.dtype)


def _sb_prompt(qkv, n_heads, tq):
    t = qkv.shape[0]
    return pl.pallas_call(
        functools.partial(_sb_prompt_kernel, tq=tq),
        grid=(n_heads, t // tq),
        in_specs=[
            pl.BlockSpec((tq, HEAD_DIM), lambda h, i: (i, h)),
            pl.BlockSpec((t, HEAD_DIM), lambda h, i: (0, n_heads + h)),
            pl.BlockSpec((t, HEAD_DIM), lambda h, i: (0, 2 * n_heads + h)),
        ],
        out_specs=pl.BlockSpec((tq, HEAD_DIM), lambda h, i: (i, h)),
        out_shape=jax.ShapeDtypeStruct((t, n_heads * HEAD_DIM), BF16),
        scratch_shapes=[pltpu.VMEM((tq, HEAD_DIM), F32), pltpu.VMEM((tq, HEAD_DIM), F32)],
        compiler_params=_params(("arbitrary", "arbitrary")),
        name="sb_prompt",
    )(qkv, qkv, qkv)


def _sb_sample_kernel(q_ref, kn_ref, vn_ref, kc_ref, vc_ref, o_ref, acc_ref, r_ref, *, tk):
    q = q_ref[...]
    seq = q.shape[0]
    past = kc_ref.shape[0]
    acc_ref[...] = jnp.zeros_like(acc_ref)
    r_ref[...] = jnp.zeros_like(r_ref)
    row, col = _iota2(seq, seq)
    _sb_accumulate(acc_ref, r_ref, *_sb_tile(q, kn_ref[...], vn_ref[...], _suffix_incl(seq), col < row))
    suffix = _suffix_incl(tk)

    def kv(j):
        st = pl.multiple_of(j * tk, tk)
        return kc_ref[pl.ds(st, tk), :].astype(BF16), vc_ref[pl.ds(st, tk), :].astype(BF16)

    def single(j):
        _sb_accumulate(acc_ref, r_ref, *_sb_tile(q, *kv(j), suffix, None))

    def pair(j):
        pv_a, tot_a = _sb_tile(q, *kv(j), suffix, None)
        pv_b, tot_b = _sb_tile(q, *kv(j - 1), suffix, None)
        _sb_accumulate(acc_ref, r_ref, pv_a + jnp.exp2(-tot_a) * pv_b, tot_a + tot_b)

    _for_past_tiles(past // tk, pair, single)
    o_ref[...] = acc_ref[...].astype(o_ref.dtype)


def _sample_specs(layer, n_heads, seq, past, first_col):
    qkv = [pl.BlockSpec((seq, HEAD_DIM), lambda b, h, k=k: (b, (first_col + k) * n_heads + h)) for k in range(3)]
    cache = pl.BlockSpec((None, None, None, past, HEAD_DIM), lambda b, h: (layer, b, h, 0, 0))
    return qkv + [cache, cache]


def _sb_sample(qkv, cache_k, cache_v, layer, seq, tk):
    n = qkv.shape[0]
    nb = n // seq
    _, _, n_heads, past, _ = cache_k.shape
    return pl.pallas_call(
        functools.partial(_sb_sample_kernel, tk=tk),
        grid=(nb, n_heads),
        in_specs=_sample_specs(layer, n_heads, seq, past, 0),
        out_specs=pl.BlockSpec((seq, HEAD_DIM), lambda b, h: (b, h)),
        out_shape=jax.ShapeDtypeStruct((n, n_heads * HEAD_DIM), BF16),
        scratch_shapes=[pltpu.VMEM((seq, HEAD_DIM), F32), pltpu.VMEM((seq, HEAD_DIM), F32)],
        compiler_params=_params(("arbitrary", "arbitrary")),
        name="sb_sample",
    )(qkv, qkv, qkv, cache_k, cache_v)


def _fox_init(m_ref, l_ref, acc_ref):
    m_ref[...] = jnp.full(m_ref.shape, NEG_BIG, F32)
    l_ref[...] = jnp.zeros_like(l_ref)
    acc_ref[...] = jnp.zeros_like(acc_ref)


def _fox_tile_t(qt, kb, vt, bias, mask, m_ref, l_ref, acc_ref):
    s = _dot(kb, qt) + jnp.concatenate([bias] * (qt.shape[1] // LANES), axis=1)
    if mask is not None:
        s = jnp.where(mask, s, NEG_BIG)
    m_prev = m_ref[...]
    m_new = jnp.maximum(m_prev, jnp.max(s, axis=0, keepdims=True))
    alpha = jnp.exp2(m_prev - m_new)
    p = jnp.exp2(s - m_new)
    l_ref[...] = alpha * l_ref[...] + jnp.sum(p, axis=0, keepdims=True)
    acc_ref[...] = alpha * acc_ref[...] + _dot(vt, p.astype(BF16))
    m_ref[...] = m_new


def _fox_prompt_kernel(q_ref, k_ref, v_ref, ck_ref, o_ref,
                       vt_ref, nb_ref, kmax_ref, edge_ref, m_ref, l_ref, acc_ref, *, tq, chunk):
    h = pl.program_id(0)
    qi = pl.program_id(1)
    t = k_ref.shape[0]
    reps = tq // LANES

    @pl.when(qi == 0)
    def _():
        sel_r, _ = _iota2(LANES, LANES)
        sel = jnp.where(sel_r == h, 1.0, 0.0).astype(BF16)
        kmax_ref[...] = jnp.zeros_like(kmax_ref)

        def body(c, carry):
            st = pl.multiple_of(c * chunk, chunk)
            vt_ref[:, pl.ds(st, chunk)] = v_ref[pl.ds(st, chunk), :].astype(F32).T.astype(BF16)
            ck_h = sum(_dot(p, sel) for p in _split_bf16(ck_ref[pl.ds(st, chunk), :], 3))
            nb_ref[pl.ds(st, chunk), :] = ck_h * (-LOG2E)
            kf = k_ref[pl.ds(st, chunk), :].astype(F32)
            kmax_ref[...] = jnp.maximum(kmax_ref[...], jnp.max(jnp.sum(kf * kf, axis=1, keepdims=True)))
            return carry

        lax.fori_loop(0, t // chunk, body, 0)

    qtf = q_ref[...].astype(F32).T
    qt = qtf.astype(BF16)
    _fox_init(m_ref, l_ref, acc_ref)
    qs = pl.multiple_of(qi * tq, tq)
    c0 = nb_ref[pl.ds(qs, 1), :]
    edge_ref[...] = c0

    def tile(st, tk, mask):
        bias = nb_ref[pl.ds(st, tk), :] - c0
        _fox_tile_t(qt, k_ref[pl.ds(st, tk), :], vt_ref[:, pl.ds(st, tk)], bias, mask, m_ref, l_ref, acc_ref)
        edge_ref[...] = nb_ref[pl.ds(st, 1), :]

    q_norm2 = jnp.sum(qtf * qtf, axis=0, keepdims=True)
    reach = jnp.sqrt(q_norm2 * jnp.concatenate([kmax_ref[...]] * reps, axis=1)) * 1.001

    def live():
        edge = jnp.concatenate([edge_ref[...] - c0] * reps, axis=1)
        return (jnp.max(reach + edge - m_ref[...]) > -FOX_DEAD).astype(jnp.int32)

    krow, qcol = _iota2(tq, tq)
    tile(qs, tq, krow <= qcol)
    _for_past_tiles(qi,
                    lambda j: tile(pl.multiple_of((j - 1) * tq, tq), 2 * tq, None),
                    lambda j: tile(pl.multiple_of(j * tq, tq), tq, None),
                    live)
    o_ref[...] = (acc_ref[...] / l_ref[...]).T.astype(o_ref.dtype)


def _fox_prompt(qkv, ck, n_heads, tq):
    t = qkv.shape[0]
    return pl.pallas_call(
        functools.partial(_fox_prompt_kernel, tq=tq, chunk=512),
        grid=(n_heads, t // tq),
        in_specs=[
            pl.BlockSpec((tq, HEAD_DIM), lambda h, i: (i, 3 * n_heads + h)),
            pl.BlockSpec((t, HEAD_DIM), lambda h, i: (0, 4 * n_heads + h)),
            pl.BlockSpec((t, HEAD_DIM), lambda h, i: (0, 5 * n_heads + h)),
            pl.BlockSpec((t, LANES), lambda h, i: (0, 0)),
        ],
        out_specs=pl.BlockSpec((tq, HEAD_DIM), lambda h, i: (i, h)),
        out_shape=jax.ShapeDtypeStruct((t, n_heads * HEAD_DIM), BF16),
        scratch_shapes=[pltpu.VMEM((HEAD_DIM, t), BF16), pltpu.VMEM((t, LANES), F32),
                        pltpu.VMEM((1, LANES), F32), pltpu.VMEM((1, LANES), F32),
                        pltpu.VMEM((1, tq), F32), pltpu.VMEM((1, tq), F32), pltpu.VMEM((HEAD_DIM, tq), F32)],
        compiler_params=_params(("arbitrary", "arbitrary")),
        name="fox_prompt",
    )(qkv, qkv, qkv, ck)


def _fox_tile(q, kb, vb, bias, mask, m_ref, l_ref, acc_ref):
    s = _dot_nt(q, kb) + bias
    if mask is not None:
        s = jnp.where(mask, s, NEG_BIG)
    m_prev = m_ref[...]
    m_new = jnp.maximum(m_prev, jnp.max(s, axis=1, keepdims=True))
    alpha = jnp.exp2(m_prev - m_new)
    p = jnp.exp2(s - m_new)
    l_ref[...] = alpha * l_ref[...] + jnp.sum(p, axis=1, keepdims=True)
    acc_ref[...] = alpha * acc_ref[...] + _dot(p.astype(BF16), vb)
    m_ref[...] = m_new


def _fox_sample_kernel(q_ref, kn_ref, vn_ref, kc_ref, vc_ref, ckp_ref, ckl_ref, o_ref,
                       m_ref, l_ref, acc_ref, *, tk):
    q = q_ref[...]
    seq = q.shape[0]
    past = kc_ref.shape[0]
    _fox_init(m_ref, l_ref, acc_ref)
    row, col = _iota2(seq, seq)
    _fox_tile(q, kn_ref[...], vn_ref[...], ckl_ref[...] * (-LOG2E), col <= row, m_ref, l_ref, acc_ref)
    c0 = ckp_ref[:, past - 1:past]

    def body(jj, c):
        st = pl.multiple_of((past // tk - 1 - jj) * tk, tk)
        kb = kc_ref[pl.ds(st, tk), :].astype(BF16)
        vb = vc_ref[pl.ds(st, tk), :].astype(BF16)
        bias = (c0 - ckp_ref[:, pl.ds(st, tk)]) * LOG2E
        _fox_tile(q, kb, vb, bias, None, m_ref, l_ref, acc_ref)
        return c

    lax.fori_loop(0, past // tk, body, 0)
    o_ref[...] = (acc_ref[...] / l_ref[...]).astype(o_ref.dtype)


def _fox_sample(qkv, cache_k, cache_v, ck_past, ck_local, layer, seq, tk):
    n = qkv.shape[0]
    nb = n // seq
    _, _, n_heads, past, _ = cache_k.shape
    return pl.pallas_call(
        functools.partial(_fox_sample_kernel, tk=tk),
        grid=(nb, n_heads),
        in_specs=_sample_specs(layer, n_heads, seq, past, 3) + [
            pl.BlockSpec((None, None, 1, past), lambda b, h: (h, b, 0, 0)),
            pl.BlockSpec((None, None, 1, seq), lambda b, h: (h, b, 0, 0)),
        ],
        out_specs=pl.BlockSpec((seq, HEAD_DIM), lambda b, h: (b, h)),
        out_shape=jax.ShapeDtypeStruct((n, n_heads * HEAD_DIM), BF16),
        scratch_shapes=[pltpu.VMEM((seq, 1), F32), pltpu.VMEM((seq, 1), F32), pltpu.VMEM((seq, HEAD_DIM), F32)],
        compiler_params=_params(("arbitrary", "arbitrary")),
        name="fox_sample",
    )(qkv, qkv, qkv, cache_k, cache_v, ck_past, ck_local)


def _out_proj_kernel(ya_ref, ysb_ref, yfox_ref, w_ref, x_ref, g_ref, xo_ref, h_ref):
    c = ya_ref.shape[1]
    s = ysb_ref.shape[1]
    acc = _dot(ya_ref[...], w_ref[0:c, :])
    acc += _dot(ysb_ref[...], w_ref[c:c + s, :])
    acc += _dot(yfox_ref[...], w_ref[c + s:, :])
    x = x_ref[...] + acc
    xo_ref[...] = x
    h_ref[...] = _rmsnorm_f32(x, g_ref[...]).astype(h_ref.dtype)


def _out_proj(ya, ysb, yfox, w, x, g, tm):
    m, d = x.shape
    row = lambda width: pl.BlockSpec((tm, width), lambda i: (i, 0))
    return pl.pallas_call(
        _out_proj_kernel,
        grid=(m // tm,),
        in_specs=[row(ya.shape[1]), row(ysb.shape[1]), row(yfox.shape[1]),
                  pl.BlockSpec(w.shape, lambda i: (0, 0)), row(d), pl.BlockSpec((1, d), lambda i: (0, 0))],
        out_specs=[row(d), row(d)],
        out_shape=[jax.ShapeDtypeStruct((m, d), F32), jax.ShapeDtypeStruct((m, d), BF16)],
        compiler_params=_params(("arbitrary",)),
        name="out_proj",
    )(ya, ysb, yfox, w, x, g.reshape(1, d))


def _ffn_kernel(h_ref, wg_ref, wu_ref, wd_ref, x_ref, g_ref, *rest, emit_x):
    out_refs, acc_ref = rest[:-1], rest[-1]
    f = pl.program_id(1)

    @pl.when(f == 0)
    def _():
        acc_ref[...] = jnp.zeros_like(acc_ref)

    h = h_ref[...]
    gate = _dot(h, wg_ref[...])
    up = _dot(h, wu_ref[...])
    act = gate * jax.nn.sigmoid(gate) * up
    acc_ref[...] += _dot(act.astype(BF16), wd_ref[...])

    @pl.when(f == pl.num_programs(1) - 1)
    def _():
        x = x_ref[...] + acc_ref[...]
        normed = _rmsnorm_f32(x, g_ref[...])
        if emit_x:
            out_refs[0][...] = x
            out_refs[1][...] = normed.astype(out_refs[1].dtype)
        else:
            out_refs[0][...] = normed


def _ffn(h, wg, wu, wd, x, g, tm, tf, emit_x):
    m, d = x.shape
    ff = wg.shape[1]
    row = pl.BlockSpec((tm, d), lambda i, f: (i, 0))
    if emit_x:
        out_specs = [row, row]
        out_shape = [jax.ShapeDtypeStruct((m, d), F32), jax.ShapeDtypeStruct((m, d), BF16)]
    else:
        out_specs = [row]
        out_shape = [jax.ShapeDtypeStruct((m, d), F32)]
    return pl.pallas_call(
        functools.partial(_ffn_kernel, emit_x=emit_x),
        grid=(m // tm, ff // tf),
        in_specs=[row,
                  pl.BlockSpec((d, tf), lambda i, f: (0, f)),
                  pl.BlockSpec((d, tf), lambda i, f: (0, f)),
                  pl.BlockSpec((tf, d), lambda i, f: (f, 0)),
                  row,
                  pl.BlockSpec((1, d), lambda i, f: (0, 0))],
        out_specs=out_specs,
        out_shape=out_shape,
        scratch_shapes=[pltpu.VMEM((tm, d), F32)],
        compiler_params=_params(("arbitrary", "arbitrary")),
        name="ffn",
    )(h, wg, wu, wd, x, g.reshape(1, d))


def kernel(x_prompt, x_sample, state_conv, cache_sb_k, cache_sb_v, cache_fox_k, cache_fox_v, cache_fox_logf,
           norm1_g, w_in, b_f, conv_w, w_out, norm2_g, w_gate, w_up, w_down, final_g):
    depth = w_in.shape[0]
    bp, t, d = x_prompt.shape
    nb, seq, _ = x_sample.shape
    assert bp == 1, "prompt kernels carry conv rows and forget sums across row tiles of one stream"
    c = conv_w.shape[2]
    n_sb = cache_sb_k.shape[3]
    n_fox = cache_fox_k.shape[3]
    assert n_sb == n_fox and n_fox <= 8
    past = cache_sb_k.shape[2]
    sb_dim = n_sb * HEAD_DIM
    qkv0 = 3 * c
    qkv1 = qkv0 + 3 * sb_dim + 3 * n_fox * HEAD_DIM
    n_s = nb * seq
    per_head = lambda a: jnp.transpose(a, (0, 1, 3, 2, 4))
    csk, csv, cfk, cfv = (per_head(a) for a in (cache_sb_k, cache_sb_v, cache_fox_k, cache_fox_v))
    w_ndk = jnp.transpose(w_in, (2, 0, 1))
    wt = _wprep(w_ndk, qkv1, 512)

    xp = x_prompt.reshape(t, d)
    xs = x_sample.reshape(n_s, d)
    hp = _rmsnorm(xp, norm1_g[0], 512)
    hs = _rmsnorm(xs, norm1_g[0], n_s)
    p_small, s_small = [], []
    p_kv, s_kv = [], []
    for l in range(depth):
        bf_row = jnp.pad(b_f[l], (0, LANES - n_fox)).reshape(1, LANES)
        bf_col = jnp.pad(b_f[l], (0, 8 - n_fox)).reshape(8, 1)
        wo = w_out[l].astype(BF16)
        wg = w_gate[l].astype(BF16)
        wu = w_up[l].astype(BF16)
        wd = w_down[l].astype(BF16)
        last = l == depth - 1
        g_next = final_g if last else norm1_g[l + 1]

        ya, conv_p, logf_p, ck_p = _mix_a_prompt(hp, wt, w_ndk, qkv1, l, bf_row, conv_w[l], 512)
        qkv, *kv = _qkv_proj(hp, wt, qkv0, 1024, l, bp, n_sb)
        p_kv.append(kv)
        ysb = _sb_prompt(qkv, n_sb, 256)
        yfox = _fox_prompt(qkv, ck_p, n_fox, 256)
        xp, h2 = _out_proj(ya, ysb, yfox, wo, xp, norm2_g[l], 512)
        outs = _ffn(h2, wg, wu, wd, xp, g_next, 512, 512, not last)
        if last:
            y_prompt = outs[0]
        else:
            xp, hp = outs
        p_small.append((conv_p, logf_p[:, :n_fox]))

        logf_past = jnp.transpose(cache_fox_logf[l], (2, 0, 1)).reshape(n_fox * nb, past)
        ck_past = _cumsum_rows(logf_past).reshape(n_fox, nb, 1, past)
        ya, conv_s, logf_s, ckl = _mix_a_sample(hs, wt, w_ndk, qkv1, l, bf_col, conv_w[l],
                                                state_conv[l].reshape(nb * (CONV_W - 1), c), seq)
        ckl = ckl[:n_fox].reshape(n_fox, nb, 1, seq)
        qkv, *kv = _qkv_proj(hs, wt, qkv0, n_s, l, nb, n_sb)
        s_kv.append(kv)
        ysb = _sb_sample(qkv, csk, csv, l, seq, 256)
        yfox = _fox_sample(qkv, cfk, cfv, ck_past, ckl, l, seq, 512)
        xs, h2 = _out_proj(ya, ysb, yfox, wo, xs, norm2_g[l], n_s)
        outs = _ffn(h2, wg, wu, wd, xs, g_next, n_s, 512, not last)
        if last:
            y_sample = outs[0]
        else:
            xs, hs = outs
        logf_s = jnp.transpose(logf_s[:n_fox].reshape(n_fox, nb, seq), (1, 2, 0))
        s_small.append((conv_s, logf_s))

    stack = lambda news, i, shape: jnp.stack([n[i] for n in news]).reshape((depth,) + shape)
    stack_kv = lambda news: [per_head(jnp.stack([n[i] for n in news])) for i in range(4)]
    return (
        y_prompt.reshape(bp, t, d),
        y_sample.reshape(nb, seq, d),
        stack(p_small, 0, (bp, CONV_W - 1, c)),
        *stack_kv(p_kv),
        stack(p_small, 1, (bp, t, n_fox)),
        stack(s_small, 0, (nb, CONV_W - 1, c)),
        *stack_kv(s_kv),
        stack(s_small, 1, (nb, seq, n_fox)),
    )
```

```python
import functools

import jax
import jax.numpy as jnp
from jax import lax
from jax.experimental import pallas as pl
from jax.experimental.pallas import tpu as pltpu

F32 = jnp.float32
BF16 = jnp.bfloat16

NORM_EPS = 1e-5
HEAD_DIM = 128
CONV_W = 3
LANES = 128
VMEM_LIMIT = 56 * 1024 * 1024
NEG_BIG = -1e30
FOX_DEAD = 160.0
LOG2E = 1.4426950408889634
QK_SCALE = HEAD_DIM ** -0.5 * LOG2E


def _params(sem, vmem=VMEM_LIMIT):
    return pltpu.CompilerParams(dimension_semantics=sem, vmem_limit_bytes=vmem)


def _dot(a, b):
    return jnp.dot(a, b, preferred_element_type=F32)


def _dot_nt(a, b):
    return lax.dot_general(a, b, (((1,), (1,)), ((), ())), preferred_element_type=F32)


def _softplus2(z2):
    return jnp.maximum(z2, 0.0) + jnp.log(1.0 + jnp.exp2(-jnp.abs(z2))) * LOG2E


def _log_sigmoid(x):
    return jnp.minimum(x, 0.0) - jnp.log(1.0 + jnp.exp(-jnp.abs(x)))


def _split_bf16(x, parts):
    out = []
    rem = x
    for p in range(parts):
        hi = rem.astype(BF16)
        out.append(hi)
        if p + 1 < parts:
            rem = rem - hi.astype(F32)
    return out


def _iota2(n, m):
    return lax.broadcasted_iota(jnp.int32, (n, m), 0), lax.broadcasted_iota(jnp.int32, (n, m), 1)


def _tri_incl(n, seg=None):
    j, s = _iota2(n, n)
    t = jnp.where(j <= s, 1.0, 0.0)
    if seg is not None:
        t = jnp.where(j // seg == s // seg, t, 0.0)
    return t.astype(BF16)


def _suffix_incl(n):
    j, s = _iota2(n, n)
    return jnp.where(j >= s, 1.0, 0.0).astype(BF16)


def _prefix_sum_lanes(x, tri, carry):
    outs = []
    for c in range(x.shape[1] // LANES):
        xc = x[:, c * LANES:(c + 1) * LANES]
        loc = sum(_dot(p, tri) for p in _split_bf16(xc, 3))
        oc = loc if carry is None else loc + carry
        outs.append(oc)
        if carry is not None:
            carry = oc[:, LANES - 1:LANES]
    return jnp.concatenate(outs, axis=1), carry


def _rmsnorm_f32(x, g):
    ms = jnp.mean(x * x, axis=-1, keepdims=True)
    return x * lax.rsqrt(ms + NORM_EPS) * g


def _rmsnorm_kernel(x_ref, g_ref, o_ref):
    o_ref[...] = _rmsnorm_f32(x_ref[...], g_ref[...]).astype(o_ref.dtype)


def _rmsnorm(x, g, tm):
    m, d = x.shape
    return pl.pallas_call(
        _rmsnorm_kernel,
        grid=(m // tm,),
        in_specs=[pl.BlockSpec((tm, d), lambda i: (i, 0)), pl.BlockSpec((1, d), lambda i: (0, 0))],
        out_specs=pl.BlockSpec((tm, d), lambda i: (i, 0)),
        out_shape=jax.ShapeDtypeStruct((m, d), BF16),
        compiler_params=_params(("arbitrary",)),
        name="rmsnorm",
    )(x, g.reshape(1, d))


def _gated_conv(proj, cw, u_m1, u_m2):
    c = proj.shape[1] // 3
    g_b, g_c, hc = proj[:, :c], proj[:, c:2 * c], proj[:, 2 * c:]
    u = g_c * hc
    r2 = pltpu.roll(u, 2, 0)
    u1 = u_m1(pltpu.roll(u, 1, 0))
    u2 = u_m2(r2)
    conv = cw[0:1, :] * u2 + cw[1:2, :] * u1 + cw[2:3, :] * u
    return g_b * conv, u, r2


def _mix_a_prompt_kernel(h_ref, wa_ref, wtail_ref, bf_ref, cw_ref,
                         ya_ref, conv_ref, logf_ref, ck_ref, uprev_ref, carry_ref, wrows_ref, wf_ref, *, layer):
    i = pl.program_id(0)

    @pl.when(i == 0)
    def _():
        uprev_ref[...] = jnp.zeros_like(uprev_ref)
        carry_ref[...] = jnp.zeros_like(carry_ref)
        wrows_ref[...] = jnp.zeros_like(wrows_ref)
        wrows_ref[0:wtail_ref.shape[0], :] = wtail_ref[:, layer, :]
        wf_ref[...] = wrows_ref[...].T.astype(wf_ref.dtype)

    h = h_ref[...]
    proj = _dot_nt(h, wa_ref[...])
    tm = proj.shape[0]
    rows = lax.broadcasted_iota(jnp.int32, (tm, proj.shape[1] // 3), 0)
    prev = uprev_ref[...]
    m1 = lambda r1: jnp.where(rows == 0, prev[1:2, :], r1)
    m2 = lambda r2: jnp.where(rows == 0, prev[0:1, :], jnp.where(rows == 1, prev[1:2, :], r2))
    ya, _, r2 = _gated_conv(proj, cw_ref[...], m1, m2)
    ya_ref[...] = ya.astype(ya_ref.dtype)
    tail = r2[0:2, :]
    uprev_ref[0:2, :] = tail
    conv_ref[...] = tail

    logf = _log_sigmoid(_dot(h, wf_ref[...]) + bf_ref[...])
    logf_ref[...] = logf
    lower = _suffix_incl(tm)
    ck = sum(_dot(lower, p) for p in _split_bf16(logf, 3)) + carry_ref[...]
    ck_ref[...] = ck
    carry_ref[...] = ck[tm - 1:tm, :]


def _tail_spec(w_ndk, first_row):
    n, depth, d = w_ndk.shape
    assert first_row % (n - first_row) == 0
    return pl.BlockSpec((n - first_row, depth, d), lambda i: (first_row // (n - first_row), 0, 0))


def _mix_a_prompt(h, wt, w_ndk, tail_row, layer, bfp, cw, tm):
    t, d = h.shape
    c = cw.shape[1]
    c3 = 3 * c
    return pl.pallas_call(
        functools.partial(_mix_a_prompt_kernel, layer=layer),
        grid=(t // tm,),
        in_specs=[
            pl.BlockSpec((tm, d), lambda i: (i, 0)),
            pl.BlockSpec((None, c3, d), lambda i: (layer, 0, 0)),
            _tail_spec(w_ndk, tail_row),
            pl.BlockSpec((1, LANES), lambda i: (0, 0)),
            pl.BlockSpec((CONV_W, c), lambda i: (0, 0)),
        ],
        out_specs=[
            pl.BlockSpec((tm, c), lambda i: (i, 0)),
            pl.BlockSpec((CONV_W - 1, c), lambda i: (0, 0)),
            pl.BlockSpec((tm, LANES), lambda i: (i, 0)),
            pl.BlockSpec((tm, LANES), lambda i: (i, 0)),
        ],
        out_shape=[
            jax.ShapeDtypeStruct((t, c), BF16),
            jax.ShapeDtypeStruct((CONV_W - 1, c), F32),
            jax.ShapeDtypeStruct((t, LANES), F32),
            jax.ShapeDtypeStruct((t, LANES), F32),
        ],
        scratch_shapes=[pltpu.VMEM((8, c), F32), pltpu.VMEM((1, LANES), F32),
                        pltpu.VMEM((LANES, d), F32), pltpu.VMEM((d, LANES), BF16)],
        compiler_params=_params(("arbitrary",)),
        name="mix_a_prompt",
    )(h, wt, w_ndk, bfp, cw)


def _mix_a_sample_kernel(h_ref, wa_ref, wtail_ref, bf_ref, cw_ref, st_ref,
                         ya_ref, conv_ref, logf_ref, ckl_ref, u_ref, wrows_ref, *, seq, layer):
    h = h_ref[...]
    proj = _dot_nt(h, wa_ref[...])
    n = proj.shape[0]
    nb = n // seq
    rows = lax.broadcasted_iota(jnp.int32, (n, proj.shape[1] // 3), 0)

    def m1(r1):
        for b in range(nb):
            r1 = jnp.where(rows == b * seq, st_ref[2 * b + 1:2 * b + 2, :], r1)
        return r1

    def m2(r2):
        for b in range(nb):
            r2 = jnp.where(rows == b * seq, st_ref[2 * b:2 * b + 1, :], r2)
            r2 = jnp.where(rows == b * seq + 1, st_ref[2 * b + 1:2 * b + 2, :], r2)
        return r2

    ya, u, _ = _gated_conv(proj, cw_ref[...], m1, m2)
    ya_ref[...] = ya.astype(ya_ref.dtype)
    u_ref[...] = u
    for b in range(nb):
        conv_ref[2 * b:2 * b + 2, :] = u_ref[(b + 1) * seq - 2:(b + 1) * seq, :]

    wrows_ref[...] = jnp.zeros_like(wrows_ref)
    wrows_ref[0:wtail_ref.shape[0], :] = wtail_ref[:, layer, :]
    logit = _dot_nt(wrows_ref[...].astype(BF16), h)[0:8, :]
    logf = _log_sigmoid(logit + bf_ref[...])
    logf_ref[...] = logf
    ckl, _ = _prefix_sum_lanes(logf, _tri_incl(LANES, seg=seq), None)
    ckl_ref[...] = ckl


def _mix_a_sample(h, wt, w_ndk, tail_row, layer, bfp, cw, state, seq):
    n, d = h.shape
    c = cw.shape[1]
    c3 = 3 * c
    nb = n // seq
    full = lambda shape: pl.BlockSpec(shape, lambda i: tuple(0 for _ in shape))
    return pl.pallas_call(
        functools.partial(_mix_a_sample_kernel, seq=seq, layer=layer),
        grid=(1,),
        in_specs=[full((n, d)), pl.BlockSpec((None, c3, d), lambda i: (layer, 0, 0)),
                  _tail_spec(w_ndk, tail_row), full((8, 1)), full((CONV_W, c)),
                  full((nb * (CONV_W - 1), c))],
        out_specs=[full((n, c)), full((nb * (CONV_W - 1), c)), full((8, n)), full((8, n))],
        out_shape=[
            jax.ShapeDtypeStruct((n, c), BF16),
            jax.ShapeDtypeStruct((nb * (CONV_W - 1), c), F32),
            jax.ShapeDtypeStruct((8, n), F32),
            jax.ShapeDtypeStruct((8, n), F32),
        ],
        scratch_shapes=[pltpu.VMEM((n, c), F32), pltpu.VMEM((16, d), F32)],
        compiler_params=_params(("arbitrary",)),
        name="mix_a_sample",
    )(h, wt, w_ndk, bfp, cw, state)


def _prefix_max_lanes(x):
    lane = lax.broadcasted_iota(jnp.int32, (x.shape[0], LANES), 1)
    outs, carry = [], None
    for c in range(x.shape[1] // LANES):
        xc = x[:, c * LANES:(c + 1) * LANES]
        shift = 1
        while shift < LANES:
            xc = jnp.maximum(xc, jnp.where(lane >= shift, pltpu.roll(xc, shift, 1), NEG_BIG))
            shift *= 2
        if carry is not None:
            xc = jnp.maximum(xc, carry)
        outs.append(xc)
        carry = xc[:, LANES - 1:LANES]
    return jnp.concatenate(outs, axis=1)


def _cumsum_rows_kernel(x_ref, ck_ref, nmax_ref):
    ck, _ = _prefix_sum_lanes(x_ref[...], _tri_incl(LANES), jnp.zeros((x_ref.shape[0], 1), F32))
    ck_ref[...] = ck
    nmax_ref[...] = _prefix_max_lanes(-ck)


def _cumsum_rows(x):
    spec = pl.BlockSpec(x.shape, lambda i: (0, 0))
    return pl.pallas_call(
        _cumsum_rows_kernel,
        grid=(1,),
        in_specs=[spec],
        out_specs=[spec, spec],
        out_shape=[jax.ShapeDtypeStruct(x.shape, F32)] * 2,
        compiler_params=_params(("arbitrary",)),
        name="cumsum_rows",
    )(x)


def _wprep_kernel(w_ref, o_ref):
    for layer in range(o_ref.shape[0]):
        o_ref[layer] = w_ref[:, layer, :].astype(o_ref.dtype)


def _wprep(w_ndk, n_rows, tn):
    _, depth, d = w_ndk.shape
    return pl.pallas_call(
        _wprep_kernel,
        grid=(n_rows // tn,),
        in_specs=[pl.BlockSpec((tn, depth, d), lambda i: (i, 0, 0))],
        out_specs=pl.BlockSpec((depth, tn, d), lambda i: (0, i, 0)),
        out_shape=jax.ShapeDtypeStruct((depth, n_rows, d), BF16),
        compiler_params=_params(("arbitrary",)),
        name="wprep",
    )(w_ndk)


def _store_heads(ref, acc):
    n_heads = ref.shape[-3]
    for hd in range(n_heads):
        part = acc[:, hd * HEAD_DIM:(hd + 1) * HEAD_DIM]
        if len(ref.shape) == 3:
            ref[hd] = part
        else:
            seq = ref.shape[2]
            for b in range(ref.shape[0]):
                ref[b, hd] = part[b * seq:(b + 1) * seq, :]


def _qkv_kernel(h_ref, w_ref, qkv_ref, sbk_ref, sbv_ref, fk_ref, fv_ref):
    j = pl.program_id(1)
    acc = _dot_nt(h_ref[...], w_ref[...])
    is_q = jnp.logical_or(j == 0, j == 3)
    qkv_ref[...] = (acc * jnp.where(is_q, QK_SCALE, 1.0)).astype(qkv_ref.dtype)
    for idx, ref in ((1, sbk_ref), (2, sbv_ref), (4, fk_ref), (5, fv_ref)):
        @pl.when(j == idx)
        def _(ref=ref):
            _store_heads(ref, acc)


def _qkv_proj(h, wt, first_row, tm, layer, batch, n_heads):
    m, d = h.shape
    seq = m // batch
    tn = n_heads * HEAD_DIM
    assert first_row % tn == 0
    blk0 = first_row // tn
    if m // tm == 1:
        cache_spec = pl.BlockSpec((batch, n_heads, seq, HEAD_DIM), lambda i, j: (0, 0, 0, 0))
    else:
        assert batch == 1
        cache_spec = pl.BlockSpec((None, n_heads, tm, HEAD_DIM), lambda i, j: (0, 0, i, 0))
    cache = jax.ShapeDtypeStruct((batch, n_heads, seq, HEAD_DIM), F32)
    return pl.pallas_call(
        _qkv_kernel,
        grid=(m // tm, 6),
        in_specs=[pl.BlockSpec((tm, d), lambda i, j: (i, 0)),
                  pl.BlockSpec((None, tn, d), lambda i, j: (layer, blk0 + j, 0))],
        out_specs=[pl.BlockSpec((tm, tn), lambda i, j: (i, j))] + [cache_spec] * 4,
        out_shape=[jax.ShapeDtypeStruct((m, 6 * tn), BF16)] + [cache] * 4,
        compiler_params=_params(("arbitrary", "arbitrary")),
        name="qkv_proj",
    )(h, wt)


def _sb_tile(q, kb, vb, suffix, mask):
    z = _dot_nt(q, kb)
    sp = _softplus2(z)
    if mask is not None:
        sp = jnp.where(mask, sp, 0.0)
    s_incl = _dot(sp.astype(BF16), suffix)
    w = jnp.exp2(z - s_incl)
    if mask is not None:
        w = jnp.where(mask, w, 0.0)
    return _dot(w.astype(BF16), vb), s_incl[:, 0:1]


def _sb_live(r_ref):
    r_min = jnp.min(r_ref[...], axis=0, keepdims=True)
    return (jnp.max(jnp.exp2(-r_min)) > 0.0).astype(jnp.int32)


def _sb_accumulate(acc_ref, r_ref, pv, tot):
    r = r_ref[...]
    acc_ref[...] += jnp.exp2(-r) * pv
    r_ref[...] = r + tot


def _for_past_tiles(n_tiles, pair_fn, single_fn, live_fn=None):
    if live_fn is None:
        def body(jj, c):
            pair_fn(n_tiles - 1 - 2 * jj)
            return c

        lax.fori_loop(0, n_tiles // 2, body, 0)
        live = True
    else:
        def step(c):
            pair_fn(n_tiles - 1 - 2 * c[0])
            return c[0] + 1, live_fn()

        _, live = lax.while_loop(lambda c: jnp.logical_and(c[0] < n_tiles // 2, c[1] > 0), step,
                                 (jnp.int32(0), live_fn()))
        live = live > 0
    odd = n_tiles % 2 == 1
    if isinstance(odd, bool) and live is True:
        if odd:
            single_fn(0)
    else:
        pl.when(jnp.logical_and(odd, live))(lambda: single_fn(0))


def _sb_prompt_kernel(q_ref, k_ref, v_ref, o_ref, acc_ref, r_ref, *, tq):
    qi = pl.program_id(1)
    q = q_ref[...]
    acc_ref[...] = jnp.zeros_like(acc_ref)
    r_ref[...] = jnp.zeros_like(r_ref)
    suffix = _suffix_incl(tq)
    row, col = _iota2(tq, tq)

    def kv(j):
        st = pl.multiple_of(j * tq, tq)
        return k_ref[pl.ds(st, tq), :], v_ref[pl.ds(st, tq), :]

    def single(j, mask=None):
        _sb_accumulate(acc_ref, r_ref, *_sb_tile(q, *kv(j), suffix, mask))

    def pair(j, mask=None):
        pv_a, tot_a = _sb_tile(q, *kv(j), suffix, mask)
        pv_b, tot_b = _sb_tile(q, *kv(j - 1), suffix, None)
        _sb_accumulate(acc_ref, r_ref, pv_a + jnp.exp2(-tot_a) * pv_b, tot_a + tot_b)

    live = functools.partial(_sb_live, r_ref)

    @pl.when(qi == 0)
    def _():
        single(qi, col < row)

    @pl.when(qi > 0)
    def _():
        pair(qi, col < row)
        _for_past_tiles(qi - 1, pair, single, live)

    o_ref[...] = acc_ref[...].astype(o_ref.dtype)


def _sb_prompt(qkv, n_heads, tq):
    t = qkv.shape[0]
    return pl.pallas_call(
        functools.partial(_sb_prompt_kernel, tq=tq),
        grid=(n_heads, t // tq),
        in_specs=[
            pl.BlockSpec((tq, HEAD_DIM), lambda h, i: (i, h)),
            pl.BlockSpec((t, HEAD_DIM), lambda h, i: (0, n_heads + h)),
            pl.BlockSpec((t, HEAD_DIM), lambda h, i: (0, 2 * n_heads + h)),
        ],
        out_specs=pl.BlockSpec((tq, HEAD_DIM), lambda h, i: (i, h)),
        out_shape=jax.ShapeDtypeStruct((t, n_heads * HEAD_DIM), BF16),
        scratch_shapes=[pltpu.VMEM((tq, HEAD_DIM), F32), pltpu.VMEM((tq, HEAD_DIM), F32)],
        compiler_params=_params(("arbitrary", "arbitrary")),
        name="sb_prompt",
    )(qkv, qkv, qkv)


def _sb_sample_kernel(q_ref, kn_ref, vn_ref, kc_ref, vc_ref, o_ref, acc_ref, r_ref, *, tk):
    q = q_ref[...]
    seq = q.shape[0]
    past = kc_ref.shape[0]
    acc_ref[...] = jnp.zeros_like(acc_ref)
    r_ref[...] = jnp.zeros_like(r_ref)
    row, col = _iota2(seq, seq)
    _sb_accumulate(acc_ref, r_ref, *_sb_tile(q, kn_ref[...], vn_ref[...], _suffix_incl(seq), col < row))
    suffix = _suffix_incl(tk)

    def kv(j):
        st = pl.multiple_of(j * tk, tk)
        return kc_ref[pl.ds(st, tk), :].astype(BF16), vc_ref[pl.ds(st, tk), :].astype(BF16)

    def single(j):
        _sb_accumulate(acc_ref, r_ref, *_sb_tile(q, *kv(j), suffix, None))

    def pair(j):
        pv_a, tot_a = _sb_tile(q, *kv(j), suffix, None)
        pv_b, tot_b = _sb_tile(q, *kv(j - 1), suffix, None)
        _sb_accumulate(acc_ref, r_ref, pv_a + jnp.exp2(-tot_a) * pv_b, tot_a + tot_b)

    _for_past_tiles(past // tk, pair, single, functools.partial(_sb_live, r_ref))
    o_ref[...] = acc_ref[...].astype(o_ref.dtype)


def _sample_specs(layer, n_heads, seq, past, first_col):
    qkv = [pl.BlockSpec((seq, HEAD_DIM), lambda b, h, k=k: (b, (first_col + k) * n_heads + h)) for k in range(3)]
    cache = pl.BlockSpec((None, None, None, past, HEAD_DIM), lambda b, h: (layer, b, h, 0, 0))
    return qkv + [cache, cache]


def _sb_sample(qkv, cache_k, cache_v, layer, seq, tk):
    n = qkv.shape[0]
    nb = n // seq
    _, _, n_heads, past, _ = cache_k.shape
    return pl.pallas_call(
        functools.partial(_sb_sample_kernel, tk=tk),
        grid=(nb, n_heads),
        in_specs=_sample_specs(layer, n_heads, seq, past, 0),
        out_specs=pl.BlockSpec((seq, HEAD_DIM), lambda b, h: (b, h)),
        out_shape=jax.ShapeDtypeStruct((n, n_heads * HEAD_DIM), BF16),
        scratch_shapes=[pltpu.VMEM((seq, HEAD_DIM), F32), pltpu.VMEM((seq, HEAD_DIM), F32)],
        compiler_params=_params(("arbitrary", "arbitrary")),
        name="sb_sample",
    )(qkv, qkv, qkv, cache_k, cache_v)


def _fox_init(m_ref, l_ref, acc_ref):
    m_ref[...] = jnp.full(m_ref.shape, NEG_BIG, F32)
    l_ref[...] = jnp.zeros_like(l_ref)
    acc_ref[...] = jnp.zeros_like(acc_ref)


def _fox_tile_t(qt, kb, vt, bias, mask, m_ref, l_ref, acc_ref):
    s = _dot(kb, qt) + jnp.concatenate([bias] * (qt.shape[1] // LANES), axis=1)
    if mask is not None:
        s = jnp.where(mask, s, NEG_BIG)
    m_prev = m_ref[...]
    m_new = jnp.maximum(m_prev, jnp.max(s, axis=0, keepdims=True))
    alpha = jnp.exp2(m_prev - m_new)
    p = jnp.exp2(s - m_new)
    l_ref[...] = alpha * l_ref[...] + jnp.sum(p, axis=0, keepdims=True)
    acc_ref[...] = alpha * acc_ref[...] + _dot(vt, p.astype(BF16))
    m_ref[...] = m_new


def _fox_prompt_kernel(q_ref, k_ref, v_ref, ck_ref, o_ref,
                       vt_ref, nb_ref, kmax_ref, edge_ref, m_ref, l_ref, acc_ref, *, tq, chunk):
    h = pl.program_id(0)
    qi = pl.program_id(1)
    t = k_ref.shape[0]
    reps = tq // LANES

    @pl.when(qi == 0)
    def _():
        sel_r, _ = _iota2(LANES, LANES)
        sel = jnp.where(sel_r == h, 1.0, 0.0).astype(BF16)
        kmax_ref[...] = jnp.zeros_like(kmax_ref)

        def body(c, carry):
            st = pl.multiple_of(c * chunk, chunk)
            vt_ref[:, pl.ds(st, chunk)] = v_ref[pl.ds(st, chunk), :].astype(F32).T.astype(BF16)
            ck_h = sum(_dot(p, sel) for p in _split_bf16(ck_ref[pl.ds(st, chunk), :], 3))
            nb_ref[pl.ds(st, chunk), :] = ck_h * (-LOG2E)
            kf = k_ref[pl.ds(st, chunk), :].astype(F32)
            kmax_ref[...] = jnp.maximum(kmax_ref[...], jnp.max(jnp.sum(kf * kf, axis=1, keepdims=True)))
            return carry

        lax.fori_loop(0, t // chunk, body, 0)

    qtf = q_ref[...].astype(F32).T
    qt = qtf.astype(BF16)
    _fox_init(m_ref, l_ref, acc_ref)
    qs = pl.multiple_of(qi * tq, tq)
    c0 = nb_ref[pl.ds(qs, 1), :]
    edge_ref[...] = c0

    def tile(st, tk, mask):
        bias = nb_ref[pl.ds(st, tk), :] - c0
        _fox_tile_t(qt, k_ref[pl.ds(st, tk), :], vt_ref[:, pl.ds(st, tk)], bias, mask, m_ref, l_ref, acc_ref)
        edge_ref[...] = nb_ref[pl.ds(st, 1), :]

    q_norm2 = jnp.sum(qtf * qtf, axis=0, keepdims=True)
    reach = jnp.sqrt(q_norm2 * jnp.concatenate([kmax_ref[...]] * reps, axis=1)) * 1.001

    def live():
        edge = jnp.concatenate([edge_ref[...] - c0] * reps, axis=1)
        return (jnp.max(reach + edge - m_ref[...]) > -FOX_DEAD).astype(jnp.int32)

    @pl.when(qi == 0)
    def _():
        krow, qcol = _iota2(tq, tq)
        tile(qs, tq, krow <= qcol)

    @pl.when(qi > 0)
    def _():
        krow, qcol = _iota2(2 * tq, tq)
        tile(pl.multiple_of(qs - tq, tq), 2 * tq, krow - tq <= qcol)
        _for_past_tiles(qi - 1,
                        lambda j: tile(pl.multiple_of((j - 1) * tq, tq), 2 * tq, None),
                        lambda j: tile(pl.multiple_of(j * tq, tq), tq, None),
                        live)

    o_ref[...] = (acc_ref[...] / l_ref[...]).T.astype(o_ref.dtype)


def _fox_prompt(qkv, ck, n_heads, tq):
    t = qkv.shape[0]
    return pl.pallas_call(
        functools.partial(_fox_prompt_kernel, tq=tq, chunk=512),
        grid=(n_heads, t // tq),
        in_specs=[
            pl.BlockSpec((tq, HEAD_DIM), lambda h, i: (i, 3 * n_heads + h)),
            pl.BlockSpec((t, HEAD_DIM), lambda h, i: (0, 4 * n_heads + h)),
            pl.BlockSpec((t, HEAD_DIM), lambda h, i: (0, 5 * n_heads + h)),
            pl.BlockSpec((t, LANES), lambda h, i: (0, 0)),
        ],
        out_specs=pl.BlockSpec((tq, HEAD_DIM), lambda h, i: (i, h)),
        out_shape=jax.ShapeDtypeStruct((t, n_heads * HEAD_DIM), BF16),
        scratch_shapes=[pltpu.VMEM((HEAD_DIM, t), BF16), pltpu.VMEM((t, LANES), F32),
                        pltpu.VMEM((1, LANES), F32), pltpu.VMEM((1, LANES), F32),
                        pltpu.VMEM((1, tq), F32), pltpu.VMEM((1, tq), F32), pltpu.VMEM((HEAD_DIM, tq), F32)],
        compiler_params=_params(("arbitrary", "arbitrary")),
        name="fox_prompt",
    )(qkv, qkv, qkv, ck)


def _fox_tile(q, kb, vb, bias, mask, m_ref, l_ref, acc_ref):
    s = _dot_nt(q, kb) + bias
    if mask is not None:
        s = jnp.where(mask, s, NEG_BIG)
    m_prev = m_ref[...]
    m_new = jnp.maximum(m_prev, jnp.max(s, axis=1, keepdims=True))
    alpha = jnp.exp2(m_prev - m_new)
    p = jnp.exp2(s - m_new)
    l_ref[...] = alpha * l_ref[...] + jnp.sum(p, axis=1, keepdims=True)
    acc_ref[...] = alpha * acc_ref[...] + _dot(p.astype(BF16), vb)
    m_ref[...] = m_new


def _fox_sample_kernel(q_ref, kn_ref, vn_ref, kc_ref, vc_ref, ckp_ref, nmax_ref, ckl_ref, o_ref,
                       m_ref, l_ref, acc_ref, *, tk):
    q = q_ref[...]
    seq = q.shape[0]
    past = kc_ref.shape[0]
    n_tiles = past // tk
    _fox_init(m_ref, l_ref, acc_ref)
    row, col = _iota2(seq, seq)
    _fox_tile(q, kn_ref[...], vn_ref[...], ckl_ref[...] * (-LOG2E), col <= row, m_ref, l_ref, acc_ref)
    c0 = ckp_ref[:, past - 1:past]

    qf = q.astype(F32)
    k_abs = jnp.max(jnp.abs(kc_ref[...]))
    reach = jnp.sqrt(jnp.sum(qf * qf, axis=1, keepdims=True) * HEAD_DIM) * (k_abs * 1.01)

    def step(c):
        st = pl.multiple_of((n_tiles - 1 - c[0]) * tk, tk)
        kb = kc_ref[pl.ds(st, tk), :].astype(BF16)
        vb = vc_ref[pl.ds(st, tk), :].astype(BF16)
        bias = (c0 - ckp_ref[:, pl.ds(st, tk)]) * LOG2E
        _fox_tile(q, kb, vb, bias, None, m_ref, l_ref, acc_ref)
        lo = pl.multiple_of(jnp.maximum(st - LANES, 0), LANES)
        left = nmax_ref[:, pl.ds(lo, LANES)][:, LANES - 1:LANES]
        live = jnp.max(reach + (c0 + left) * LOG2E - m_ref[...]) > -FOX_DEAD
        return c[0] + 1, live.astype(jnp.int32)

    lax.while_loop(lambda c: jnp.logical_and(c[0] < n_tiles, c[1] > 0), step, (jnp.int32(0), jnp.int32(1)))
    o_ref[...] = (acc_ref[...] / l_ref[...]).astype(o_ref.dtype)


def _fox_sample(qkv, cache_k, cache_v, ck_past, nmax_past, ck_local, layer, seq, tk):
    n = qkv.shape[0]
    nb = n // seq
    _, _, n_heads, past, _ = cache_k.shape
    return pl.pallas_call(
        functools.partial(_fox_sample_kernel, tk=tk),
        grid=(nb, n_heads),
        in_specs=_sample_specs(layer, n_heads, seq, past, 3) + [
            pl.BlockSpec((None, None, 1, past), lambda b, h: (h, b, 0, 0)),
            pl.BlockSpec((None, None, 1, past), lambda b, h: (h, b, 0, 0)),
            pl.BlockSpec((None, None, 1, seq), lambda b, h: (h, b, 0, 0)),
        ],
        out_specs=pl.BlockSpec((seq, HEAD_DIM), lambda b, h: (b, h)),
        out_shape=jax.ShapeDtypeStruct((n, n_heads * HEAD_DIM), BF16),
        scratch_shapes=[pltpu.VMEM((seq, 1), F32), pltpu.VMEM((seq, 1), F32), pltpu.VMEM((seq, HEAD_DIM), F32)],
        compiler_params=_params(("arbitrary", "arbitrary")),
        name="fox_sample",
    )(qkv, qkv, qkv, cache_k, cache_v, ck_past, nmax_past, ck_local)


def _out_proj_kernel(ya_ref, ysb_ref, yfox_ref, w_ref, x_ref, g_ref, xo_ref, h_ref):
    c = ya_ref.shape[1]
    s = ysb_ref.shape[1]
    acc = _dot(ya_ref[...], w_ref[0:c, :])
    acc += _dot(ysb_ref[...], w_ref[c:c + s, :])
    acc += _dot(yfox_ref[...], w_ref[c + s:, :])
    x = x_ref[...] + acc
    xo_ref[...] = x
    h_ref[...] = _rmsnorm_f32(x, g_ref[...]).astype(h_ref.dtype)


def _out_proj(ya, ysb, yfox, w, x, g, tm):
    m, d = x.shape
    row = lambda width: pl.BlockSpec((tm, width), lambda i: (i, 0))
    return pl.pallas_call(
        _out_proj_kernel,
        grid=(m // tm,),
        in_specs=[row(ya.shape[1]), row(ysb.shape[1]), row(yfox.shape[1]),
                  pl.BlockSpec(w.shape, lambda i: (0, 0)), row(d), pl.BlockSpec((1, d), lambda i: (0, 0))],
        out_specs=[row(d), row(d)],
        out_shape=[jax.ShapeDtypeStruct((m, d), F32), jax.ShapeDtypeStruct((m, d), BF16)],
        compiler_params=_params(("arbitrary",)),
        name="out_proj",
    )(ya, ysb, yfox, w, x, g.reshape(1, d))


def _ffn_kernel(h_ref, wg_ref, wu_ref, wd_ref, x_ref, g_ref, *rest, emit_x):
    out_refs, acc_ref = rest[:-1], rest[-1]
    f = pl.program_id(1)

    @pl.when(f == 0)
    def _():
        acc_ref[...] = jnp.zeros_like(acc_ref)

    h = h_ref[...]
    gate = _dot(h, wg_ref[...])
    up = _dot(h, wu_ref[...])
    act = gate * jax.nn.sigmoid(gate) * up
    acc_ref[...] += _dot(act.astype(BF16), wd_ref[...])

    @pl.when(f == pl.num_programs(1) - 1)
    def _():
        x = x_ref[...] + acc_ref[...]
        normed = _rmsnorm_f32(x, g_ref[...])
        if emit_x:
            out_refs[0][...] = x
            out_refs[1][...] = normed.astype(out_refs[1].dtype)
        else:
            out_refs[0][...] = normed


def _ffn(h, wg, wu, wd, x, g, tm, tf, emit_x):
    m, d = x.shape
    ff = wg.shape[1]
    row = pl.BlockSpec((tm, d), lambda i, f: (i, 0))
    if emit_x:
        out_specs = [row, row]
        out_shape = [jax.ShapeDtypeStruct((m, d), F32), jax.ShapeDtypeStruct((m, d), BF16)]
    else:
        out_specs = [row]
        out_shape = [jax.ShapeDtypeStruct((m, d), F32)]
    return pl.pallas_call(
        functools.partial(_ffn_kernel, emit_x=emit_x),
        grid=(m // tm, ff // tf),
        in_specs=[row,
                  pl.BlockSpec((d, tf), lambda i, f: (0, f)),
                  pl.BlockSpec((d, tf), lambda i, f: (0, f)),
                  pl.BlockSpec((tf, d), lambda i, f: (f, 0)),
                  row,
                  pl.BlockSpec((1, d), lambda i, f: (0, 0))],
        out_specs=out_specs,
        out_shape=out_shape,
        scratch_shapes=[pltpu.VMEM((tm, d), F32)],
        compiler_params=_params(("arbitrary", "arbitrary")),
        name="ffn",
    )(h, wg, wu, wd, x, g.reshape(1, d))


def kernel(x_prompt, x_sample, state_conv, cache_sb_k, cache_sb_v, cache_fox_k, cache_fox_v, cache_fox_logf,
           norm1_g, w_in, b_f, conv_w, w_out, norm2_g, w_gate, w_up, w_down, final_g):
    depth = w_in.shape[0]
    bp, t, d = x_prompt.shape
    nb, seq, _ = x_sample.shape
    assert bp == 1, "prompt kernels carry conv rows and forget sums across row tiles of one stream"
    c = conv_w.shape[2]
    n_sb = cache_sb_k.shape[3]
    n_fox = cache_fox_k.shape[3]
    assert n_sb == n_fox and n_fox <= 8
    past = cache_sb_k.shape[2]
    sb_dim = n_sb * HEAD_DIM
    qkv0 = 3 * c
    qkv1 = qkv0 + 3 * sb_dim + 3 * n_fox * HEAD_DIM
    n_s = nb * seq
    per_head = lambda a: jnp.transpose(a, (0, 1, 3, 2, 4))
    csk, csv, cfk, cfv = (per_head(a) for a in (cache_sb_k, cache_sb_v, cache_fox_k, cache_fox_v))
    w_ndk = jnp.transpose(w_in, (2, 0, 1))
    wt = _wprep(w_ndk, qkv1, 512)

    xp = x_prompt.reshape(t, d)
    xs = x_sample.reshape(n_s, d)
    hp = _rmsnorm(xp, norm1_g[0], 512)
    hs = _rmsnorm(xs, norm1_g[0], n_s)
    p_small, s_small = [], []
    p_kv, s_kv = [], []
    for l in range(depth):
        bf_row = jnp.pad(b_f[l], (0, LANES - n_fox)).reshape(1, LANES)
        bf_col = jnp.pad(b_f[l], (0, 8 - n_fox)).reshape(8, 1)
        wo = w_out[l].astype(BF16)
        wg = w_gate[l].astype(BF16)
        wu = w_up[l].astype(BF16)
        wd = w_down[l].astype(BF16)
        last = l == depth - 1
        g_next = final_g if last else norm1_g[l + 1]

        ya, conv_p, logf_p, ck_p = _mix_a_prompt(hp, wt, w_ndk, qkv1, l, bf_row, conv_w[l], 512)
        qkv, *kv = _qkv_proj(hp, wt, qkv0, 1024, l, bp, n_sb)
        p_kv.append(kv)
        ysb = _sb_prompt(qkv, n_sb, 256)
        yfox = _fox_prompt(qkv, ck_p, n_fox, 256)
        xp, h2 = _out_proj(ya, ysb, yfox, wo, xp, norm2_g[l], 512)
        outs = _ffn(h2, wg, wu, wd, xp, g_next, 512, 512, not last)
        if last:
            y_prompt = outs[0]
        else:
            xp, hp = outs
        p_small.append((conv_p, logf_p[:, :n_fox]))

        logf_past = jnp.transpose(cache_fox_logf[l], (2, 0, 1)).reshape(n_fox * nb, past)
        ck_past, nmax_past = (a.reshape(n_fox, nb, 1, past) for a in _cumsum_rows(logf_past))
        ya, conv_s, logf_s, ckl = _mix_a_sample(hs, wt, w_ndk, qkv1, l, bf_col, conv_w[l],
                                                state_conv[l].reshape(nb * (CONV_W - 1), c), seq)
        ckl = ckl[:n_fox].reshape(n_fox, nb, 1, seq)
        qkv, *kv = _qkv_proj(hs, wt, qkv0, n_s, l, nb, n_sb)
        s_kv.append(kv)
        ysb = _sb_sample(qkv, csk, csv, l, seq, 256)
        yfox = _fox_sample(qkv, cfk, cfv, ck_past, nmax_past, ckl, l, seq, 512)
        xs, h2 = _out_proj(ya, ysb, yfox, wo, xs, norm2_g[l], n_s)
        outs = _ffn(h2, wg, wu, wd, xs, g_next, n_s, 512, not last)
        if last:
            y_sample = outs[0]
        else:
            xs, hs = outs
        logf_s = jnp.transpose(logf_s[:n_fox].reshape(n_fox, nb, seq), (1, 2, 0))
        s_small.append((conv_s, logf_s))

    stack = lambda news, i, shape: jnp.stack([n[i] for n in news]).reshape((depth,) + shape)
    stack_kv = lambda news: [per_head(jnp.stack([n[i] for n in news])) for i in range(4)]
    return (
        y_prompt.reshape(bp, t, d),
        y_sample.reshape(nb, seq, d),
        stack(p_small, 0, (bp, CONV_W - 1, c)),
        *stack_kv(p_kv),
        stack(p_small, 1, (bp, t, n_fox)),
        stack(s_small, 0, (nb, CONV_W - 1, c)),
        *stack_kv(s_kv),
        stack(s_small, 1, (nb, seq, n_fox)),
    )
```

```python
import functools

import jax
import jax.numpy as jnp
from jax import lax
from jax.experimental import pallas as pl
from jax.experimental.pallas import tpu as pltpu

F32 = jnp.float32
BF16 = jnp.bfloat16

NORM_EPS = 1e-5
HEAD_DIM = 128
CONV_W = 3
LANES = 128
VMEM_LIMIT = 56 * 1024 * 1024
NEG_BIG = -1e30
FOX_DEAD = 160.0
LOG2E = 1.4426950408889634
QK_SCALE = HEAD_DIM ** -0.5 * LOG2E


def _params(sem, vmem=VMEM_LIMIT):
    return pltpu.CompilerParams(dimension_semantics=sem, vmem_limit_bytes=vmem)


def _dot(a, b):
    return jnp.dot(a, b, preferred_element_type=F32)


def _dot_nt(a, b):
    return lax.dot_general(a, b, (((1,), (1,)), ((), ())), preferred_element_type=F32)


def _softplus2(z2):
    return jnp.maximum(z2, 0.0) + jnp.log(1.0 + jnp.exp2(-jnp.abs(z2))) * LOG2E


def _log_sigmoid(x):
    return jnp.minimum(x, 0.0) - jnp.log(1.0 + jnp.exp(-jnp.abs(x)))


def _split_bf16(x, parts):
    out = []
    rem = x
    for p in range(parts):
        hi = rem.astype(BF16)
        out.append(hi)
        if p + 1 < parts:
            rem = rem - hi.astype(F32)
    return out


def _iota2(n, m):
    return lax.broadcasted_iota(jnp.int32, (n, m), 0), lax.broadcasted_iota(jnp.int32, (n, m), 1)


def _tri_incl(n, seg=None):
    j, s = _iota2(n, n)
    t = jnp.where(j <= s, 1.0, 0.0)
    if seg is not None:
        t = jnp.where(j // seg == s // seg, t, 0.0)
    return t.astype(BF16)


def _suffix_incl(n):
    j, s = _iota2(n, n)
    return jnp.where(j >= s, 1.0, 0.0).astype(BF16)


def _prefix_sum_lanes(x, tri, carry):
    outs = []
    for c in range(x.shape[1] // LANES):
        xc = x[:, c * LANES:(c + 1) * LANES]
        loc = sum(_dot(p, tri) for p in _split_bf16(xc, 3))
        oc = loc if carry is None else loc + carry
        outs.append(oc)
        if carry is not None:
            carry = oc[:, LANES - 1:LANES]
    return jnp.concatenate(outs, axis=1), carry


def _rmsnorm_f32(x, g):
    ms = jnp.mean(x * x, axis=-1, keepdims=True)
    return x * lax.rsqrt(ms + NORM_EPS) * g


def _rmsnorm_kernel(x_ref, g_ref, o_ref):
    o_ref[...] = _rmsnorm_f32(x_ref[...], g_ref[...]).astype(o_ref.dtype)


def _rmsnorm(x, g, tm):
    m, d = x.shape
    return pl.pallas_call(
        _rmsnorm_kernel,
        grid=(m // tm,),
        in_specs=[pl.BlockSpec((tm, d), lambda i: (i, 0)), pl.BlockSpec((1, d), lambda i: (0, 0))],
        out_specs=pl.BlockSpec((tm, d), lambda i: (i, 0)),
        out_shape=jax.ShapeDtypeStruct((m, d), BF16),
        compiler_params=_params(("arbitrary",)),
        name="rmsnorm",
    )(x, g.reshape(1, d))


def _gated_conv(proj, cw, u_m1, u_m2):
    c = proj.shape[1] // 3
    g_b, g_c, hc = proj[:, :c], proj[:, c:2 * c], proj[:, 2 * c:]
    u = g_c * hc
    r2 = pltpu.roll(u, 2, 0)
    u1 = u_m1(pltpu.roll(u, 1, 0))
    u2 = u_m2(r2)
    conv = cw[0:1, :] * u2 + cw[1:2, :] * u1 + cw[2:3, :] * u
    return g_b * conv, u, r2


def _mix_a_prompt_kernel(h_ref, wa_ref, wtail_ref, bf_ref, cw_ref,
                         ya_ref, conv_ref, logf_ref, ck_ref, uprev_ref, carry_ref, wrows_ref, wf_ref, *, layer):
    i = pl.program_id(0)

    @pl.when(i == 0)
    def _():
        uprev_ref[...] = jnp.zeros_like(uprev_ref)
        carry_ref[...] = jnp.zeros_like(carry_ref)
        wrows_ref[...] = jnp.zeros_like(wrows_ref)
        wrows_ref[0:wtail_ref.shape[0], :] = wtail_ref[:, layer, :]
        wf_ref[...] = wrows_ref[...].T.astype(wf_ref.dtype)

    h = h_ref[...]
    proj = _dot_nt(h, wa_ref[...])
    tm = proj.shape[0]
    rows = lax.broadcasted_iota(jnp.int32, (tm, proj.shape[1] // 3), 0)
    prev = uprev_ref[...]
    m1 = lambda r1: jnp.where(rows == 0, prev[1:2, :], r1)
    m2 = lambda r2: jnp.where(rows == 0, prev[0:1, :], jnp.where(rows == 1, prev[1:2, :], r2))
    ya, _, r2 = _gated_conv(proj, cw_ref[...], m1, m2)
    ya_ref[...] = ya.astype(ya_ref.dtype)
    tail = r2[0:2, :]
    uprev_ref[0:2, :] = tail
    conv_ref[...] = tail

    logf = _log_sigmoid(_dot(h, wf_ref[...]) + bf_ref[...])
    logf_ref[...] = logf
    lower = _suffix_incl(tm)
    ck = sum(_dot(lower, p) for p in _split_bf16(logf, 3)) + carry_ref[...]
    ck_ref[...] = ck
    carry_ref[...] = ck[tm - 1:tm, :]


def _tail_spec(w_ndk, first_row):
    n, depth, d = w_ndk.shape
    assert first_row % (n - first_row) == 0
    return pl.BlockSpec((n - first_row, depth, d), lambda i: (first_row // (n - first_row), 0, 0))


def _mix_a_prompt(h, wt, w_ndk, tail_row, layer, bfp, cw, tm):
    t, d = h.shape
    c = cw.shape[1]
    c3 = 3 * c
    return pl.pallas_call(
        functools.partial(_mix_a_prompt_kernel, layer=layer),
        grid=(t // tm,),
        in_specs=[
            pl.BlockSpec((tm, d), lambda i: (i, 0)),
            pl.BlockSpec((None, c3, d), lambda i: (layer, 0, 0)),
            _tail_spec(w_ndk, tail_row),
            pl.BlockSpec((1, LANES), lambda i: (0, 0)),
            pl.BlockSpec((CONV_W, c), lambda i: (0, 0)),
        ],
        out_specs=[
            pl.BlockSpec((tm, c), lambda i: (i, 0)),
            pl.BlockSpec((CONV_W - 1, c), lambda i: (0, 0)),
            pl.BlockSpec((tm, LANES), lambda i: (i, 0)),
            pl.BlockSpec((tm, LANES), lambda i: (i, 0)),
        ],
        out_shape=[
            jax.ShapeDtypeStruct((t, c), BF16),
            jax.ShapeDtypeStruct((CONV_W - 1, c), F32),
            jax.ShapeDtypeStruct((t, LANES), F32),
            jax.ShapeDtypeStruct((t, LANES), F32),
        ],
        scratch_shapes=[pltpu.VMEM((8, c), F32), pltpu.VMEM((1, LANES), F32),
                        pltpu.VMEM((LANES, d), F32), pltpu.VMEM((d, LANES), BF16)],
        compiler_params=_params(("arbitrary",)),
        name="mix_a_prompt",
    )(h, wt, w_ndk, bfp, cw)


def _mix_a_sample_kernel(h_ref, wa_ref, wtail_ref, bf_ref, cw_ref, st_ref,
                         ya_ref, conv_ref, logf_ref, ckl_ref, u_ref, wrows_ref, *, seq, layer):
    h = h_ref[...]
    proj = _dot_nt(h, wa_ref[...])
    n = proj.shape[0]
    nb = n // seq
    rows = lax.broadcasted_iota(jnp.int32, (n, proj.shape[1] // 3), 0)

    def m1(r1):
        for b in range(nb):
            r1 = jnp.where(rows == b * seq, st_ref[2 * b + 1:2 * b + 2, :], r1)
        return r1

    def m2(r2):
        for b in range(nb):
            r2 = jnp.where(rows == b * seq, st_ref[2 * b:2 * b + 1, :], r2)
            r2 = jnp.where(rows == b * seq + 1, st_ref[2 * b + 1:2 * b + 2, :], r2)
        return r2

    ya, u, _ = _gated_conv(proj, cw_ref[...], m1, m2)
    ya_ref[...] = ya.astype(ya_ref.dtype)
    u_ref[...] = u
    for b in range(nb):
        conv_ref[2 * b:2 * b + 2, :] = u_ref[(b + 1) * seq - 2:(b + 1) * seq, :]

    wrows_ref[...] = jnp.zeros_like(wrows_ref)
    wrows_ref[0:wtail_ref.shape[0], :] = wtail_ref[:, layer, :]
    logit = _dot_nt(wrows_ref[...].astype(BF16), h)[0:8, :]
    logf = _log_sigmoid(logit + bf_ref[...])
    logf_ref[...] = logf
    ckl, _ = _prefix_sum_lanes(logf, _tri_incl(LANES, seg=seq), None)
    ckl_ref[...] = ckl


def _mix_a_sample(h, wt, w_ndk, tail_row, layer, bfp, cw, state, seq):
    n, d = h.shape
    c = cw.shape[1]
    c3 = 3 * c
    nb = n // seq
    full = lambda shape: pl.BlockSpec(shape, lambda i: tuple(0 for _ in shape))
    return pl.pallas_call(
        functools.partial(_mix_a_sample_kernel, seq=seq, layer=layer),
        grid=(1,),
        in_specs=[full((n, d)), pl.BlockSpec((None, c3, d), lambda i: (layer, 0, 0)),
                  _tail_spec(w_ndk, tail_row), full((8, 1)), full((CONV_W, c)),
                  full((nb * (CONV_W - 1), c))],
        out_specs=[full((n, c)), full((nb * (CONV_W - 1), c)), full((8, n)), full((8, n))],
        out_shape=[
            jax.ShapeDtypeStruct((n, c), BF16),
            jax.ShapeDtypeStruct((nb * (CONV_W - 1), c), F32),
            jax.ShapeDtypeStruct((8, n), F32),
            jax.ShapeDtypeStruct((8, n), F32),
        ],
        scratch_shapes=[pltpu.VMEM((n, c), F32), pltpu.VMEM((16, d), F32)],
        compiler_params=_params(("arbitrary",)),
        name="mix_a_sample",
    )(h, wt, w_ndk, bfp, cw, state)


def _prefix_max_lanes(x):
    lane = lax.broadcasted_iota(jnp.int32, (x.shape[0], LANES), 1)
    outs, carry = [], None
    for c in range(x.shape[1] // LANES):
        xc = x[:, c * LANES:(c + 1) * LANES]
        shift = 1
        while shift < LANES:
            xc = jnp.maximum(xc, jnp.where(lane >= shift, pltpu.roll(xc, shift, 1), NEG_BIG))
            shift *= 2
        if carry is not None:
            xc = jnp.maximum(xc, carry)
        outs.append(xc)
        carry = xc[:, LANES - 1:LANES]
    return jnp.concatenate(outs, axis=1)


def _cumsum_rows_kernel(x_ref, ck_ref, nmax_ref):
    ck, _ = _prefix_sum_lanes(x_ref[...], _tri_incl(LANES), jnp.zeros((x_ref.shape[0], 1), F32))
    ck_ref[...] = ck
    nmax_ref[...] = _prefix_max_lanes(-ck)


def _cumsum_rows(x):
    spec = pl.BlockSpec(x.shape, lambda i: (0, 0))
    return pl.pallas_call(
        _cumsum_rows_kernel,
        grid=(1,),
        in_specs=[spec],
        out_specs=[spec, spec],
        out_shape=[jax.ShapeDtypeStruct(x.shape, F32)] * 2,
        compiler_params=_params(("arbitrary",)),
        name="cumsum_rows",
    )(x)


def _wprep_kernel(w_ref, o_ref):
    for layer in range(o_ref.shape[0]):
        o_ref[layer] = w_ref[:, layer, :].astype(o_ref.dtype)


def _wprep(w_ndk, n_rows, tn):
    _, depth, d = w_ndk.shape
    return pl.pallas_call(
        _wprep_kernel,
        grid=(n_rows // tn,),
        in_specs=[pl.BlockSpec((tn, depth, d), lambda i: (i, 0, 0))],
        out_specs=pl.BlockSpec((depth, tn, d), lambda i: (0, i, 0)),
        out_shape=jax.ShapeDtypeStruct((depth, n_rows, d), BF16),
        compiler_params=_params(("arbitrary",)),
        name="wprep",
    )(w_ndk)


def _store_heads(ref, acc):
    n_heads = ref.shape[-3]
    for hd in range(n_heads):
        part = acc[:, hd * HEAD_DIM:(hd + 1) * HEAD_DIM]
        if len(ref.shape) == 3:
            ref[hd] = part
        else:
            seq = ref.shape[2]
            for b in range(ref.shape[0]):
                ref[b, hd] = part[b * seq:(b + 1) * seq, :]


def _qkv_kernel(h_ref, w_ref, qkv_ref, sbk_ref, sbv_ref, fk_ref, fv_ref):
    j = pl.program_id(1)
    acc = _dot_nt(h_ref[...], w_ref[...])
    is_q = jnp.logical_or(j == 0, j == 3)
    qkv_ref[...] = (acc * jnp.where(is_q, QK_SCALE, 1.0)).astype(qkv_ref.dtype)
    for idx, ref in ((1, sbk_ref), (2, sbv_ref), (4, fk_ref), (5, fv_ref)):
        @pl.when(j == idx)
        def _(ref=ref):
            _store_heads(ref, acc)


def _qkv_proj(h, wt, first_row, tm, layer, batch, n_heads):
    m, d = h.shape
    seq = m // batch
    tn = n_heads * HEAD_DIM
    assert first_row % tn == 0
    blk0 = first_row // tn
    if m // tm == 1:
        cache_spec = pl.BlockSpec((batch, n_heads, seq, HEAD_DIM), lambda i, j: (0, 0, 0, 0))
    else:
        assert batch == 1
        cache_spec = pl.BlockSpec((None, n_heads, tm, HEAD_DIM), lambda i, j: (0, 0, i, 0))
    cache = jax.ShapeDtypeStruct((batch, n_heads, seq, HEAD_DIM), F32)
    return pl.pallas_call(
        _qkv_kernel,
        grid=(m // tm, 6),
        in_specs=[pl.BlockSpec((tm, d), lambda i, j: (i, 0)),
                  pl.BlockSpec((None, tn, d), lambda i, j: (layer, blk0 + j, 0))],
        out_specs=[pl.BlockSpec((tm, tn), lambda i, j: (i, j))] + [cache_spec] * 4,
        out_shape=[jax.ShapeDtypeStruct((m, 6 * tn), BF16)] + [cache] * 4,
        compiler_params=_params(("arbitrary", "arbitrary")),
        name="qkv_proj",
    )(h, wt)


def _sb_tile(q, kb, vb, suffix, mask):
    z = _dot_nt(q, kb)
    sp = _softplus2(z)
    if mask is not None:
        sp = jnp.where(mask, sp, 0.0)
    s_incl = _dot(sp.astype(BF16), suffix)
    w = jnp.exp2(z - s_incl)
    if mask is not None:
        w = jnp.where(mask, w, 0.0)
    return _dot(w.astype(BF16), vb), s_incl[:, 0:1]


def _sb_live(r_ref):
    r_min = jnp.min(r_ref[...], axis=0, keepdims=True)
    return (jnp.max(jnp.exp2(-r_min)) > 0.0).astype(jnp.int32)


def _sb_accumulate(acc_ref, r_ref, pv, tot):
    r = r_ref[...]
    acc_ref[...] += jnp.exp2(-r) * pv
    r_ref[...] = r + tot


def _for_past_tiles(n_tiles, pair_fn, single_fn, live_fn=None):
    if live_fn is None:
        def body(jj, c):
            pair_fn(n_tiles - 1 - 2 * jj)
            return c

        lax.fori_loop(0, n_tiles // 2, body, 0)
        live = True
    else:
        def step(c):
            pair_fn(n_tiles - 1 - 2 * c[0])
            return c[0] + 1, live_fn()

        _, live = lax.while_loop(lambda c: jnp.logical_and(c[0] < n_tiles // 2, c[1] > 0), step,
                                 (jnp.int32(0), live_fn()))
        live = live > 0
    odd = n_tiles % 2 == 1
    if isinstance(odd, bool) and live is True:
        if odd:
            single_fn(0)
    else:
        pl.when(jnp.logical_and(odd, live))(lambda: single_fn(0))


def _sb_prompt_kernel(q_ref, k_ref, v_ref, o_ref, acc_ref, r_ref, *, tq):
    qi = pl.program_id(1)
    acc_ref[...] = jnp.zeros_like(acc_ref)
    r_ref[...] = jnp.zeros_like(r_ref)
    suffix = _suffix_incl(tq)
    row, col = _iota2(tq, tq)

    def kv(j):
        st = pl.multiple_of(j * tq, tq)
        return k_ref[pl.ds(st, tq), :], v_ref[pl.ds(st, tq), :]

    def block(s):
        q = q_ref[s * tq:(s + 1) * tq, :]
        acc, r = acc_ref.at[s], r_ref.at[s]

        def single(j, mask=None):
            _sb_accumulate(acc, r, *_sb_tile(q, *kv(j), suffix, mask))

        def pair(j, mask=None):
            pv_a, tot_a = _sb_tile(q, *kv(j), suffix, mask)
            pv_b, tot_b = _sb_tile(q, *kv(j - 1), suffix, None)
            _sb_accumulate(acc, r, pv_a + jnp.exp2(-tot_a) * pv_b, tot_a + tot_b)

        return single, pair, functools.partial(_sb_live, r)

    single0, pair0, live0 = block(0)
    single1, pair1, live1 = block(1)
    g0, g1 = 2 * qi, 2 * qi + 1

    @pl.when(qi == 0)
    def _():
        single0(g0, col < row)
        pair1(g1, col < row)

    @pl.when(qi > 0)
    def _():
        pair0(g0, col < row)
        pair1(g1, col < row)
        _for_past_tiles(g0 - 1, pair0, single0, live0)
        _for_past_tiles(g1 - 1, pair1, single1, live1)

    for s in range(2):
        o_ref[s * tq:(s + 1) * tq, :] = acc_ref[s].astype(o_ref.dtype)


def _sb_prompt(qkv, n_heads, tq):
    t = qkv.shape[0]
    return pl.pallas_call(
        functools.partial(_sb_prompt_kernel, tq=tq),
        grid=(n_heads, t // (2 * tq)),
        in_specs=[
            pl.BlockSpec((2 * tq, HEAD_DIM), lambda h, i: (i, h)),
            pl.BlockSpec((t, HEAD_DIM), lambda h, i: (0, n_heads + h)),
            pl.BlockSpec((t, HEAD_DIM), lambda h, i: (0, 2 * n_heads + h)),
        ],
        out_specs=pl.BlockSpec((2 * tq, HEAD_DIM), lambda h, i: (i, h)),
        out_shape=jax.ShapeDtypeStruct((t, n_heads * HEAD_DIM), BF16),
        scratch_shapes=[pltpu.VMEM((2, tq, HEAD_DIM), F32), pltpu.VMEM((2, tq, HEAD_DIM), F32)],
        compiler_params=_params(("arbitrary", "arbitrary")),
        name="sb_prompt",
    )(qkv, qkv, qkv)


def _sb_sample_kernel(q_ref, kn_ref, vn_ref, kc_ref, vc_ref, o_ref, acc_ref, r_ref, *, tk):
    q = q_ref[...]
    seq = q.shape[0]
    past = kc_ref.shape[0]
    acc_ref[...] = jnp.zeros_like(acc_ref)
    r_ref[...] = jnp.zeros_like(r_ref)
    row, col = _iota2(seq, seq)
    _sb_accumulate(acc_ref, r_ref, *_sb_tile(q, kn_ref[...], vn_ref[...], _suffix_incl(seq), col < row))
    suffix = _suffix_incl(tk)

    def kv(j):
        st = pl.multiple_of(j * tk, tk)
        return kc_ref[pl.ds(st, tk), :].astype(BF16), vc_ref[pl.ds(st, tk), :].astype(BF16)

    def single(j):
        _sb_accumulate(acc_ref, r_ref, *_sb_tile(q, *kv(j), suffix, None))

    def pair(j):
        pv_a, tot_a = _sb_tile(q, *kv(j), suffix, None)
        pv_b, tot_b = _sb_tile(q, *kv(j - 1), suffix, None)
        _sb_accumulate(acc_ref, r_ref, pv_a + jnp.exp2(-tot_a) * pv_b, tot_a + tot_b)

    _for_past_tiles(past // tk, pair, single, functools.partial(_sb_live, r_ref))
    o_ref[...] = acc_ref[...].astype(o_ref.dtype)


def _sample_specs(layer, n_heads, seq, past, first_col):
    qkv = [pl.BlockSpec((seq, HEAD_DIM), lambda b, h, k=k: (b, (first_col + k) * n_heads + h)) for k in range(3)]
    cache = pl.BlockSpec((None, None, None, past, HEAD_DIM), lambda b, h: (layer, b, h, 0, 0))
    return qkv + [cache, cache]


def _sb_sample(qkv, cache_k, cache_v, layer, seq, tk):
    n = qkv.shape[0]
    nb = n // seq
    _, _, n_heads, past, _ = cache_k.shape
    return pl.pallas_call(
        functools.partial(_sb_sample_kernel, tk=tk),
        grid=(nb, n_heads),
        in_specs=_sample_specs(layer, n_heads, seq, past, 0),
        out_specs=pl.BlockSpec((seq, HEAD_DIM), lambda b, h: (b, h)),
        out_shape=jax.ShapeDtypeStruct((n, n_heads * HEAD_DIM), BF16),
        scratch_shapes=[pltpu.VMEM((seq, HEAD_DIM), F32), pltpu.VMEM((seq, HEAD_DIM), F32)],
        compiler_params=_params(("arbitrary", "arbitrary")),
        name="sb_sample",
    )(qkv, qkv, qkv, cache_k, cache_v)


def _fox_init(m_ref, l_ref, acc_ref):
    m_ref[...] = jnp.full(m_ref.shape, NEG_BIG, F32)
    l_ref[...] = jnp.zeros_like(l_ref)
    acc_ref[...] = jnp.zeros_like(acc_ref)


def _fox_tile_t(qt, kb, vt, bias, mask, m_ref, l_ref, acc_ref):
    s = _dot(kb, qt) + jnp.concatenate([bias] * (qt.shape[1] // LANES), axis=1)
    if mask is not None:
        s = jnp.where(mask, s, NEG_BIG)
    m_prev = m_ref[...]
    m_new = jnp.maximum(m_prev, jnp.max(s, axis=0, keepdims=True))
    alpha = jnp.exp2(m_prev - m_new)
    p = jnp.exp2(s - m_new)
    l_ref[...] = alpha * l_ref[...] + jnp.sum(p, axis=0, keepdims=True)
    acc_ref[...] = alpha * acc_ref[...] + _dot(vt, p.astype(BF16))
    m_ref[...] = m_new


def _fox_prompt_kernel(q_ref, k_ref, v_ref, ck_ref, o_ref,
                       vt_ref, nb_ref, kmax_ref, edge_ref, m_ref, l_ref, acc_ref, *, tq, chunk):
    h = pl.program_id(0)
    qi = pl.program_id(1)
    t = k_ref.shape[0]
    reps = tq // LANES

    @pl.when(qi == 0)
    def _():
        sel_r, _ = _iota2(LANES, LANES)
        sel = jnp.where(sel_r == h, 1.0, 0.0).astype(BF16)
        kmax_ref[...] = jnp.zeros_like(kmax_ref)

        def body(c, carry):
            st = pl.multiple_of(c * chunk, chunk)
            vt_ref[:, pl.ds(st, chunk)] = v_ref[pl.ds(st, chunk), :].astype(F32).T.astype(BF16)
            ck_h = sum(_dot(p, sel) for p in _split_bf16(ck_ref[pl.ds(st, chunk), :], 3))
            nb_ref[pl.ds(st, chunk), :] = ck_h * (-LOG2E)
            kf = k_ref[pl.ds(st, chunk), :].astype(F32)
            kmax_ref[...] = jnp.maximum(kmax_ref[...], jnp.max(jnp.sum(kf * kf, axis=1, keepdims=True)))
            return carry

        lax.fori_loop(0, t // chunk, body, 0)

    qtf = q_ref[...].astype(F32).T
    qt = qtf.astype(BF16)
    _fox_init(m_ref, l_ref, acc_ref)
    qs = pl.multiple_of(qi * tq, tq)
    c0 = nb_ref[pl.ds(qs, 1), :]
    edge_ref[...] = c0

    def tile(st, tk, mask):
        bias = nb_ref[pl.ds(st, tk), :] - c0
        _fox_tile_t(qt, k_ref[pl.ds(st, tk), :], vt_ref[:, pl.ds(st, tk)], bias, mask, m_ref, l_ref, acc_ref)
        edge_ref[...] = nb_ref[pl.ds(st, 1), :]

    q_norm2 = jnp.sum(qtf * qtf, axis=0, keepdims=True)
    reach = jnp.sqrt(q_norm2 * jnp.concatenate([kmax_ref[...]] * reps, axis=1)) * 1.001

    def live():
        edge = jnp.concatenate([edge_ref[...] - c0] * reps, axis=1)
        return (jnp.max(reach + edge - m_ref[...]) > -FOX_DEAD).astype(jnp.int32)

    @pl.when(qi == 0)
    def _():
        krow, qcol = _iota2(tq, tq)
        tile(qs, tq, krow <= qcol)

    @pl.when(qi > 0)
    def _():
        krow, qcol = _iota2(2 * tq, tq)
        tile(pl.multiple_of(qs - tq, tq), 2 * tq, krow - tq <= qcol)
        _for_past_tiles(qi - 1,
                        lambda j: tile(pl.multiple_of((j - 1) * tq, tq), 2 * tq, None),
                        lambda j: tile(pl.multiple_of(j * tq, tq), tq, None),
                        live)

    o_ref[...] = (acc_ref[...] / l_ref[...]).T.astype(o_ref.dtype)


def _fox_prompt(qkv, ck, n_heads, tq):
    t = qkv.shape[0]
    return pl.pallas_call(
        functools.partial(_fox_prompt_kernel, tq=tq, chunk=512),
        grid=(n_heads, t // tq),
        in_specs=[
            pl.BlockSpec((tq, HEAD_DIM), lambda h, i: (i, 3 * n_heads + h)),
            pl.BlockSpec((t, HEAD_DIM), lambda h, i: (0, 4 * n_heads + h)),
            pl.BlockSpec((t, HEAD_DIM), lambda h, i: (0, 5 * n_heads + h)),
            pl.BlockSpec((t, LANES), lambda h, i: (0, 0)),
        ],
        out_specs=pl.BlockSpec((tq, HEAD_DIM), lambda h, i: (i, h)),
        out_shape=jax.ShapeDtypeStruct((t, n_heads * HEAD_DIM), BF16),
        scratch_shapes=[pltpu.VMEM((HEAD_DIM, t), BF16), pltpu.VMEM((t, LANES), F32),
                        pltpu.VMEM((1, LANES), F32), pltpu.VMEM((1, LANES), F32),
                        pltpu.VMEM((1, tq), F32), pltpu.VMEM((1, tq), F32), pltpu.VMEM((HEAD_DIM, tq), F32)],
        compiler_params=_params(("arbitrary", "arbitrary")),
        name="fox_prompt",
    )(qkv, qkv, qkv, ck)


def _fox_tile(q, kb, vb, bias, mask, m_ref, l_ref, acc_ref):
    s = _dot_nt(q, kb) + bias
    if mask is not None:
        s = jnp.where(mask, s, NEG_BIG)
    m_prev = m_ref[...]
    m_new = jnp.maximum(m_prev, jnp.max(s, axis=1, keepdims=True))
    alpha = jnp.exp2(m_prev - m_new)
    p = jnp.exp2(s - m_new)
    l_ref[...] = alpha * l_ref[...] + jnp.sum(p, axis=1, keepdims=True)
    acc_ref[...] = alpha * acc_ref[...] + _dot(p.astype(BF16), vb)
    m_ref[...] = m_new


def _fox_sample_kernel(q_ref, kn_ref, vn_ref, kc_ref, vc_ref, ckp_ref, nmax_ref, ckl_ref, o_ref,
                       m_ref, l_ref, acc_ref, *, tk):
    q = q_ref[...]
    seq = q.shape[0]
    past = kc_ref.shape[0]
    n_tiles = past // tk
    _fox_init(m_ref, l_ref, acc_ref)
    row, col = _iota2(seq, seq)
    _fox_tile(q, kn_ref[...], vn_ref[...], ckl_ref[...] * (-LOG2E), col <= row, m_ref, l_ref, acc_ref)
    c0 = ckp_ref[:, past - 1:past]

    qf = q.astype(F32)
    k_abs = jnp.max(jnp.abs(kc_ref[...]))
    reach = jnp.sqrt(jnp.sum(qf * qf, axis=1, keepdims=True) * HEAD_DIM) * (k_abs * 1.01)

    def step(c):
        st = pl.multiple_of((n_tiles - 1 - c[0]) * tk, tk)
        kb = kc_ref[pl.ds(st, tk), :].astype(BF16)
        vb = vc_ref[pl.ds(st, tk), :].astype(BF16)
        bias = (c0 - ckp_ref[:, pl.ds(st, tk)]) * LOG2E
        _fox_tile(q, kb, vb, bias, None, m_ref, l_ref, acc_ref)
        lo = pl.multiple_of(jnp.maximum(st - LANES, 0), LANES)
        left = nmax_ref[:, pl.ds(lo, LANES)][:, LANES - 1:LANES]
        live = jnp.max(reach + (c0 + left) * LOG2E - m_ref[...]) > -FOX_DEAD
        return c[0] + 1, live.astype(jnp.int32)

    lax.while_loop(lambda c: jnp.logical_and(c[0] < n_tiles, c[1] > 0), step, (jnp.int32(0), jnp.int32(1)))
    o_ref[...] = (acc_ref[...] / l_ref[...]).astype(o_ref.dtype)


def _fox_sample(qkv, cache_k, cache_v, ck_past, nmax_past, ck_local, layer, seq, tk):
    n = qkv.shape[0]
    nb = n // seq
    _, _, n_heads, past, _ = cache_k.shape
    return pl.pallas_call(
        functools.partial(_fox_sample_kernel, tk=tk),
        grid=(nb, n_heads),
        in_specs=_sample_specs(layer, n_heads, seq, past, 3) + [
            pl.BlockSpec((None, None, 1, past), lambda b, h: (h, b, 0, 0)),
            pl.BlockSpec((None, None, 1, past), lambda b, h: (h, b, 0, 0)),
            pl.BlockSpec((None, None, 1, seq), lambda b, h: (h, b, 0, 0)),
        ],
        out_specs=pl.BlockSpec((seq, HEAD_DIM), lambda b, h: (b, h)),
        out_shape=jax.ShapeDtypeStruct((n, n_heads * HEAD_DIM), BF16),
        scratch_shapes=[pltpu.VMEM((seq, 1), F32), pltpu.VMEM((seq, 1), F32), pltpu.VMEM((seq, HEAD_DIM), F32)],
        compiler_params=_params(("arbitrary", "arbitrary")),
        name="fox_sample",
    )(qkv, qkv, qkv, cache_k, cache_v, ck_past, nmax_past, ck_local)


def _out_proj_kernel(ya_ref, ysb_ref, yfox_ref, w_ref, x_ref, g_ref, xo_ref, h_ref):
    c = ya_ref.shape[1]
    s = ysb_ref.shape[1]
    acc = _dot(ya_ref[...], w_ref[0:c, :])
    acc += _dot(ysb_ref[...], w_ref[c:c + s, :])
    acc += _dot(yfox_ref[...], w_ref[c + s:, :])
    x = x_ref[...] + acc
    xo_ref[...] = x
    h_ref[...] = _rmsnorm_f32(x, g_ref[...]).astype(h_ref.dtype)


def _out_proj(ya, ysb, yfox, w, layer, x, g, tm):
    m, d = x.shape
    row = lambda width: pl.BlockSpec((tm, width), lambda i: (i, 0))
    return pl.pallas_call(
        _out_proj_kernel,
        grid=(m // tm,),
        in_specs=[row(ya.shape[1]), row(ysb.shape[1]), row(yfox.shape[1]),
                  pl.BlockSpec((None,) + w.shape[1:], lambda i: (layer, 0, 0)), row(d),
                  pl.BlockSpec((1, d), lambda i: (0, 0))],
        out_specs=[row(d), row(d)],
        out_shape=[jax.ShapeDtypeStruct((m, d), F32), jax.ShapeDtypeStruct((m, d), BF16)],
        compiler_params=_params(("arbitrary",)),
        name="out_proj",
    )(ya, ysb, yfox, w, x, g.reshape(1, d))


def _ffn_kernel(h_ref, wg_ref, wu_ref, wd_ref, x_ref, g_ref, *rest, emit_x):
    out_refs, acc_ref = rest[:-1], rest[-1]
    f = pl.program_id(1)

    @pl.when(f == 0)
    def _():
        acc_ref[...] = jnp.zeros_like(acc_ref)

    h = h_ref[...]
    gate = _dot(h, wg_ref[...])
    up = _dot(h, wu_ref[...])
    act = gate * jax.nn.sigmoid(gate) * up
    acc_ref[...] += _dot(act.astype(BF16), wd_ref[...])

    @pl.when(f == pl.num_programs(1) - 1)
    def _():
        x = x_ref[...] + acc_ref[...]
        normed = _rmsnorm_f32(x, g_ref[...])
        if emit_x:
            out_refs[0][...] = x
            out_refs[1][...] = normed.astype(out_refs[1].dtype)
        else:
            out_refs[0][...] = normed


def _ffn(h, wg, wu, wd, layer, x, g, tm, tf, emit_x):
    m, d = x.shape
    ff = wg.shape[2]
    row = pl.BlockSpec((tm, d), lambda i, f: (i, 0))
    if emit_x:
        out_specs = [row, row]
        out_shape = [jax.ShapeDtypeStruct((m, d), F32), jax.ShapeDtypeStruct((m, d), BF16)]
    else:
        out_specs = [row]
        out_shape = [jax.ShapeDtypeStruct((m, d), F32)]
    return pl.pallas_call(
        functools.partial(_ffn_kernel, emit_x=emit_x),
        grid=(m // tm, ff // tf),
        in_specs=[row,
                  pl.BlockSpec((None, d, tf), lambda i, f: (layer, 0, f)),
                  pl.BlockSpec((None, d, tf), lambda i, f: (layer, 0, f)),
                  pl.BlockSpec((None, tf, d), lambda i, f: (layer, f, 0)),
                  row,
                  pl.BlockSpec((1, d), lambda i, f: (0, 0))],
        out_specs=out_specs,
        out_shape=out_shape,
        scratch_shapes=[pltpu.VMEM((tm, d), F32)],
        compiler_params=_params(("arbitrary", "arbitrary")),
        name="ffn",
    )(h, wg, wu, wd, x, g.reshape(1, d))


def kernel(x_prompt, x_sample, state_conv, cache_sb_k, cache_sb_v, cache_fox_k, cache_fox_v, cache_fox_logf,
           norm1_g, w_in, b_f, conv_w, w_out, norm2_g, w_gate, w_up, w_down, final_g):
    depth = w_in.shape[0]
    bp, t, d = x_prompt.shape
    nb, seq, _ = x_sample.shape
    assert bp == 1, "prompt kernels carry conv rows and forget sums across row tiles of one stream"
    c = conv_w.shape[2]
    n_sb = cache_sb_k.shape[3]
    n_fox = cache_fox_k.shape[3]
    assert n_sb == n_fox and n_fox <= 8
    past = cache_sb_k.shape[2]
    sb_dim = n_sb * HEAD_DIM
    qkv0 = 3 * c
    qkv1 = qkv0 + 3 * sb_dim + 3 * n_fox * HEAD_DIM
    n_s = nb * seq
    per_head = lambda a: jnp.transpose(a, (0, 1, 3, 2, 4))
    csk, csv, cfk, cfv = (per_head(a) for a in (cache_sb_k, cache_sb_v, cache_fox_k, cache_fox_v))
    w_ndk = jnp.transpose(w_in, (2, 0, 1))
    wt = _wprep(w_ndk, qkv1, 512)
    wo, wg, wu, wd = (w.astype(BF16) for w in (w_out, w_gate, w_up, w_down))

    xp = x_prompt.reshape(t, d)
    xs = x_sample.reshape(n_s, d)
    hp = _rmsnorm(xp, norm1_g[0], 512)
    hs = _rmsnorm(xs, norm1_g[0], n_s)
    p_small, s_small = [], []
    p_kv, s_kv = [], []
    for l in range(depth):
        bf_row = jnp.pad(b_f[l], (0, LANES - n_fox)).reshape(1, LANES)
        bf_col = jnp.pad(b_f[l], (0, 8 - n_fox)).reshape(8, 1)
        last = l == depth - 1
        g_next = final_g if last else norm1_g[l + 1]

        ya, conv_p, logf_p, ck_p = _mix_a_prompt(hp, wt, w_ndk, qkv1, l, bf_row, conv_w[l], 512)
        qkv, *kv = _qkv_proj(hp, wt, qkv0, 1024, l, bp, n_sb)
        p_kv.append(kv)
        ysb = _sb_prompt(qkv, n_sb, 256)
        yfox = _fox_prompt(qkv, ck_p, n_fox, 256)
        xp, h2 = _out_proj(ya, ysb, yfox, wo, l, xp, norm2_g[l], 512)
        outs = _ffn(h2, wg, wu, wd, l, xp, g_next, 512, 512, not last)
        if last:
            y_prompt = outs[0]
        else:
            xp, hp = outs
        p_small.append((conv_p, logf_p[:, :n_fox]))

        logf_past = jnp.transpose(cache_fox_logf[l], (2, 0, 1)).reshape(n_fox * nb, past)
        ck_past, nmax_past = (a.reshape(n_fox, nb, 1, past) for a in _cumsum_rows(logf_past))
        ya, conv_s, logf_s, ckl = _mix_a_sample(hs, wt, w_ndk, qkv1, l, bf_col, conv_w[l],
                                                state_conv[l].reshape(nb * (CONV_W - 1), c), seq)
        ckl = ckl[:n_fox].reshape(n_fox, nb, 1, seq)
        qkv, *kv = _qkv_proj(hs, wt, qkv0, n_s, l, nb, n_sb)
        s_kv.append(kv)
        ysb = _sb_sample(qkv, csk, csv, l, seq, 256)
        yfox = _fox_sample(qkv, cfk, cfv, ck_past, nmax_past, ckl, l, seq, 512)
        xs, h2 = _out_proj(ya, ysb, yfox, wo, l, xs, norm2_g[l], n_s)
        outs = _ffn(h2, wg, wu, wd, l, xs, g_next, n_s, 512, not last)
        if last:
            y_sample = outs[0]
        else:
            xs, hs = outs
        logf_s = jnp.transpose(logf_s[:n_fox].reshape(n_fox, nb, seq), (1, 2, 0))
        s_small.append((conv_s, logf_s))

    stack = lambda news, i, shape: jnp.stack([n[i] for n in news]).reshape((depth,) + shape)
    stack_kv = lambda news: [per_head(jnp.stack([n[i] for n in news])) for i in range(4)]
    return (
        y_prompt.reshape(bp, t, d),
        y_sample.reshape(nb, seq, d),
        stack(p_small, 0, (bp, CONV_W - 1, c)),
        *stack_kv(p_kv),
        stack(p_small, 1, (bp, t, n_fox)),
        stack(s_small, 0, (nb, CONV_W - 1, c)),
        *stack_kv(s_kv),
        stack(s_small, 1, (nb, seq, n_fox)),
    )
```

```python
import functools

import jax
import jax.numpy as jnp
from jax import lax
from jax.experimental import pallas as pl
from jax.experimental.pallas import tpu as pltpu

F32 = jnp.float32
BF16 = jnp.bfloat16

NORM_EPS = 1e-5
HEAD_DIM = 128
CONV_W = 3
LANES = 128
VMEM_LIMIT = 56 * 1024 * 1024
NEG_BIG = -1e30
FOX_DEAD = 160.0
LOG2E = 1.4426950408889634
QK_SCALE = HEAD_DIM ** -0.5 * LOG2E


def _params(sem, vmem=VMEM_LIMIT):
    return pltpu.CompilerParams(dimension_semantics=sem, vmem_limit_bytes=vmem)


def _dot(a, b):
    return jnp.dot(a, b, preferred_element_type=F32)


def _dot_nt(a, b):
    return lax.dot_general(a, b, (((1,), (1,)), ((), ())), preferred_element_type=F32)


def _softplus2(z2):
    return jnp.maximum(z2, 0.0) + jnp.log(1.0 + jnp.exp2(-jnp.abs(z2))) * LOG2E


def _log_sigmoid(x):
    return jnp.minimum(x, 0.0) - jnp.log(1.0 + jnp.exp(-jnp.abs(x)))


def _split_bf16(x, parts):
    out = []
    rem = x
    for p in range(parts):
        hi = rem.astype(BF16)
        out.append(hi)
        if p + 1 < parts:
            rem = rem - hi.astype(F32)
    return out


def _iota2(n, m):
    return lax.broadcasted_iota(jnp.int32, (n, m), 0), lax.broadcasted_iota(jnp.int32, (n, m), 1)


def _tri_incl(n, seg=None):
    j, s = _iota2(n, n)
    t = jnp.where(j <= s, 1.0, 0.0)
    if seg is not None:
        t = jnp.where(j // seg == s // seg, t, 0.0)
    return t.astype(BF16)


def _suffix_incl(n):
    j, s = _iota2(n, n)
    return jnp.where(j >= s, 1.0, 0.0).astype(BF16)


def _prefix_sum_lanes(x, tri, carry):
    outs = []
    for c in range(x.shape[1] // LANES):
        xc = x[:, c * LANES:(c + 1) * LANES]
        loc = sum(_dot(p, tri) for p in _split_bf16(xc, 3))
        oc = loc if carry is None else loc + carry
        outs.append(oc)
        if carry is not None:
            carry = oc[:, LANES - 1:LANES]
    return jnp.concatenate(outs, axis=1), carry


def _rmsnorm_f32(x, g):
    ms = jnp.mean(x * x, axis=-1, keepdims=True)
    return x * lax.rsqrt(ms + NORM_EPS) * g


def _rmsnorm_kernel(x_ref, g_ref, o_ref):
    o_ref[...] = _rmsnorm_f32(x_ref[...], g_ref[...]).astype(o_ref.dtype)


def _rmsnorm(x, g, tm):
    m, d = x.shape
    return pl.pallas_call(
        _rmsnorm_kernel,
        grid=(m // tm,),
        in_specs=[pl.BlockSpec((tm, d), lambda i: (i, 0)), pl.BlockSpec((1, d), lambda i: (0, 0))],
        out_specs=pl.BlockSpec((tm, d), lambda i: (i, 0)),
        out_shape=jax.ShapeDtypeStruct((m, d), BF16),
        compiler_params=_params(("arbitrary",)),
        name="rmsnorm",
    )(x, g.reshape(1, d))


def _gated_conv(proj, cw, u_m1, u_m2):
    c = proj.shape[1] // 3
    g_b, g_c, hc = proj[:, :c], proj[:, c:2 * c], proj[:, 2 * c:]
    u = g_c * hc
    r2 = pltpu.roll(u, 2, 0)
    u1 = u_m1(pltpu.roll(u, 1, 0))
    u2 = u_m2(r2)
    conv = cw[0:1, :] * u2 + cw[1:2, :] * u1 + cw[2:3, :] * u
    return g_b * conv, u, r2


def _mix_a_prompt_kernel(h_ref, wa_ref, wtail_ref, bf_ref, cw_ref,
                         ya_ref, conv_ref, logf_ref, ck_ref, uprev_ref, carry_ref, wrows_ref, wf_ref, *, layer):
    i = pl.program_id(0)

    @pl.when(i == 0)
    def _():
        uprev_ref[...] = jnp.zeros_like(uprev_ref)
        carry_ref[...] = jnp.zeros_like(carry_ref)
        wrows_ref[...] = jnp.zeros_like(wrows_ref)
        wrows_ref[0:wtail_ref.shape[0], :] = wtail_ref[:, layer, :]
        wf_ref[...] = wrows_ref[...].T.astype(wf_ref.dtype)

    h = h_ref[...]
    proj = _dot_nt(h, wa_ref[...])
    tm = proj.shape[0]
    rows = lax.broadcasted_iota(jnp.int32, (tm, proj.shape[1] // 3), 0)
    prev = uprev_ref[...]
    m1 = lambda r1: jnp.where(rows == 0, prev[1:2, :], r1)
    m2 = lambda r2: jnp.where(rows == 0, prev[0:1, :], jnp.where(rows == 1, prev[1:2, :], r2))
    ya, _, r2 = _gated_conv(proj, cw_ref[...], m1, m2)
    ya_ref[...] = ya.astype(ya_ref.dtype)
    tail = r2[0:2, :]
    uprev_ref[0:2, :] = tail
    conv_ref[...] = tail

    logf = _log_sigmoid(_dot(h, wf_ref[...]) + bf_ref[...])
    logf_ref[...] = logf
    lower = _suffix_incl(tm)
    ck = sum(_dot(lower, p) for p in _split_bf16(logf, 3)) + carry_ref[...]
    ck_ref[...] = ck
    carry_ref[...] = ck[tm - 1:tm, :]


def _tail_spec(w_ndk, first_row):
    n, depth, d = w_ndk.shape
    assert first_row % (n - first_row) == 0
    return pl.BlockSpec((n - first_row, depth, d), lambda i: (first_row // (n - first_row), 0, 0))


def _mix_a_prompt(h, wt, w_ndk, tail_row, layer, bfp, cw, tm):
    t, d = h.shape
    c = cw.shape[1]
    c3 = 3 * c
    return pl.pallas_call(
        functools.partial(_mix_a_prompt_kernel, layer=layer),
        grid=(t // tm,),
        in_specs=[
            pl.BlockSpec((tm, d), lambda i: (i, 0)),
            pl.BlockSpec((None, c3, d), lambda i: (layer, 0, 0)),
            _tail_spec(w_ndk, tail_row),
            pl.BlockSpec((1, LANES), lambda i: (0, 0)),
            pl.BlockSpec((CONV_W, c), lambda i: (0, 0)),
        ],
        out_specs=[
            pl.BlockSpec((tm, c), lambda i: (i, 0)),
            pl.BlockSpec((CONV_W - 1, c), lambda i: (0, 0)),
            pl.BlockSpec((tm, LANES), lambda i: (i, 0)),
            pl.BlockSpec((tm, LANES), lambda i: (i, 0)),
        ],
        out_shape=[
            jax.ShapeDtypeStruct((t, c), BF16),
            jax.ShapeDtypeStruct((CONV_W - 1, c), F32),
            jax.ShapeDtypeStruct((t, LANES), F32),
            jax.ShapeDtypeStruct((t, LANES), F32),
        ],
        scratch_shapes=[pltpu.VMEM((8, c), F32), pltpu.VMEM((1, LANES), F32),
                        pltpu.VMEM((LANES, d), F32), pltpu.VMEM((d, LANES), BF16)],
        compiler_params=_params(("arbitrary",)),
        name="mix_a_prompt",
    )(h, wt, w_ndk, bfp, cw)


def _mix_a_sample_kernel(h_ref, wa_ref, wtail_ref, bf_ref, cw_ref, st_ref,
                         ya_ref, conv_ref, logf_ref, ckl_ref, u_ref, wrows_ref, *, seq, layer):
    h = h_ref[...]
    proj = _dot_nt(h, wa_ref[...])
    n = proj.shape[0]
    nb = n // seq
    rows = lax.broadcasted_iota(jnp.int32, (n, proj.shape[1] // 3), 0)

    def m1(r1):
        for b in range(nb):
            r1 = jnp.where(rows == b * seq, st_ref[2 * b + 1:2 * b + 2, :], r1)
        return r1

    def m2(r2):
        for b in range(nb):
            r2 = jnp.where(rows == b * seq, st_ref[2 * b:2 * b + 1, :], r2)
            r2 = jnp.where(rows == b * seq + 1, st_ref[2 * b + 1:2 * b + 2, :], r2)
        return r2

    ya, u, _ = _gated_conv(proj, cw_ref[...], m1, m2)
    ya_ref[...] = ya.astype(ya_ref.dtype)
    u_ref[...] = u
    for b in range(nb):
        conv_ref[2 * b:2 * b + 2, :] = u_ref[(b + 1) * seq - 2:(b + 1) * seq, :]

    wrows_ref[...] = jnp.zeros_like(wrows_ref)
    wrows_ref[0:wtail_ref.shape[0], :] = wtail_ref[:, layer, :]
    logit = _dot_nt(wrows_ref[...].astype(BF16), h)[0:8, :]
    logf = _log_sigmoid(logit + bf_ref[...])
    logf_ref[...] = logf
    ckl, _ = _prefix_sum_lanes(logf, _tri_incl(LANES, seg=seq), None)
    ckl_ref[...] = ckl


def _mix_a_sample(h, wt, w_ndk, tail_row, layer, bfp, cw, state, seq):
    n, d = h.shape
    c = cw.shape[1]
    c3 = 3 * c
    nb = n // seq
    full = lambda shape: pl.BlockSpec(shape, lambda i: tuple(0 for _ in shape))
    return pl.pallas_call(
        functools.partial(_mix_a_sample_kernel, seq=seq, layer=layer),
        grid=(1,),
        in_specs=[full((n, d)), pl.BlockSpec((None, c3, d), lambda i: (layer, 0, 0)),
                  _tail_spec(w_ndk, tail_row), full((8, 1)), full((CONV_W, c)),
                  full((nb * (CONV_W - 1), c))],
        out_specs=[full((n, c)), full((nb * (CONV_W - 1), c)), full((8, n)), full((8, n))],
        out_shape=[
            jax.ShapeDtypeStruct((n, c), BF16),
            jax.ShapeDtypeStruct((nb * (CONV_W - 1), c), F32),
            jax.ShapeDtypeStruct((8, n), F32),
            jax.ShapeDtypeStruct((8, n), F32),
        ],
        scratch_shapes=[pltpu.VMEM((n, c), F32), pltpu.VMEM((16, d), F32)],
        compiler_params=_params(("arbitrary",)),
        name="mix_a_sample",
    )(h, wt, w_ndk, bfp, cw, state)


def _prefix_max_lanes(x):
    lane = lax.broadcasted_iota(jnp.int32, (x.shape[0], LANES), 1)
    outs, carry = [], None
    for c in range(x.shape[1] // LANES):
        xc = x[:, c * LANES:(c + 1) * LANES]
        shift = 1
        while shift < LANES:
            xc = jnp.maximum(xc, jnp.where(lane >= shift, pltpu.roll(xc, shift, 1), NEG_BIG))
            shift *= 2
        if carry is not None:
            xc = jnp.maximum(xc, carry)
        outs.append(xc)
        carry = xc[:, LANES - 1:LANES]
    return jnp.concatenate(outs, axis=1)


def _cumsum_rows_kernel(x_ref, ck_ref, nmax_ref):
    ck, _ = _prefix_sum_lanes(x_ref[...], _tri_incl(LANES), jnp.zeros((x_ref.shape[0], 1), F32))
    ck_ref[...] = ck
    nmax_ref[...] = _prefix_max_lanes(-ck)


def _cumsum_rows(x):
    spec = pl.BlockSpec(x.shape, lambda i: (0, 0))
    return pl.pallas_call(
        _cumsum_rows_kernel,
        grid=(1,),
        in_specs=[spec],
        out_specs=[spec, spec],
        out_shape=[jax.ShapeDtypeStruct(x.shape, F32)] * 2,
        compiler_params=_params(("arbitrary",)),
        name="cumsum_rows",
    )(x)


def _wprep_kernel(w_ref, o_ref):
    for layer in range(o_ref.shape[0]):
        o_ref[layer] = w_ref[:, layer, :].astype(o_ref.dtype)


def _wprep(w_ndk, n_rows, tn):
    _, depth, d = w_ndk.shape
    return pl.pallas_call(
        _wprep_kernel,
        grid=(n_rows // tn,),
        in_specs=[pl.BlockSpec((tn, depth, d), lambda i: (i, 0, 0))],
        out_specs=pl.BlockSpec((depth, tn, d), lambda i: (0, i, 0)),
        out_shape=jax.ShapeDtypeStruct((depth, n_rows, d), BF16),
        compiler_params=_params(("arbitrary",)),
        name="wprep",
    )(w_ndk)


def _store_heads(ref, acc):
    n_heads = ref.shape[-3]
    for hd in range(n_heads):
        part = acc[:, hd * HEAD_DIM:(hd + 1) * HEAD_DIM]
        if len(ref.shape) == 3:
            ref[hd] = part
        else:
            seq = ref.shape[2]
            for b in range(ref.shape[0]):
                ref[b, hd] = part[b * seq:(b + 1) * seq, :]


def _qkv_kernel(h_ref, w_ref, *refs, n_earlier):
    earlier = refs[:4 * n_earlier]
    qkv_ref = refs[4 * n_earlier]
    kv_refs = refs[4 * n_earlier + 1:]
    j = pl.program_id(1)
    acc = _dot_nt(h_ref[...], w_ref[...])
    @pl.when(jnp.logical_or(j == 0, j == 3))
    def _():
        qkv_ref[...] = (acc * QK_SCALE).astype(qkv_ref.dtype)

    for kind, idx in enumerate((1, 2, 4, 5)):
        @pl.when(j == idx)
        def _(kind=kind):
            qkv_ref[...] = acc.astype(qkv_ref.dtype)
            ref = kv_refs[kind]
            if n_earlier:
                for p in range(n_earlier):
                    ref[p] = earlier[4 * p + kind][...]
                ref = ref.at[n_earlier]
            _store_heads(ref, acc)


def _qkv_proj(h, wt, first_row, tm, layer, batch, n_heads, earlier=()):
    m, d = h.shape
    seq = m // batch
    tn = n_heads * HEAD_DIM
    assert first_row % tn == 0
    blk0 = first_row // tn
    if m // tm == 1:
        block, index = (batch, n_heads, seq, HEAD_DIM), lambda i, j: (0, 0, 0, 0)
    else:
        assert batch == 1
        block, index = (None, n_heads, tm, HEAD_DIM), lambda i, j: (0, 0, i, 0)
    kv_in_spec = pl.BlockSpec(block, index)
    kv_shape = (batch, n_heads, seq, HEAD_DIM)
    n_earlier = len(earlier)
    if n_earlier:
        kv_out_spec = pl.BlockSpec((n_earlier + 1,) + block, lambda i, j: (0,) + index(i, j))
        kv_shape = (n_earlier + 1,) + kv_shape
    else:
        kv_out_spec = kv_in_spec
    return pl.pallas_call(
        functools.partial(_qkv_kernel, n_earlier=n_earlier),
        grid=(m // tm, 6),
        in_specs=[pl.BlockSpec((tm, d), lambda i, j: (i, 0)),
                  pl.BlockSpec((None, tn, d), lambda i, j: (layer, blk0 + j, 0))]
        + [kv_in_spec] * (4 * n_earlier),
        out_specs=[pl.BlockSpec((tm, tn), lambda i, j: (i, j))] + [kv_out_spec] * 4,
        out_shape=[jax.ShapeDtypeStruct((m, 6 * tn), BF16)] + [jax.ShapeDtypeStruct(kv_shape, F32)] * 4,
        compiler_params=_params(("arbitrary", "arbitrary")),
        name="qkv_proj",
    )(h, wt, *(a for kv in earlier for a in kv))


def _sb_tile(q, kb, vb, suffix, mask):
    z = _dot_nt(q, kb)
    sp = _softplus2(z)
    if mask is not None:
        sp = jnp.where(mask, sp, 0.0)
    s_incl = _dot(sp.astype(BF16), suffix)
    w = jnp.exp2(z - s_incl)
    if mask is not None:
        w = jnp.where(mask, w, 0.0)
    return _dot(w.astype(BF16), vb), s_incl[:, 0:1]


def _sb_live(r_ref):
    r_min = jnp.min(r_ref[...], axis=0, keepdims=True)
    return (jnp.max(jnp.exp2(-r_min)) > 0.0).astype(jnp.int32)


def _sb_accumulate(acc_ref, r_ref, pv, tot):
    r = r_ref[...]
    acc_ref[...] += jnp.exp2(-r) * pv
    r_ref[...] = r + tot


def _for_past_tiles(n_tiles, pair_fn, single_fn, live_fn=None):
    if live_fn is None:
        def body(jj, c):
            pair_fn(n_tiles - 1 - 2 * jj)
            return c

        lax.fori_loop(0, n_tiles // 2, body, 0)
        live = True
    else:
        def step(c):
            pair_fn(n_tiles - 1 - 2 * c[0])
            return c[0] + 1, live_fn()

        _, live = lax.while_loop(lambda c: jnp.logical_and(c[0] < n_tiles // 2, c[1] > 0), step,
                                 (jnp.int32(0), live_fn()))
        live = live > 0
    odd = n_tiles % 2 == 1
    if isinstance(odd, bool) and live is True:
        if odd:
            single_fn(0)
    else:
        pl.when(jnp.logical_and(odd, live))(lambda: single_fn(0))


def _sb_prompt_kernel(q_ref, k_ref, v_ref, o_ref, acc_ref, r_ref, *, tq):
    qi = pl.program_id(1)
    acc_ref[...] = jnp.zeros_like(acc_ref)
    r_ref[...] = jnp.zeros_like(r_ref)
    suffix = _suffix_incl(tq)
    row, col = _iota2(tq, tq)

    def kv(j):
        st = pl.multiple_of(j * tq, tq)
        return k_ref[pl.ds(st, tq), :], v_ref[pl.ds(st, tq), :]

    def block(s):
        q = q_ref[s * tq:(s + 1) * tq, :]
        acc, r = acc_ref.at[s], r_ref.at[s]

        def single(j, mask=None):
            _sb_accumulate(acc, r, *_sb_tile(q, *kv(j), suffix, mask))

        def pair(j, mask=None):
            pv_a, tot_a = _sb_tile(q, *kv(j), suffix, mask)
            pv_b, tot_b = _sb_tile(q, *kv(j - 1), suffix, None)
            _sb_accumulate(acc, r, pv_a + jnp.exp2(-tot_a) * pv_b, tot_a + tot_b)

        return single, pair, functools.partial(_sb_live, r)

    single0, pair0, live0 = block(0)
    single1, pair1, live1 = block(1)
    g0, g1 = 2 * qi, 2 * qi + 1

    @pl.when(qi == 0)
    def _():
        single0(g0, col < row)
        pair1(g1, col < row)

    @pl.when(qi > 0)
    def _():
        pair0(g0, col < row)
        pair1(g1, col < row)
        _for_past_tiles(g0 - 1, pair0, single0, live0)
        _for_past_tiles(g1 - 1, pair1, single1, live1)

    for s in range(2):
        o_ref[s * tq:(s + 1) * tq, :] = acc_ref[s].astype(o_ref.dtype)


def _sb_prompt(qkv, n_heads, tq):
    t = qkv.shape[0]
    return pl.pallas_call(
        functools.partial(_sb_prompt_kernel, tq=tq),
        grid=(n_heads, t // (2 * tq)),
        in_specs=[
            pl.BlockSpec((2 * tq, HEAD_DIM), lambda h, i: (i, h)),
            pl.BlockSpec((t, HEAD_DIM), lambda h, i: (0, n_heads + h)),
            pl.BlockSpec((t, HEAD_DIM), lambda h, i: (0, 2 * n_heads + h)),
        ],
        out_specs=pl.BlockSpec((2 * tq, HEAD_DIM), lambda h, i: (i, h)),
        out_shape=jax.ShapeDtypeStruct((t, n_heads * HEAD_DIM), BF16),
        scratch_shapes=[pltpu.VMEM((2, tq, HEAD_DIM), F32), pltpu.VMEM((2, tq, HEAD_DIM), F32)],
        compiler_params=_params(("arbitrary", "arbitrary")),
        name="sb_prompt",
    )(qkv, qkv, qkv)


def _sb_sample_kernel(q_ref, kn_ref, vn_ref, kc_ref, vc_ref, o_ref, acc_ref, r_ref, *, tk):
    q = q_ref[...]
    seq = q.shape[0]
    past = kc_ref.shape[0]
    acc_ref[...] = jnp.zeros_like(acc_ref)
    r_ref[...] = jnp.zeros_like(r_ref)
    row, col = _iota2(seq, seq)
    _sb_accumulate(acc_ref, r_ref, *_sb_tile(q, kn_ref[...], vn_ref[...], _suffix_incl(seq), col < row))
    suffix = _suffix_incl(tk)

    def kv(j):
        st = pl.multiple_of(j * tk, tk)
        return kc_ref[pl.ds(st, tk), :].astype(BF16), vc_ref[pl.ds(st, tk), :].astype(BF16)

    def single(j):
        _sb_accumulate(acc_ref, r_ref, *_sb_tile(q, *kv(j), suffix, None))

    def pair(j):
        pv_a, tot_a = _sb_tile(q, *kv(j), suffix, None)
        pv_b, tot_b = _sb_tile(q, *kv(j - 1), suffix, None)
        _sb_accumulate(acc_ref, r_ref, pv_a + jnp.exp2(-tot_a) * pv_b, tot_a + tot_b)

    _for_past_tiles(past // tk, pair, single, functools.partial(_sb_live, r_ref))
    o_ref[...] = acc_ref[...].astype(o_ref.dtype)


def _sample_specs(layer, n_heads, seq, past, first_col):
    qkv = [pl.BlockSpec((seq, HEAD_DIM), lambda b, h, k=k: (b, (first_col + k) * n_heads + h)) for k in range(3)]
    cache = pl.BlockSpec((None, None, None, past, HEAD_DIM), lambda b, h: (layer, b, h, 0, 0))
    return qkv + [cache, cache]


def _sb_sample(qkv, cache_k, cache_v, layer, seq, tk):
    n = qkv.shape[0]
    nb = n // seq
    _, _, n_heads, past, _ = cache_k.shape
    return pl.pallas_call(
        functools.partial(_sb_sample_kernel, tk=tk),
        grid=(nb, n_heads),
        in_specs=_sample_specs(layer, n_heads, seq, past, 0),
        out_specs=pl.BlockSpec((seq, HEAD_DIM), lambda b, h: (b, h)),
        out_shape=jax.ShapeDtypeStruct((n, n_heads * HEAD_DIM), BF16),
        scratch_shapes=[pltpu.VMEM((seq, HEAD_DIM), F32), pltpu.VMEM((seq, HEAD_DIM), F32)],
        compiler_params=_params(("arbitrary", "arbitrary")),
        name="sb_sample",
    )(qkv, qkv, qkv, cache_k, cache_v)


def _fox_init(m_ref, l_ref, acc_ref):
    m_ref[...] = jnp.full(m_ref.shape, NEG_BIG, F32)
    l_ref[...] = jnp.zeros_like(l_ref)
    acc_ref[...] = jnp.zeros_like(acc_ref)


def _fox_tile_t(qt, kb, vt, bias, mask, m_ref, l_ref, acc_ref):
    s = _dot(kb, qt) + jnp.concatenate([bias] * (qt.shape[1] // LANES), axis=1)
    if mask is not None:
        s = jnp.where(mask, s, NEG_BIG)
    m_prev = m_ref[...]
    m_new = jnp.maximum(m_prev, jnp.max(s, axis=0, keepdims=True))
    alpha = jnp.exp2(m_prev - m_new)
    p = jnp.exp2(s - m_new)
    l_ref[...] = alpha * l_ref[...] + jnp.sum(p, axis=0, keepdims=True)
    acc_ref[...] = alpha * acc_ref[...] + _dot(vt, p.astype(BF16))
    m_ref[...] = m_new


def _fox_prompt_kernel(q_ref, k_ref, v_ref, ck_ref, o_ref,
                       vt_ref, nb_ref, kmax_ref, edge_ref, m_ref, l_ref, acc_ref, *, tq, chunk):
    h = pl.program_id(0)
    qi = pl.program_id(1)
    t = k_ref.shape[0]
    reps = tq // LANES

    @pl.when(qi == 0)
    def _():
        sel_r, _ = _iota2(LANES, LANES)
        sel = jnp.where(sel_r == h, 1.0, 0.0).astype(BF16)
        kmax_ref[...] = jnp.zeros_like(kmax_ref)

        def body(c, carry):
            st = pl.multiple_of(c * chunk, chunk)
            vt_ref[:, pl.ds(st, chunk)] = v_ref[pl.ds(st, chunk), :].astype(F32).T.astype(BF16)
            ck_h = sum(_dot(p, sel) for p in _split_bf16(ck_ref[pl.ds(st, chunk), :], 3))
            nb_ref[pl.ds(st, chunk), :] = ck_h * (-LOG2E)
            kf = k_ref[pl.ds(st, chunk), :].astype(F32)
            kmax_ref[...] = jnp.maximum(kmax_ref[...], jnp.max(jnp.sum(kf * kf, axis=1, keepdims=True)))
            return carry

        lax.fori_loop(0, t // chunk, body, 0)

    def block(s):
        g = 2 * qi + s
        qtf = q_ref[s * tq:(s + 1) * tq, :].astype(F32).T
        qt = qtf.astype(BF16)
        m, l, acc, edge = m_ref.at[s], l_ref.at[s], acc_ref.at[s], edge_ref.at[s]
        _fox_init(m, l, acc)
        qs = pl.multiple_of(g * tq, tq)
        c0 = nb_ref[pl.ds(qs, 1), :]
        edge[...] = c0

        def tile(st, tk, mask):
            bias = nb_ref[pl.ds(st, tk), :] - c0
            _fox_tile_t(qt, k_ref[pl.ds(st, tk), :], vt_ref[:, pl.ds(st, tk)], bias, mask, m, l, acc)
            edge[...] = nb_ref[pl.ds(st, 1), :]

        q_norm2 = jnp.sum(qtf * qtf, axis=0, keepdims=True)
        reach = jnp.sqrt(q_norm2 * jnp.concatenate([kmax_ref[...]] * reps, axis=1)) * 1.001

        def live():
            left = jnp.concatenate([edge[...] - c0] * reps, axis=1)
            return (jnp.max(reach + left - m[...]) > -FOX_DEAD).astype(jnp.int32)

        def first(with_left):
            if with_left:
                krow, qcol = _iota2(2 * tq, tq)
                tile(pl.multiple_of(qs - tq, tq), 2 * tq, krow - tq <= qcol)
            else:
                krow, qcol = _iota2(tq, tq)
                tile(qs, tq, krow <= qcol)

        def rest():
            _for_past_tiles(g - 1,
                            lambda j: tile(pl.multiple_of((j - 1) * tq, tq), 2 * tq, None),
                            lambda j: tile(pl.multiple_of(j * tq, tq), tq, None),
                            live)

        return first, rest

    first0, rest0 = block(0)
    first1, rest1 = block(1)

    @pl.when(qi == 0)
    def _():
        first0(False)
        first1(True)

    @pl.when(qi > 0)
    def _():
        first0(True)
        first1(True)
        rest0()
        rest1()

    for s in range(2):
        o_ref[s * tq:(s + 1) * tq, :] = (acc_ref[s] / l_ref[s]).T.astype(o_ref.dtype)


def _fox_prompt(qkv, ck, n_heads, tq):
    t = qkv.shape[0]
    return pl.pallas_call(
        functools.partial(_fox_prompt_kernel, tq=tq, chunk=512),
        grid=(n_heads, t // (2 * tq)),
        in_specs=[
            pl.BlockSpec((2 * tq, HEAD_DIM), lambda h, i: (i, 3 * n_heads + h)),
            pl.BlockSpec((t, HEAD_DIM), lambda h, i: (0, 4 * n_heads + h)),
            pl.BlockSpec((t, HEAD_DIM), lambda h, i: (0, 5 * n_heads + h)),
            pl.BlockSpec((t, LANES), lambda h, i: (0, 0)),
        ],
        out_specs=pl.BlockSpec((2 * tq, HEAD_DIM), lambda h, i: (i, h)),
        out_shape=jax.ShapeDtypeStruct((t, n_heads * HEAD_DIM), BF16),
        scratch_shapes=[pltpu.VMEM((HEAD_DIM, t), BF16), pltpu.VMEM((t, LANES), F32),
                        pltpu.VMEM((1, LANES), F32), pltpu.VMEM((2, 1, LANES), F32),
                        pltpu.VMEM((2, 1, tq), F32), pltpu.VMEM((2, 1, tq), F32),
                        pltpu.VMEM((2, HEAD_DIM, tq), F32)],
        compiler_params=_params(("arbitrary", "arbitrary")),
        name="fox_prompt",
    )(qkv, qkv, qkv, ck)


def _fox_tile(q, kb, vb, bias, mask, m_ref, l_ref, acc_ref):
    s = _dot_nt(q, kb) + bias
    if mask is not None:
        s = jnp.where(mask, s, NEG_BIG)
    m_prev = m_ref[...]
    m_new = jnp.maximum(m_prev, jnp.max(s, axis=1, keepdims=True))
    alpha = jnp.exp2(m_prev - m_new)
    p = jnp.exp2(s - m_new)
    l_ref[...] = alpha * l_ref[...] + jnp.sum(p, axis=1, keepdims=True)
    acc_ref[...] = alpha * acc_ref[...] + _dot(p.astype(BF16), vb)
    m_ref[...] = m_new


def _fox_sample_kernel(q_ref, kn_ref, vn_ref, kc_ref, vc_ref, ckp_ref, nmax_ref, ckl_ref, o_ref,
                       m_ref, l_ref, acc_ref, *, tk):
    q = q_ref[...]
    seq = q.shape[0]
    past = kc_ref.shape[0]
    n_tiles = past // tk
    _fox_init(m_ref, l_ref, acc_ref)
    row, col = _iota2(seq, seq)
    _fox_tile(q, kn_ref[...], vn_ref[...], ckl_ref[...] * (-LOG2E), col <= row, m_ref, l_ref, acc_ref)
    c0 = ckp_ref[:, past - 1:past]

    qf = q.astype(F32)
    k_abs = jnp.max(jnp.abs(kc_ref[...]))
    reach = jnp.sqrt(jnp.sum(qf * qf, axis=1, keepdims=True) * HEAD_DIM) * (k_abs * 1.01)

    def step(c):
        st = pl.multiple_of((n_tiles - 1 - c[0]) * tk, tk)
        kb = kc_ref[pl.ds(st, tk), :].astype(BF16)
        vb = vc_ref[pl.ds(st, tk), :].astype(BF16)
        bias = (c0 - ckp_ref[:, pl.ds(st, tk)]) * LOG2E
        _fox_tile(q, kb, vb, bias, None, m_ref, l_ref, acc_ref)
        lo = pl.multiple_of(jnp.maximum(st - LANES, 0), LANES)
        left = nmax_ref[:, pl.ds(lo, LANES)][:, LANES - 1:LANES]
        live = jnp.max(reach + (c0 + left) * LOG2E - m_ref[...]) > -FOX_DEAD
        return c[0] + 1, live.astype(jnp.int32)

    lax.while_loop(lambda c: jnp.logical_and(c[0] < n_tiles, c[1] > 0), step, (jnp.int32(0), jnp.int32(1)))
    o_ref[...] = (acc_ref[...] / l_ref[...]).astype(o_ref.dtype)


def _fox_sample(qkv, cache_k, cache_v, ck_past, nmax_past, ck_local, layer, seq, tk):
    n = qkv.shape[0]
    nb = n // seq
    _, _, n_heads, past, _ = cache_k.shape
    return pl.pallas_call(
        functools.partial(_fox_sample_kernel, tk=tk),
        grid=(nb, n_heads),
        in_specs=_sample_specs(layer, n_heads, seq, past, 3) + [
            pl.BlockSpec((None, None, 1, past), lambda b, h: (h, b, 0, 0)),
            pl.BlockSpec((None, None, 1, past), lambda b, h: (h, b, 0, 0)),
            pl.BlockSpec((None, None, 1, seq), lambda b, h: (h, b, 0, 0)),
        ],
        out_specs=pl.BlockSpec((seq, HEAD_DIM), lambda b, h: (b, h)),
        out_shape=jax.ShapeDtypeStruct((n, n_heads * HEAD_DIM), BF16),
        scratch_shapes=[pltpu.VMEM((seq, 1), F32), pltpu.VMEM((seq, 1), F32), pltpu.VMEM((seq, HEAD_DIM), F32)],
        compiler_params=_params(("arbitrary", "arbitrary")),
        name="fox_sample",
    )(qkv, qkv, qkv, cache_k, cache_v, ck_past, nmax_past, ck_local)


def _out_proj_kernel(ya_ref, ysb_ref, yfox_ref, w_ref, x_ref, g_ref, xo_ref, h_ref):
    c = ya_ref.shape[1]
    s = ysb_ref.shape[1]
    acc = _dot(ya_ref[...], w_ref[0:c, :])
    acc += _dot(ysb_ref[...], w_ref[c:c + s, :])
    acc += _dot(yfox_ref[...], w_ref[c + s:, :])
    x = x_ref[...] + acc
    xo_ref[...] = x
    h_ref[...] = _rmsnorm_f32(x, g_ref[...]).astype(h_ref.dtype)


def _out_proj(ya, ysb, yfox, w, layer, x, g, tm):
    m, d = x.shape
    row = lambda width: pl.BlockSpec((tm, width), lambda i: (i, 0))
    return pl.pallas_call(
        _out_proj_kernel,
        grid=(m // tm,),
        in_specs=[row(ya.shape[1]), row(ysb.shape[1]), row(yfox.shape[1]),
                  pl.BlockSpec((None,) + w.shape[1:], lambda i: (layer, 0, 0)), row(d),
                  pl.BlockSpec((1, d), lambda i: (0, 0))],
        out_specs=[row(d), row(d)],
        out_shape=[jax.ShapeDtypeStruct((m, d), F32), jax.ShapeDtypeStruct((m, d), BF16)],
        compiler_params=_params(("arbitrary",)),
        name="out_proj",
    )(ya, ysb, yfox, w, x, g.reshape(1, d))


def _ffn_kernel(h_ref, wg_ref, wu_ref, wd_ref, x_ref, g_ref, *rest, emit_x):
    out_refs, acc_ref = rest[:-1], rest[-1]
    f = pl.program_id(1)

    @pl.when(f == 0)
    def _():
        acc_ref[...] = jnp.zeros_like(acc_ref)

    h = h_ref[...]
    gate = _dot(h, wg_ref[...])
    up = _dot(h, wu_ref[...])
    act = gate * jax.nn.sigmoid(gate) * up
    acc_ref[...] += _dot(act.astype(BF16), wd_ref[...])

    @pl.when(f == pl.num_programs(1) - 1)
    def _():
        x = x_ref[...] + acc_ref[...]
        normed = _rmsnorm_f32(x, g_ref[...])
        if emit_x:
            out_refs[0][...] = x
            out_refs[1][...] = normed.astype(out_refs[1].dtype)
        else:
            out_refs[0][...] = normed


def _ffn(h, wg, wu, wd, layer, x, g, tm, tf, emit_x):
    m, d = x.shape
    ff = wg.shape[2]
    row = pl.BlockSpec((tm, d), lambda i, f: (i, 0))
    if emit_x:
        out_specs = [row, row]
        out_shape = [jax.ShapeDtypeStruct((m, d), F32), jax.ShapeDtypeStruct((m, d), BF16)]
    else:
        out_specs = [row]
        out_shape = [jax.ShapeDtypeStruct((m, d), F32)]
    return pl.pallas_call(
        functools.partial(_ffn_kernel, emit_x=emit_x),
        grid=(m // tm, ff // tf),
        in_specs=[row,
                  pl.BlockSpec((None, d, tf), lambda i, f: (layer, 0, f)),
                  pl.BlockSpec((None, d, tf), lambda i, f: (layer, 0, f)),
                  pl.BlockSpec((None, tf, d), lambda i, f: (layer, f, 0)),
                  row,
                  pl.BlockSpec((1, d), lambda i, f: (0, 0))],
        out_specs=out_specs,
        out_shape=out_shape,
        scratch_shapes=[pltpu.VMEM((tm, d), F32)],
        compiler_params=_params(("arbitrary", "arbitrary")),
        name="ffn",
    )(h, wg, wu, wd, x, g.reshape(1, d))


def kernel(x_prompt, x_sample, state_conv, cache_sb_k, cache_sb_v, cache_fox_k, cache_fox_v, cache_fox_logf,
           norm1_g, w_in, b_f, conv_w, w_out, norm2_g, w_gate, w_up, w_down, final_g):
    depth = w_in.shape[0]
    bp, t, d = x_prompt.shape
    nb, seq, _ = x_sample.shape
    assert bp == 1, "prompt kernels carry conv rows and forget sums across row tiles of one stream"
    c = conv_w.shape[2]
    n_sb = cache_sb_k.shape[3]
    n_fox = cache_fox_k.shape[3]
    assert n_sb == n_fox and n_fox <= 8
    past = cache_sb_k.shape[2]
    sb_dim = n_sb * HEAD_DIM
    qkv0 = 3 * c
    qkv1 = qkv0 + 3 * sb_dim + 3 * n_fox * HEAD_DIM
    n_s = nb * seq
    per_head = lambda a: jnp.transpose(a, (0, 1, 3, 2, 4))
    csk, csv, cfk, cfv = (per_head(a) for a in (cache_sb_k, cache_sb_v, cache_fox_k, cache_fox_v))
    w_ndk = jnp.transpose(w_in, (2, 0, 1))
    wt = _wprep(w_ndk, qkv1, 512)
    wo, wg, wu, wd = (w.astype(BF16) for w in (w_out, w_gate, w_up, w_down))

    xp = x_prompt.reshape(t, d)
    xs = x_sample.reshape(n_s, d)
    hp = _rmsnorm(xp, norm1_g[0], 512)
    hs = _rmsnorm(xs, norm1_g[0], n_s)
    p_small, s_small = [], []
    p_kv, s_kv = [], []
    for l in range(depth):
        bf_row = jnp.pad(b_f[l], (0, LANES - n_fox)).reshape(1, LANES)
        bf_col = jnp.pad(b_f[l], (0, 8 - n_fox)).reshape(8, 1)
        last = l == depth - 1
        g_next = final_g if last else norm1_g[l + 1]

        ya, conv_p, logf_p, ck_p = _mix_a_prompt(hp, wt, w_ndk, qkv1, l, bf_row, conv_w[l], 512)
        stacking = last and l > 0
        qkv, *kv = _qkv_proj(hp, wt, qkv0, 512 if stacking else 1024, l, bp, n_sb, p_kv if stacking else ())
        p_kv.append(kv)
        ysb = _sb_prompt(qkv, n_sb, 256)
        yfox = _fox_prompt(qkv, ck_p, n_fox, 256)
        xp, h2 = _out_proj(ya, ysb, yfox, wo, l, xp, norm2_g[l], 512)
        outs = _ffn(h2, wg, wu, wd, l, xp, g_next, 512, 512, not last)
        if last:
            y_prompt = outs[0]
        else:
            xp, hp = outs
        p_small.append((conv_p, logf_p[:, :n_fox]))

        logf_past = jnp.transpose(cache_fox_logf[l], (2, 0, 1)).reshape(n_fox * nb, past)
        ck_past, nmax_past = (a.reshape(n_fox, nb, 1, past) for a in _cumsum_rows(logf_past))
        ya, conv_s, logf_s, ckl = _mix_a_sample(hs, wt, w_ndk, qkv1, l, bf_col, conv_w[l],
                                                state_conv[l].reshape(nb * (CONV_W - 1), c), seq)
        ckl = ckl[:n_fox].reshape(n_fox, nb, 1, seq)
        qkv, *kv = _qkv_proj(hs, wt, qkv0, n_s, l, nb, n_sb, s_kv if stacking else ())
        s_kv.append(kv)
        ysb = _sb_sample(qkv, csk, csv, l, seq, 256)
        yfox = _fox_sample(qkv, cfk, cfv, ck_past, nmax_past, ckl, l, seq, 512)
        xs, h2 = _out_proj(ya, ysb, yfox, wo, l, xs, norm2_g[l], n_s)
        outs = _ffn(h2, wg, wu, wd, l, xs, g_next, n_s, 512, not last)
        if last:
            y_sample = outs[0]
        else:
            xs, hs = outs
        logf_s = jnp.transpose(logf_s[:n_fox].reshape(n_fox, nb, seq), (1, 2, 0))
        s_small.append((conv_s, logf_s))

    stack = lambda news, i, shape: jnp.stack([n[i] for n in news]).reshape((depth,) + shape)
    stack_kv = lambda news: [per_head(a if depth > 1 else a[None]) for a in news[-1]]
    return (
        y_prompt.reshape(bp, t, d),
        y_sample.reshape(nb, seq, d),
        stack(p_small, 0, (bp, CONV_W - 1, c)),
        *stack_kv(p_kv),
        stack(p_small, 1, (bp, t, n_fox)),
        stack(s_small, 0, (nb, CONV_W - 1, c)),
        *stack_kv(s_kv),
        stack(s_small, 1, (nb, seq, n_fox)),
    )
```

```python
import functools

import jax
import jax.numpy as jnp
from jax import lax
from jax.experimental import pallas as pl
from jax.experimental.pallas import tpu as pltpu

F32 = jnp.float32
BF16 = jnp.bfloat16

NORM_EPS = 1e-5
HEAD_DIM = 128
CONV_W = 3
LANES = 128
VMEM_LIMIT = 56 * 1024 * 1024
NEG_BIG = -1e30
FOX_DEAD = 160.0
LOG2E = 1.4426950408889634
QK_SCALE = HEAD_DIM ** -0.5 * LOG2E


def _params(sem, vmem=VMEM_LIMIT):
    return pltpu.CompilerParams(dimension_semantics=sem, vmem_limit_bytes=vmem)


def _dot(a, b):
    return jnp.dot(a, b, preferred_element_type=F32)


def _dot_nt(a, b):
    return lax.dot_general(a, b, (((1,), (1,)), ((), ())), preferred_element_type=F32)


def _softplus2(z2):
    return jnp.maximum(z2, 0.0) + jnp.log(1.0 + jnp.exp2(-jnp.abs(z2))) * LOG2E


def _log_sigmoid(x):
    return jnp.minimum(x, 0.0) - jnp.log(1.0 + jnp.exp(-jnp.abs(x)))


def _split_bf16(x, parts):
    out = []
    rem = x
    for p in range(parts):
        hi = rem.astype(BF16)
        out.append(hi)
        if p + 1 < parts:
            rem = rem - hi.astype(F32)
    return out


def _iota2(n, m):
    return lax.broadcasted_iota(jnp.int32, (n, m), 0), lax.broadcasted_iota(jnp.int32, (n, m), 1)


def _tri_incl(n, seg=None):
    j, s = _iota2(n, n)
    t = jnp.where(j <= s, 1.0, 0.0)
    if seg is not None:
        t = jnp.where(j // seg == s // seg, t, 0.0)
    return t.astype(BF16)


def _suffix_incl(n):
    j, s = _iota2(n, n)
    return jnp.where(j >= s, 1.0, 0.0).astype(BF16)


def _prefix_sum_lanes(x, tri, carry):
    outs = []
    for c in range(x.shape[1] // LANES):
        xc = x[:, c * LANES:(c + 1) * LANES]
        loc = sum(_dot(p, tri) for p in _split_bf16(xc, 3))
        oc = loc if carry is None else loc + carry
        outs.append(oc)
        if carry is not None:
            carry = oc[:, LANES - 1:LANES]
    return jnp.concatenate(outs, axis=1), carry


def _rmsnorm_f32(x, g):
    ms = jnp.mean(x * x, axis=-1, keepdims=True)
    return x * lax.rsqrt(ms + NORM_EPS) * g


def _rmsnorm_kernel(x_ref, g_ref, o_ref):
    o_ref[...] = _rmsnorm_f32(x_ref[...], g_ref[...]).astype(o_ref.dtype)


def _rmsnorm(x, g, tm):
    m, d = x.shape
    return pl.pallas_call(
        _rmsnorm_kernel,
        grid=(m // tm,),
        in_specs=[pl.BlockSpec((tm, d), lambda i: (i, 0)), pl.BlockSpec((1, d), lambda i: (0, 0))],
        out_specs=pl.BlockSpec((tm, d), lambda i: (i, 0)),
        out_shape=jax.ShapeDtypeStruct((m, d), BF16),
        compiler_params=_params(("arbitrary",)),
        name="rmsnorm",
    )(x, g.reshape(1, d))


def _gated_conv(proj, cw, u_m1, u_m2):
    c = proj.shape[1] // 3
    g_b, g_c, hc = proj[:, :c], proj[:, c:2 * c], proj[:, 2 * c:]
    u = g_c * hc
    r2 = pltpu.roll(u, 2, 0)
    u1 = u_m1(pltpu.roll(u, 1, 0))
    u2 = u_m2(r2)
    conv = cw[0:1, :] * u2 + cw[1:2, :] * u1 + cw[2:3, :] * u
    return g_b * conv, u, r2


def _mix_a_prompt_kernel(h_ref, wa_ref, wtail_ref, bf_ref, cw_ref,
                         ya_ref, conv_ref, logf_ref, ck_ref, uprev_ref, carry_ref, wrows_ref, wf_ref, *, layer):
    i = pl.program_id(0)

    @pl.when(i == 0)
    def _():
        uprev_ref[...] = jnp.zeros_like(uprev_ref)
        carry_ref[...] = jnp.zeros_like(carry_ref)
        wrows_ref[...] = jnp.zeros_like(wrows_ref)
        wrows_ref[0:wtail_ref.shape[0], :] = wtail_ref[:, layer, :]
        wf_ref[...] = wrows_ref[...].T.astype(wf_ref.dtype)

    h = h_ref[...]
    proj = _dot_nt(h, wa_ref[...])
    tm = proj.shape[0]
    rows = lax.broadcasted_iota(jnp.int32, (tm, proj.shape[1] // 3), 0)
    prev = uprev_ref[...]
    m1 = lambda r1: jnp.where(rows == 0, prev[1:2, :], r1)
    m2 = lambda r2: jnp.where(rows == 0, prev[0:1, :], jnp.where(rows == 1, prev[1:2, :], r2))
    ya, _, r2 = _gated_conv(proj, cw_ref[...], m1, m2)
    ya_ref[...] = ya.astype(ya_ref.dtype)
    tail = r2[0:2, :]
    uprev_ref[0:2, :] = tail
    conv_ref[...] = tail

    logf = _log_sigmoid(_dot(h, wf_ref[...]) + bf_ref[...])
    logf_ref[...] = logf
    lower = _suffix_incl(tm)
    ck = sum(_dot(lower, p) for p in _split_bf16(logf, 3)) + carry_ref[...]
    ck_ref[...] = ck
    carry_ref[...] = ck[tm - 1:tm, :]


def _tail_spec(w_ndk, first_row):
    n, depth, d = w_ndk.shape
    assert first_row % (n - first_row) == 0
    return pl.BlockSpec((n - first_row, depth, d), lambda i: (first_row // (n - first_row), 0, 0))


def _mix_a_prompt(h, wt, w_ndk, tail_row, layer, bfp, cw, tm):
    t, d = h.shape
    c = cw.shape[1]
    c3 = 3 * c
    return pl.pallas_call(
        functools.partial(_mix_a_prompt_kernel, layer=layer),
        grid=(t // tm,),
        in_specs=[
            pl.BlockSpec((tm, d), lambda i: (i, 0)),
            pl.BlockSpec((None, c3, d), lambda i: (layer, 0, 0)),
            _tail_spec(w_ndk, tail_row),
            pl.BlockSpec((1, LANES), lambda i: (0, 0)),
            pl.BlockSpec((CONV_W, c), lambda i: (0, 0)),
        ],
        out_specs=[
            pl.BlockSpec((tm, c), lambda i: (i, 0)),
            pl.BlockSpec((CONV_W - 1, c), lambda i: (0, 0)),
            pl.BlockSpec((tm, LANES), lambda i: (i, 0)),
            pl.BlockSpec((tm, LANES), lambda i: (i, 0)),
        ],
        out_shape=[
            jax.ShapeDtypeStruct((t, c), BF16),
            jax.ShapeDtypeStruct((CONV_W - 1, c), F32),
            jax.ShapeDtypeStruct((t, LANES), F32),
            jax.ShapeDtypeStruct((t, LANES), F32),
        ],
        scratch_shapes=[pltpu.VMEM((8, c), F32), pltpu.VMEM((1, LANES), F32),
                        pltpu.VMEM((LANES, d), F32), pltpu.VMEM((d, LANES), BF16)],
        compiler_params=_params(("arbitrary",)),
        name="mix_a_prompt",
    )(h, wt, w_ndk, bfp, cw)


def _mix_a_sample_kernel(h_ref, wa_ref, wtail_ref, bf_ref, cw_ref, st_ref,
                         ya_ref, conv_ref, logf_ref, ckl_ref, u_ref, wrows_ref, *, seq, layer):
    h = h_ref[...]
    proj = _dot_nt(h, wa_ref[...])
    n = proj.shape[0]
    nb = n // seq
    rows = lax.broadcasted_iota(jnp.int32, (n, proj.shape[1] // 3), 0)

    def m1(r1):
        for b in range(nb):
            r1 = jnp.where(rows == b * seq, st_ref[2 * b + 1:2 * b + 2, :], r1)
        return r1

    def m2(r2):
        for b in range(nb):
            r2 = jnp.where(rows == b * seq, st_ref[2 * b:2 * b + 1, :], r2)
            r2 = jnp.where(rows == b * seq + 1, st_ref[2 * b + 1:2 * b + 2, :], r2)
        return r2

    ya, u, _ = _gated_conv(proj, cw_ref[...], m1, m2)
    ya_ref[...] = ya.astype(ya_ref.dtype)
    u_ref[...] = u
    for b in range(nb):
        conv_ref[2 * b:2 * b + 2, :] = u_ref[(b + 1) * seq - 2:(b + 1) * seq, :]

    wrows_ref[...] = jnp.zeros_like(wrows_ref)
    wrows_ref[0:wtail_ref.shape[0], :] = wtail_ref[:, layer, :]
    logit = _dot_nt(wrows_ref[...].astype(BF16), h)[0:8, :]
    logf = _log_sigmoid(logit + bf_ref[...])
    logf_ref[...] = logf
    ckl, _ = _prefix_sum_lanes(logf, _tri_incl(LANES, seg=seq), None)
    ckl_ref[...] = ckl


def _mix_a_sample(h, wt, w_ndk, tail_row, layer, bfp, cw, state, seq):
    n, d = h.shape
    c = cw.shape[1]
    c3 = 3 * c
    nb = n // seq
    full = lambda shape: pl.BlockSpec(shape, lambda i: tuple(0 for _ in shape))
    return pl.pallas_call(
        functools.partial(_mix_a_sample_kernel, seq=seq, layer=layer),
        grid=(1,),
        in_specs=[full((n, d)), pl.BlockSpec((None, c3, d), lambda i: (layer, 0, 0)),
                  _tail_spec(w_ndk, tail_row), full((8, 1)), full((CONV_W, c)),
                  full((nb * (CONV_W - 1), c))],
        out_specs=[full((n, c)), full((nb * (CONV_W - 1), c)), full((8, n)), full((8, n))],
        out_shape=[
            jax.ShapeDtypeStruct((n, c), BF16),
            jax.ShapeDtypeStruct((nb * (CONV_W - 1), c), F32),
            jax.ShapeDtypeStruct((8, n), F32),
            jax.ShapeDtypeStruct((8, n), F32),
        ],
        scratch_shapes=[pltpu.VMEM((n, c), F32), pltpu.VMEM((16, d), F32)],
        compiler_params=_params(("arbitrary",)),
        name="mix_a_sample",
    )(h, wt, w_ndk, bfp, cw, state)


def _prefix_max_lanes(x):
    lane = lax.broadcasted_iota(jnp.int32, (x.shape[0], LANES), 1)
    outs, carry = [], None
    for c in range(x.shape[1] // LANES):
        xc = x[:, c * LANES:(c + 1) * LANES]
        shift = 1
        while shift < LANES:
            xc = jnp.maximum(xc, jnp.where(lane >= shift, pltpu.roll(xc, shift, 1), NEG_BIG))
            shift *= 2
        if carry is not None:
            xc = jnp.maximum(xc, carry)
        outs.append(xc)
        carry = xc[:, LANES - 1:LANES]
    return jnp.concatenate(outs, axis=1)


def _cumsum_rows_kernel(x_ref, ck_ref, nmax_ref):
    ck, _ = _prefix_sum_lanes(x_ref[...], _tri_incl(LANES), jnp.zeros((x_ref.shape[0], 1), F32))
    ck_ref[...] = ck
    nmax_ref[...] = _prefix_max_lanes(-ck)


def _cumsum_rows(x):
    spec = pl.BlockSpec(x.shape, lambda i: (0, 0))
    return pl.pallas_call(
        _cumsum_rows_kernel,
        grid=(1,),
        in_specs=[spec],
        out_specs=[spec, spec],
        out_shape=[jax.ShapeDtypeStruct(x.shape, F32)] * 2,
        compiler_params=_params(("arbitrary",)),
        name="cumsum_rows",
    )(x)


def _wprep_kernel(w_hbm, o_ref, buf_ref, sem_ref):
    depth = pl.num_programs(1)
    tn = buf_ref.shape[1]
    s = pl.program_id(0) * depth + pl.program_id(1)
    n_steps = pl.num_programs(0) * depth

    def fetch(step, slot):
        i, layer = step // depth, step % depth
        return pltpu.make_async_copy(w_hbm.at[pl.ds(i * tn, tn), layer, :], buf_ref.at[slot], sem_ref.at[slot])

    @pl.when(s == 0)
    def _():
        fetch(s, 0).start()

    @pl.when(s + 1 < n_steps)
    def _():
        fetch(s + 1, (s + 1) % 2).start()

    fetch(s, s % 2).wait()
    o_ref[...] = buf_ref[s % 2].astype(o_ref.dtype)


def _wprep(w_ndk, n_rows, tn):
    _, depth, d = w_ndk.shape
    return pl.pallas_call(
        _wprep_kernel,
        grid=(n_rows // tn, depth),
        in_specs=[pl.BlockSpec(memory_space=pl.ANY)],
        out_specs=pl.BlockSpec((None, tn, d), lambda i, l: (l, i, 0)),
        out_shape=jax.ShapeDtypeStruct((depth, n_rows, d), BF16),
        scratch_shapes=[pltpu.VMEM((2, tn, d), F32), pltpu.SemaphoreType.DMA((2,))],
        compiler_params=_params(("arbitrary", "arbitrary")),
        name="wprep",
    )(w_ndk)


def _store_heads(ref, acc):
    n_heads = ref.shape[-3]
    for hd in range(n_heads):
        part = acc[:, hd * HEAD_DIM:(hd + 1) * HEAD_DIM]
        if len(ref.shape) == 3:
            ref[hd] = part
        else:
            seq = ref.shape[2]
            for b in range(ref.shape[0]):
                ref[b, hd] = part[b * seq:(b + 1) * seq, :]


def _qkv_kernel(h_ref, w_ref, *refs, n_earlier):
    earlier = refs[:4 * n_earlier]
    qkv_ref = refs[4 * n_earlier]
    kv_refs = refs[4 * n_earlier + 1:]
    j = pl.program_id(1)
    acc = _dot_nt(h_ref[...], w_ref[...])
    @pl.when(jnp.logical_or(j == 0, j == 3))
    def _():
        qkv_ref[...] = (acc * QK_SCALE).astype(qkv_ref.dtype)

    for kind, idx in enumerate((1, 2, 4, 5)):
        @pl.when(j == idx)
        def _(kind=kind):
            qkv_ref[...] = acc.astype(qkv_ref.dtype)
            ref = kv_refs[kind]
            if n_earlier:
                for p in range(n_earlier):
                    ref[p] = earlier[4 * p + kind][...]
                ref = ref.at[n_earlier]
            _store_heads(ref, acc)


def _qkv_proj(h, wt, first_row, tm, layer, batch, n_heads, earlier=()):
    m, d = h.shape
    seq = m // batch
    tn = n_heads * HEAD_DIM
    assert first_row % tn == 0
    blk0 = first_row // tn
    if m // tm == 1:
        block, index = (batch, n_heads, seq, HEAD_DIM), lambda i, j: (0, 0, 0, 0)
    else:
        assert batch == 1
        block, index = (None, n_heads, tm, HEAD_DIM), lambda i, j: (0, 0, i, 0)
    kv_in_spec = pl.BlockSpec(block, index)
    kv_shape = (batch, n_heads, seq, HEAD_DIM)
    n_earlier = len(earlier)
    if n_earlier:
        kv_out_spec = pl.BlockSpec((n_earlier + 1,) + block, lambda i, j: (0,) + index(i, j))
        kv_shape = (n_earlier + 1,) + kv_shape
    else:
        kv_out_spec = kv_in_spec
    return pl.pallas_call(
        functools.partial(_qkv_kernel, n_earlier=n_earlier),
        grid=(m // tm, 6),
        in_specs=[pl.BlockSpec((tm, d), lambda i, j: (i, 0)),
                  pl.BlockSpec((None, tn, d), lambda i, j: (layer, blk0 + j, 0))]
        + [kv_in_spec] * (4 * n_earlier),
        out_specs=[pl.BlockSpec((tm, tn), lambda i, j: (i, j))] + [kv_out_spec] * 4,
        out_shape=[jax.ShapeDtypeStruct((m, 6 * tn), BF16)] + [jax.ShapeDtypeStruct(kv_shape, F32)] * 4,
        compiler_params=_params(("arbitrary", "arbitrary")),
        name="qkv_proj",
    )(h, wt, *(a for kv in earlier for a in kv))


def _sb_tile(q, kb, vb, suffix, mask):
    z = _dot_nt(q, kb)
    sp = _softplus2(z)
    if mask is not None:
        sp = jnp.where(mask, sp, 0.0)
    s_incl = _dot(sp.astype(BF16), suffix)
    w = jnp.exp2(z - s_incl)
    if mask is not None:
        w = jnp.where(mask, w, 0.0)
    return _dot(w.astype(BF16), vb), s_incl[:, 0:1]


def _sb_live(r_ref):
    r_min = jnp.min(r_ref[...], axis=0, keepdims=True)
    return (jnp.max(jnp.exp2(-r_min)) > 0.0).astype(jnp.int32)


def _sb_accumulate(acc_ref, r_ref, pv, tot):
    r = r_ref[...]
    acc_ref[...] += jnp.exp2(-r) * pv
    r_ref[...] = r + tot


def _for_past_tiles(n_tiles, pair_fn, single_fn, live_fn=None):
    if live_fn is None:
        def body(jj, c):
            pair_fn(n_tiles - 1 - 2 * jj)
            return c

        lax.fori_loop(0, n_tiles // 2, body, 0)
        live = True
    else:
        def step(c):
            pair_fn(n_tiles - 1 - 2 * c[0])
            return c[0] + 1, live_fn()

        _, live = lax.while_loop(lambda c: jnp.logical_and(c[0] < n_tiles // 2, c[1] > 0), step,
                                 (jnp.int32(0), live_fn()))
        live = live > 0
    odd = n_tiles % 2 == 1
    if isinstance(odd, bool) and live is True:
        if odd:
            single_fn(0)
    else:
        pl.when(jnp.logical_and(odd, live))(lambda: single_fn(0))


def _sb_prompt_kernel(q_ref, k_ref, v_ref, o_ref, acc_ref, r_ref, *, tq):
    qi = pl.program_id(1)
    acc_ref[...] = jnp.zeros_like(acc_ref)
    r_ref[...] = jnp.zeros_like(r_ref)
    suffix = _suffix_incl(tq)
    row, col = _iota2(tq, tq)

    def kv(j):
        st = pl.multiple_of(j * tq, tq)
        return k_ref[pl.ds(st, tq), :], v_ref[pl.ds(st, tq), :]

    def block(s):
        q = q_ref[s * tq:(s + 1) * tq, :]
        acc, r = acc_ref.at[s], r_ref.at[s]

        def single(j, mask=None):
            _sb_accumulate(acc, r, *_sb_tile(q, *kv(j), suffix, mask))

        def pair(j, mask=None):
            pv_a, tot_a = _sb_tile(q, *kv(j), suffix, mask)
            pv_b, tot_b = _sb_tile(q, *kv(j - 1), suffix, None)
            _sb_accumulate(acc, r, pv_a + jnp.exp2(-tot_a) * pv_b, tot_a + tot_b)

        return single, pair, functools.partial(_sb_live, r)

    single0, pair0, live0 = block(0)
    single1, pair1, live1 = block(1)
    g0, g1 = 2 * qi, 2 * qi + 1

    @pl.when(qi == 0)
    def _():
        single0(g0, col < row)
        pair1(g1, col < row)

    @pl.when(qi > 0)
    def _():
        pair0(g0, col < row)
        pair1(g1, col < row)
        _for_past_tiles(g0 - 1, pair0, single0, live0)
        _for_past_tiles(g1 - 1, pair1, single1, live1)

    for s in range(2):
        o_ref[s * tq:(s + 1) * tq, :] = acc_ref[s].astype(o_ref.dtype)


def _sb_prompt(qkv, n_heads, tq):
    t = qkv.shape[0]
    return pl.pallas_call(
        functools.partial(_sb_prompt_kernel, tq=tq),
        grid=(n_heads, t // (2 * tq)),
        in_specs=[
            pl.BlockSpec((2 * tq, HEAD_DIM), lambda h, i: (i, h)),
            pl.BlockSpec((t, HEAD_DIM), lambda h, i: (0, n_heads + h)),
            pl.BlockSpec((t, HEAD_DIM), lambda h, i: (0, 2 * n_heads + h)),
        ],
        out_specs=pl.BlockSpec((2 * tq, HEAD_DIM), lambda h, i: (i, h)),
        out_shape=jax.ShapeDtypeStruct((t, n_heads * HEAD_DIM), BF16),
        scratch_shapes=[pltpu.VMEM((2, tq, HEAD_DIM), F32), pltpu.VMEM((2, tq, HEAD_DIM), F32)],
        compiler_params=_params(("arbitrary", "arbitrary")),
        name="sb_prompt",
    )(qkv, qkv, qkv)


def _sb_sample_kernel(q_ref, kn_ref, vn_ref, kc_ref, vc_ref, o_ref, acc_ref, r_ref, *, tk):
    q = q_ref[...]
    seq = q.shape[0]
    past = kc_ref.shape[0]
    acc_ref[...] = jnp.zeros_like(acc_ref)
    r_ref[...] = jnp.zeros_like(r_ref)
    row, col = _iota2(seq, seq)
    _sb_accumulate(acc_ref, r_ref, *_sb_tile(q, kn_ref[...], vn_ref[...], _suffix_incl(seq), col < row))
    suffix = _suffix_incl(tk)

    def kv(j):
        st = pl.multiple_of(j * tk, tk)
        return kc_ref[pl.ds(st, tk), :].astype(BF16), vc_ref[pl.ds(st, tk), :].astype(BF16)

    def single(j):
        _sb_accumulate(acc_ref, r_ref, *_sb_tile(q, *kv(j), suffix, None))

    def pair(j):
        pv_a, tot_a = _sb_tile(q, *kv(j), suffix, None)
        pv_b, tot_b = _sb_tile(q, *kv(j - 1), suffix, None)
        _sb_accumulate(acc_ref, r_ref, pv_a + jnp.exp2(-tot_a) * pv_b, tot_a + tot_b)

    _for_past_tiles(past // tk, pair, single, functools.partial(_sb_live, r_ref))
    o_ref[...] = acc_ref[...].astype(o_ref.dtype)


def _sample_specs(layer, n_heads, seq, past, first_col):
    qkv = [pl.BlockSpec((seq, HEAD_DIM), lambda b, h, k=k: (b, (first_col + k) * n_heads + h)) for k in range(3)]
    cache = pl.BlockSpec((None, None, None, past, HEAD_DIM), lambda b, h: (layer, b, h, 0, 0))
    return qkv + [cache, cache]


def _sb_sample(qkv, cache_k, cache_v, layer, seq, tk):
    n = qkv.shape[0]
    nb = n // seq
    _, _, n_heads, past, _ = cache_k.shape
    return pl.pallas_call(
        functools.partial(_sb_sample_kernel, tk=tk),
        grid=(nb, n_heads),
        in_specs=_sample_specs(layer, n_heads, seq, past, 0),
        out_specs=pl.BlockSpec((seq, HEAD_DIM), lambda b, h: (b, h)),
        out_shape=jax.ShapeDtypeStruct((n, n_heads * HEAD_DIM), BF16),
        scratch_shapes=[pltpu.VMEM((seq, HEAD_DIM), F32), pltpu.VMEM((seq, HEAD_DIM), F32)],
        compiler_params=_params(("arbitrary", "arbitrary")),
        name="sb_sample",
    )(qkv, qkv, qkv, cache_k, cache_v)


def _fox_init(m_ref, l_ref, acc_ref):
    m_ref[...] = jnp.full(m_ref.shape, NEG_BIG, F32)
    l_ref[...] = jnp.zeros_like(l_ref)
    acc_ref[...] = jnp.zeros_like(acc_ref)


def _fox_tile_t(qt, kb, vt, bias, mask, m_ref, l_ref, acc_ref):
    s = _dot(kb, qt) + jnp.concatenate([bias] * (qt.shape[1] // LANES), axis=1)
    if mask is not None:
        s = jnp.where(mask, s, NEG_BIG)
    m_prev = m_ref[...]
    m_new = jnp.maximum(m_prev, jnp.max(s, axis=0, keepdims=True))
    alpha = jnp.exp2(m_prev - m_new)
    p = jnp.exp2(s - m_new)
    l_ref[...] = alpha * l_ref[...] + jnp.sum(p, axis=0, keepdims=True)
    acc_ref[...] = alpha * acc_ref[...] + _dot(vt, p.astype(BF16))
    m_ref[...] = m_new


def _fox_prompt_kernel(q_ref, k_ref, v_ref, ck_ref, o_ref,
                       vt_ref, nb_ref, kmax_ref, edge_ref, m_ref, l_ref, acc_ref, *, tq, chunk):
    h = pl.program_id(0)
    qi = pl.program_id(1)
    t = k_ref.shape[0]
    reps = tq // LANES

    @pl.when(qi == 0)
    def _():
        sel_r, _ = _iota2(LANES, LANES)
        sel = jnp.where(sel_r == h, 1.0, 0.0).astype(BF16)
        kmax_ref[...] = jnp.zeros_like(kmax_ref)

        def body(c, carry):
            st = pl.multiple_of(c * chunk, chunk)
            vt_ref[:, pl.ds(st, chunk)] = v_ref[pl.ds(st, chunk), :].astype(F32).T.astype(BF16)
            ck_h = sum(_dot(p, sel) for p in _split_bf16(ck_ref[pl.ds(st, chunk), :], 3))
            nb_ref[pl.ds(st, chunk), :] = ck_h * (-LOG2E)
            kf = k_ref[pl.ds(st, chunk), :].astype(F32)
            kmax_ref[...] = jnp.maximum(kmax_ref[...], jnp.max(jnp.sum(kf * kf, axis=1, keepdims=True)))
            return carry

        lax.fori_loop(0, t // chunk, body, 0)

    def block(s):
        g = 2 * qi + s
        qtf = q_ref[s * tq:(s + 1) * tq, :].astype(F32).T
        qt = qtf.astype(BF16)
        m, l, acc, edge = m_ref.at[s], l_ref.at[s], acc_ref.at[s], edge_ref.at[s]
        _fox_init(m, l, acc)
        qs = pl.multiple_of(g * tq, tq)
        c0 = nb_ref[pl.ds(qs, 1), :]
        edge[...] = c0

        def tile(st, tk, mask):
            bias = nb_ref[pl.ds(st, tk), :] - c0
            _fox_tile_t(qt, k_ref[pl.ds(st, tk), :], vt_ref[:, pl.ds(st, tk)], bias, mask, m, l, acc)
            edge[...] = nb_ref[pl.ds(st, 1), :]

        q_norm2 = jnp.sum(qtf * qtf, axis=0, keepdims=True)
        reach = jnp.sqrt(q_norm2 * jnp.concatenate([kmax_ref[...]] * reps, axis=1)) * 1.001

        def live():
            left = jnp.concatenate([edge[...] - c0] * reps, axis=1)
            return (jnp.max(reach + left - m[...]) > -FOX_DEAD).astype(jnp.int32)

        def first(with_left):
            if with_left:
                krow, qcol = _iota2(2 * tq, tq)
                tile(pl.multiple_of(qs - tq, tq), 2 * tq, krow - tq <= qcol)
            else:
                krow, qcol = _iota2(tq, tq)
                tile(qs, tq, krow <= qcol)

        def rest():
            _for_past_tiles(g - 1,
                            lambda j: tile(pl.multiple_of((j - 1) * tq, tq), 2 * tq, None),
                            lambda j: tile(pl.multiple_of(j * tq, tq), tq, None),
                            live)

        return first, rest

    first0, rest0 = block(0)
    first1, rest1 = block(1)

    @pl.when(qi == 0)
    def _():
        first0(False)
        first1(True)

    @pl.when(qi > 0)
    def _():
        first0(True)
        first1(True)
        rest0()
        rest1()

    for s in range(2):
        o_ref[s * tq:(s + 1) * tq, :] = (acc_ref[s] / l_ref[s]).T.astype(o_ref.dtype)


def _fox_prompt(qkv, ck, n_heads, tq):
    t = qkv.shape[0]
    return pl.pallas_call(
        functools.partial(_fox_prompt_kernel, tq=tq, chunk=512),
        grid=(n_heads, t // (2 * tq)),
        in_specs=[
            pl.BlockSpec((2 * tq, HEAD_DIM), lambda h, i: (i, 3 * n_heads + h)),
            pl.BlockSpec((t, HEAD_DIM), lambda h, i: (0, 4 * n_heads + h)),
            pl.BlockSpec((t, HEAD_DIM), lambda h, i: (0, 5 * n_heads + h)),
            pl.BlockSpec((t, LANES), lambda h, i: (0, 0)),
        ],
        out_specs=pl.BlockSpec((2 * tq, HEAD_DIM), lambda h, i: (i, h)),
        out_shape=jax.ShapeDtypeStruct((t, n_heads * HEAD_DIM), BF16),
        scratch_shapes=[pltpu.VMEM((HEAD_DIM, t), BF16), pltpu.VMEM((t, LANES), F32),
                        pltpu.VMEM((1, LANES), F32), pltpu.VMEM((2, 1, LANES), F32),
                        pltpu.VMEM((2, 1, tq), F32), pltpu.VMEM((2, 1, tq), F32),
                        pltpu.VMEM((2, HEAD_DIM, tq), F32)],
        compiler_params=_params(("arbitrary", "arbitrary")),
        name="fox_prompt",
    )(qkv, qkv, qkv, ck)


def _fox_tile(q, kb, vb, bias, mask, m_ref, l_ref, acc_ref):
    s = _dot_nt(q, kb) + bias
    if mask is not None:
        s = jnp.where(mask, s, NEG_BIG)
    m_prev = m_ref[...]
    m_new = jnp.maximum(m_prev, jnp.max(s, axis=1, keepdims=True))
    alpha = jnp.exp2(m_prev - m_new)
    p = jnp.exp2(s - m_new)
    l_ref[...] = alpha * l_ref[...] + jnp.sum(p, axis=1, keepdims=True)
    acc_ref[...] = alpha * acc_ref[...] + _dot(p.astype(BF16), vb)
    m_ref[...] = m_new


def _fox_sample_kernel(q_ref, kn_ref, vn_ref, kc_ref, vc_ref, ckp_ref, nmax_ref, ckl_ref, o_ref,
                       m_ref, l_ref, acc_ref, *, tk):
    q = q_ref[...]
    seq = q.shape[0]
    past = kc_ref.shape[0]
    n_tiles = past // tk
    _fox_init(m_ref, l_ref, acc_ref)
    row, col = _iota2(seq, seq)
    _fox_tile(q, kn_ref[...], vn_ref[...], ckl_ref[...] * (-LOG2E), col <= row, m_ref, l_ref, acc_ref)
    c0 = ckp_ref[:, past - 1:past]

    qf = q.astype(F32)
    k_abs = jnp.max(jnp.abs(kc_ref[...]))
    reach = jnp.sqrt(jnp.sum(qf * qf, axis=1, keepdims=True) * HEAD_DIM) * (k_abs * 1.01)

    def step(c):
        st = pl.multiple_of((n_tiles - 1 - c[0]) * tk, tk)
        kb = kc_ref[pl.ds(st, tk), :].astype(BF16)
        vb = vc_ref[pl.ds(st, tk), :].astype(BF16)
        bias = (c0 - ckp_ref[:, pl.ds(st, tk)]) * LOG2E
        _fox_tile(q, kb, vb, bias, None, m_ref, l_ref, acc_ref)
        lo = pl.multiple_of(jnp.maximum(st - LANES, 0), LANES)
        left = nmax_ref[:, pl.ds(lo, LANES)][:, LANES - 1:LANES]
        live = jnp.max(reach + (c0 + left) * LOG2E - m_ref[...]) > -FOX_DEAD
        return c[0] + 1, live.astype(jnp.int32)

    lax.while_loop(lambda c: jnp.logical_and(c[0] < n_tiles, c[1] > 0), step, (jnp.int32(0), jnp.int32(1)))
    o_ref[...] = (acc_ref[...] / l_ref[...]).astype(o_ref.dtype)


def _fox_sample(qkv, cache_k, cache_v, ck_past, nmax_past, ck_local, layer, seq, tk):
    n = qkv.shape[0]
    nb = n // seq
    _, _, n_heads, past, _ = cache_k.shape
    return pl.pallas_call(
        functools.partial(_fox_sample_kernel, tk=tk),
        grid=(nb, n_heads),
        in_specs=_sample_specs(layer, n_heads, seq, past, 3) + [
            pl.BlockSpec((None, None, 1, past), lambda b, h: (h, b, 0, 0)),
            pl.BlockSpec((None, None, 1, past), lambda b, h: (h, b, 0, 0)),
            pl.BlockSpec((None, None, 1, seq), lambda b, h: (h, b, 0, 0)),
        ],
        out_specs=pl.BlockSpec((seq, HEAD_DIM), lambda b, h: (b, h)),
        out_shape=jax.ShapeDtypeStruct((n, n_heads * HEAD_DIM), BF16),
        scratch_shapes=[pltpu.VMEM((seq, 1), F32), pltpu.VMEM((seq, 1), F32), pltpu.VMEM((seq, HEAD_DIM), F32)],
        compiler_params=_params(("arbitrary", "arbitrary")),
        name="fox_sample",
    )(qkv, qkv, qkv, cache_k, cache_v, ck_past, nmax_past, ck_local)


def _out_proj_kernel(ya_ref, ysb_ref, yfox_ref, w_ref, x_ref, g_ref, xo_ref, h_ref):
    c = ya_ref.shape[1]
    s = ysb_ref.shape[1]
    acc = _dot(ya_ref[...], w_ref[0:c, :])
    acc += _dot(ysb_ref[...], w_ref[c:c + s, :])
    acc += _dot(yfox_ref[...], w_ref[c + s:, :])
    x = x_ref[...] + acc
    xo_ref[...] = x
    h_ref[...] = _rmsnorm_f32(x, g_ref[...]).astype(h_ref.dtype)


def _out_proj(ya, ysb, yfox, w, layer, x, g, tm):
    m, d = x.shape
    row = lambda width: pl.BlockSpec((tm, width), lambda i: (i, 0))
    return pl.pallas_call(
        _out_proj_kernel,
        grid=(m // tm,),
        in_specs=[row(ya.shape[1]), row(ysb.shape[1]), row(yfox.shape[1]),
                  pl.BlockSpec((None,) + w.shape[1:], lambda i: (layer, 0, 0)), row(d),
                  pl.BlockSpec((1, d), lambda i: (0, 0))],
        out_specs=[row(d), row(d)],
        out_shape=[jax.ShapeDtypeStruct((m, d), F32), jax.ShapeDtypeStruct((m, d), BF16)],
        compiler_params=_params(("arbitrary",)),
        name="out_proj",
    )(ya, ysb, yfox, w, x, g.reshape(1, d))


def _ffn_kernel(h_ref, wg_ref, wu_ref, wd_ref, x_ref, g_ref, *rest, emit_x):
    out_refs, acc_ref = rest[:-1], rest[-1]
    f = pl.program_id(1)

    @pl.when(f == 0)
    def _():
        acc_ref[...] = jnp.zeros_like(acc_ref)

    h = h_ref[...]
    gate = _dot(h, wg_ref[...])
    up = _dot(h, wu_ref[...])
    act = gate * jax.nn.sigmoid(gate) * up
    acc_ref[...] += _dot(act.astype(BF16), wd_ref[...])

    @pl.when(f == pl.num_programs(1) - 1)
    def _():
        x = x_ref[...] + acc_ref[...]
        normed = _rmsnorm_f32(x, g_ref[...])
        if emit_x:
            out_refs[0][...] = x
            out_refs[1][...] = normed.astype(out_refs[1].dtype)
        else:
            out_refs[0][...] = normed


def _ffn(h, wg, wu, wd, layer, x, g, tm, tf, emit_x):
    m, d = x.shape
    ff = wg.shape[2]
    row = pl.BlockSpec((tm, d), lambda i, f: (i, 0))
    if emit_x:
        out_specs = [row, row]
        out_shape = [jax.ShapeDtypeStruct((m, d), F32), jax.ShapeDtypeStruct((m, d), BF16)]
    else:
        out_specs = [row]
        out_shape = [jax.ShapeDtypeStruct((m, d), F32)]
    return pl.pallas_call(
        functools.partial(_ffn_kernel, emit_x=emit_x),
        grid=(m // tm, ff // tf),
        in_specs=[row,
                  pl.BlockSpec((None, d, tf), lambda i, f: (layer, 0, f)),
                  pl.BlockSpec((None, d, tf), lambda i, f: (layer, 0, f)),
                  pl.BlockSpec((None, tf, d), lambda i, f: (layer, f, 0)),
                  row,
                  pl.BlockSpec((1, d), lambda i, f: (0, 0))],
        out_specs=out_specs,
        out_shape=out_shape,
        scratch_shapes=[pltpu.VMEM((tm, d), F32)],
        compiler_params=_params(("arbitrary", "arbitrary")),
        name="ffn",
    )(h, wg, wu, wd, x, g.reshape(1, d))


def kernel(x_prompt, x_sample, state_conv, cache_sb_k, cache_sb_v, cache_fox_k, cache_fox_v, cache_fox_logf,
           norm1_g, w_in, b_f, conv_w, w_out, norm2_g, w_gate, w_up, w_down, final_g):
    depth = w_in.shape[0]
    bp, t, d = x_prompt.shape
    nb, seq, _ = x_sample.shape
    assert bp == 1, "prompt kernels carry conv rows and forget sums across row tiles of one stream"
    c = conv_w.shape[2]
    n_sb = cache_sb_k.shape[3]
    n_fox = cache_fox_k.shape[3]
    assert n_sb == n_fox and n_fox <= 8
    past = cache_sb_k.shape[2]
    sb_dim = n_sb * HEAD_DIM
    qkv0 = 3 * c
    qkv1 = qkv0 + 3 * sb_dim + 3 * n_fox * HEAD_DIM
    n_s = nb * seq
    per_head = lambda a: jnp.transpose(a, (0, 1, 3, 2, 4))
    csk, csv, cfk, cfv = (per_head(a) for a in (cache_sb_k, cache_sb_v, cache_fox_k, cache_fox_v))
    w_ndk = jnp.transpose(w_in, (2, 0, 1))
    wt = _wprep(w_ndk, qkv1, 512)
    wo, wg, wu, wd = (w.astype(BF16) for w in (w_out, w_gate, w_up, w_down))

    xp = x_prompt.reshape(t, d)
    xs = x_sample.reshape(n_s, d)
    hp = _rmsnorm(xp, norm1_g[0], 512)
    hs = _rmsnorm(xs, norm1_g[0], n_s)
    p_small, s_small = [], []
    p_kv, s_kv = [], []
    for l in range(depth):
        bf_row = jnp.pad(b_f[l], (0, LANES - n_fox)).reshape(1, LANES)
        bf_col = jnp.pad(b_f[l], (0, 8 - n_fox)).reshape(8, 1)
        last = l == depth - 1
        g_next = final_g if last else norm1_g[l + 1]

        ya, conv_p, logf_p, ck_p = _mix_a_prompt(hp, wt, w_ndk, qkv1, l, bf_row, conv_w[l], 512)
        stacking = last and l > 0
        qkv, *kv = _qkv_proj(hp, wt, qkv0, 512 if stacking else 1024, l, bp, n_sb, p_kv if stacking else ())
        p_kv.append(kv)
        ysb = _sb_prompt(qkv, n_sb, 256)
        yfox = _fox_prompt(qkv, ck_p, n_fox, 256)
        xp, h2 = _out_proj(ya, ysb, yfox, wo, l, xp, norm2_g[l], 512)
        outs = _ffn(h2, wg, wu, wd, l, xp, g_next, 512, 512, not last)
        if last:
            y_prompt = outs[0]
        else:
            xp, hp = outs
        p_small.append((conv_p, logf_p[:, :n_fox]))

        logf_past = jnp.transpose(cache_fox_logf[l], (2, 0, 1)).reshape(n_fox * nb, past)
        ck_past, nmax_past = (a.reshape(n_fox, nb, 1, past) for a in _cumsum_rows(logf_past))
        ya, conv_s, logf_s, ckl = _mix_a_sample(hs, wt, w_ndk, qkv1, l, bf_col, conv_w[l],
                                                state_conv[l].reshape(nb * (CONV_W - 1), c), seq)
        ckl = ckl[:n_fox].reshape(n_fox, nb, 1, seq)
        qkv, *kv = _qkv_proj(hs, wt, qkv0, n_s, l, nb, n_sb, s_kv if stacking else ())
        s_kv.append(kv)
        ysb = _sb_sample(qkv, csk, csv, l, seq, 256)
        yfox = _fox_sample(qkv, cfk, cfv, ck_past, nmax_past, ckl, l, seq, 512)
        xs, h2 = _out_proj(ya, ysb, yfox, wo, l, xs, norm2_g[l], n_s)
        outs = _ffn(h2, wg, wu, wd, l, xs, g_next, n_s, 512, not last)
        if last:
            y_sample = outs[0]
        else:
            xs, hs = outs
        logf_s = jnp.transpose(logf_s[:n_fox].reshape(n_fox, nb, seq), (1, 2, 0))
        s_small.append((conv_s, logf_s))

    stack = lambda news, i, shape: jnp.stack([n[i] for n in news]).reshape((depth,) + shape)
    stack_kv = lambda news: [per_head(a if depth > 1 else a[None]) for a in news[-1]]
    return (
        y_prompt.reshape(bp, t, d),
        y_sample.reshape(nb, seq, d),
        stack(p_small, 0, (bp, CONV_W - 1, c)),
        *stack_kv(p_kv),
        stack(p_small, 1, (bp, t, n_fox)),
        stack(s_small, 0, (nb, CONV_W - 1, c)),
        *stack_kv(s_kv),
        stack(s_small, 1, (nb, seq, n_fox)),
    )
```

```python
import functools

import jax
import jax.numpy as jnp
from jax import lax
from jax.experimental import pallas as pl
from jax.experimental.pallas import tpu as pltpu

F32 = jnp.float32
BF16 = jnp.bfloat16

NORM_EPS = 1e-5
HEAD_DIM = 128
CONV_W = 3
LANES = 128
VMEM_LIMIT = 56 * 1024 * 1024
NEG_BIG = -1e30
FOX_DEAD = 160.0
LOG2E = 1.4426950408889634
QK_SCALE = HEAD_DIM ** -0.5 * LOG2E


def _params(sem, vmem=VMEM_LIMIT):
    return pltpu.CompilerParams(dimension_semantics=sem, vmem_limit_bytes=vmem)


def _dot(a, b):
    return jnp.dot(a, b, preferred_element_type=F32)


def _dot_nt(a, b):
    return lax.dot_general(a, b, (((1,), (1,)), ((), ())), preferred_element_type=F32)


def _softplus2(z2):
    return jnp.maximum(z2, 0.0) + jnp.log(1.0 + jnp.exp2(-jnp.abs(z2))) * LOG2E


def _log_sigmoid(x):
    return jnp.minimum(x, 0.0) - jnp.log(1.0 + jnp.exp(-jnp.abs(x)))


def _split_bf16(x, parts):
    out = []
    rem = x
    for p in range(parts):
        hi = rem.astype(BF16)
        out.append(hi)
        if p + 1 < parts:
            rem = rem - hi.astype(F32)
    return out


def _iota2(n, m):
    return lax.broadcasted_iota(jnp.int32, (n, m), 0), lax.broadcasted_iota(jnp.int32, (n, m), 1)


def _tri_incl(n, seg=None):
    j, s = _iota2(n, n)
    t = jnp.where(j <= s, 1.0, 0.0)
    if seg is not None:
        t = jnp.where(j // seg == s // seg, t, 0.0)
    return t.astype(BF16)


def _suffix_incl(n):
    j, s = _iota2(n, n)
    return jnp.where(j >= s, 1.0, 0.0).astype(BF16)


def _prefix_sum_lanes(x, tri, carry):
    outs = []
    for c in range(x.shape[1] // LANES):
        xc = x[:, c * LANES:(c + 1) * LANES]
        loc = sum(_dot(p, tri) for p in _split_bf16(xc, 3))
        oc = loc if carry is None else loc + carry
        outs.append(oc)
        if carry is not None:
            carry = oc[:, LANES - 1:LANES]
    return jnp.concatenate(outs, axis=1), carry


def _rmsnorm_f32(x, g):
    ms = jnp.mean(x * x, axis=-1, keepdims=True)
    return x * lax.rsqrt(ms + NORM_EPS) * g


def _rmsnorm_kernel(x_ref, g_ref, o_ref):
    o_ref[...] = _rmsnorm_f32(x_ref[...], g_ref[...]).astype(o_ref.dtype)


def _rmsnorm(x, g, tm):
    m, d = x.shape
    return pl.pallas_call(
        _rmsnorm_kernel,
        grid=(m // tm,),
        in_specs=[pl.BlockSpec((tm, d), lambda i: (i, 0)), pl.BlockSpec((1, d), lambda i: (0, 0))],
        out_specs=pl.BlockSpec((tm, d), lambda i: (i, 0)),
        out_shape=jax.ShapeDtypeStruct((m, d), BF16),
        compiler_params=_params(("arbitrary",)),
        name="rmsnorm",
    )(x, g.reshape(1, d))


def _gated_conv(proj, cw, u_m1, u_m2):
    c = proj.shape[1] // 3
    g_b, g_c, hc = proj[:, :c], proj[:, c:2 * c], proj[:, 2 * c:]
    u = g_c * hc
    r2 = pltpu.roll(u, 2, 0)
    u1 = u_m1(pltpu.roll(u, 1, 0))
    u2 = u_m2(r2)
    conv = cw[0:1, :] * u2 + cw[1:2, :] * u1 + cw[2:3, :] * u
    return g_b * conv, u, r2


def _mix_a_prompt_kernel(h_ref, wa_ref, wtail_ref, bf_ref, cw_ref,
                         ya_ref, conv_ref, logf_ref, ck_ref, uprev_ref, carry_ref, wrows_ref, wf_ref, *, layer):
    i = pl.program_id(0)

    @pl.when(i == 0)
    def _():
        uprev_ref[...] = jnp.zeros_like(uprev_ref)
        carry_ref[...] = jnp.zeros_like(carry_ref)
        wrows_ref[...] = jnp.zeros_like(wrows_ref)
        wrows_ref[0:wtail_ref.shape[0], :] = wtail_ref[:, layer, :]
        wf_ref[...] = wrows_ref[...].T.astype(wf_ref.dtype)

    h = h_ref[...]
    proj = _dot_nt(h, wa_ref[...])
    tm = proj.shape[0]
    rows = lax.broadcasted_iota(jnp.int32, (tm, proj.shape[1] // 3), 0)
    prev = uprev_ref[...]
    m1 = lambda r1: jnp.where(rows == 0, prev[1:2, :], r1)
    m2 = lambda r2: jnp.where(rows == 0, prev[0:1, :], jnp.where(rows == 1, prev[1:2, :], r2))
    ya, _, r2 = _gated_conv(proj, cw_ref[...], m1, m2)
    ya_ref[...] = ya.astype(ya_ref.dtype)
    tail = r2[0:2, :]
    uprev_ref[0:2, :] = tail
    conv_ref[...] = tail

    logf = _log_sigmoid(_dot(h, wf_ref[...]) + bf_ref[...])
    logf_ref[...] = logf
    lower = _suffix_incl(tm)
    ck = sum(_dot(lower, p) for p in _split_bf16(logf, 3)) + carry_ref[...]
    ck_ref[...] = ck
    carry_ref[...] = ck[tm - 1:tm, :]


def _tail_spec(w_ndk, first_row):
    n, depth, d = w_ndk.shape
    assert first_row % (n - first_row) == 0
    return pl.BlockSpec((n - first_row, depth, d), lambda i: (first_row // (n - first_row), 0, 0))


def _mix_a_prompt(h, wt, w_ndk, tail_row, layer, bfp, cw, tm):
    t, d = h.shape
    c = cw.shape[1]
    c3 = 3 * c
    return pl.pallas_call(
        functools.partial(_mix_a_prompt_kernel, layer=layer),
        grid=(t // tm,),
        in_specs=[
            pl.BlockSpec((tm, d), lambda i: (i, 0)),
            pl.BlockSpec((None, c3, d), lambda i: (layer, 0, 0)),
            _tail_spec(w_ndk, tail_row),
            pl.BlockSpec((1, LANES), lambda i: (0, 0)),
            pl.BlockSpec((CONV_W, c), lambda i: (0, 0)),
        ],
        out_specs=[
            pl.BlockSpec((tm, c), lambda i: (i, 0)),
            pl.BlockSpec((CONV_W - 1, c), lambda i: (0, 0)),
            pl.BlockSpec((tm, LANES), lambda i: (i, 0)),
            pl.BlockSpec((tm, LANES), lambda i: (i, 0)),
        ],
        out_shape=[
            jax.ShapeDtypeStruct((t, c), BF16),
            jax.ShapeDtypeStruct((CONV_W - 1, c), F32),
            jax.ShapeDtypeStruct((t, LANES), F32),
            jax.ShapeDtypeStruct((t, LANES), F32),
        ],
        scratch_shapes=[pltpu.VMEM((8, c), F32), pltpu.VMEM((1, LANES), F32),
                        pltpu.VMEM((LANES, d), F32), pltpu.VMEM((d, LANES), BF16)],
        compiler_params=_params(("arbitrary",)),
        name="mix_a_prompt",
    )(h, wt, w_ndk, bfp, cw)


def _mix_a_sample_kernel(h_ref, wa_ref, wtail_ref, bf_ref, cw_ref, st_ref,
                         ya_ref, conv_ref, logf_ref, ckl_ref, u_ref, wrows_ref, *, seq, layer):
    h = h_ref[...]
    proj = _dot_nt(h, wa_ref[...])
    n = proj.shape[0]
    nb = n // seq
    rows = lax.broadcasted_iota(jnp.int32, (n, proj.shape[1] // 3), 0)

    def m1(r1):
        for b in range(nb):
            r1 = jnp.where(rows == b * seq, st_ref[2 * b + 1:2 * b + 2, :], r1)
        return r1

    def m2(r2):
        for b in range(nb):
            r2 = jnp.where(rows == b * seq, st_ref[2 * b:2 * b + 1, :], r2)
            r2 = jnp.where(rows == b * seq + 1, st_ref[2 * b + 1:2 * b + 2, :], r2)
        return r2

    ya, u, _ = _gated_conv(proj, cw_ref[...], m1, m2)
    ya_ref[...] = ya.astype(ya_ref.dtype)
    u_ref[...] = u
    for b in range(nb):
        conv_ref[2 * b:2 * b + 2, :] = u_ref[(b + 1) * seq - 2:(b + 1) * seq, :]

    wrows_ref[...] = jnp.zeros_like(wrows_ref)
    wrows_ref[0:wtail_ref.shape[0], :] = wtail_ref[:, layer, :]
    logit = _dot_nt(wrows_ref[...].astype(BF16), h)[0:8, :]
    logf = _log_sigmoid(logit + bf_ref[...])
    logf_ref[...] = logf
    ckl, _ = _prefix_sum_lanes(logf, _tri_incl(LANES, seg=seq), None)
    ckl_ref[...] = ckl


def _mix_a_sample(h, wt, w_ndk, tail_row, layer, bfp, cw, state, seq):
    n, d = h.shape
    c = cw.shape[1]
    c3 = 3 * c
    nb = n // seq
    full = lambda shape: pl.BlockSpec(shape, lambda i: tuple(0 for _ in shape))
    return pl.pallas_call(
        functools.partial(_mix_a_sample_kernel, seq=seq, layer=layer),
        grid=(1,),
        in_specs=[full((n, d)), pl.BlockSpec((None, c3, d), lambda i: (layer, 0, 0)),
                  _tail_spec(w_ndk, tail_row), full((8, 1)), full((CONV_W, c)),
                  full((nb * (CONV_W - 1), c))],
        out_specs=[full((n, c)), full((nb * (CONV_W - 1), c)), full((8, n)), full((8, n))],
        out_shape=[
            jax.ShapeDtypeStruct((n, c), BF16),
            jax.ShapeDtypeStruct((nb * (CONV_W - 1), c), F32),
            jax.ShapeDtypeStruct((8, n), F32),
            jax.ShapeDtypeStruct((8, n), F32),
        ],
        scratch_shapes=[pltpu.VMEM((n, c), F32), pltpu.VMEM((16, d), F32)],
        compiler_params=_params(("arbitrary",)),
        name="mix_a_sample",
    )(h, wt, w_ndk, bfp, cw, state)


def _prefix_max_lanes(x):
    lane = lax.broadcasted_iota(jnp.int32, (x.shape[0], LANES), 1)
    outs, carry = [], None
    for c in range(x.shape[1] // LANES):
        xc = x[:, c * LANES:(c + 1) * LANES]
        shift = 1
        while shift < LANES:
            xc = jnp.maximum(xc, jnp.where(lane >= shift, pltpu.roll(xc, shift, 1), NEG_BIG))
            shift *= 2
        if carry is not None:
            xc = jnp.maximum(xc, carry)
        outs.append(xc)
        carry = xc[:, LANES - 1:LANES]
    return jnp.concatenate(outs, axis=1)


def _cumsum_rows_kernel(x_ref, ck_ref, nmax_ref):
    ck, _ = _prefix_sum_lanes(x_ref[...], _tri_incl(LANES), jnp.zeros((x_ref.shape[0], 1), F32))
    ck_ref[...] = ck
    nmax_ref[...] = _prefix_max_lanes(-ck)


def _cumsum_rows(x):
    spec = pl.BlockSpec(x.shape, lambda i: (0, 0))
    return pl.pallas_call(
        _cumsum_rows_kernel,
        grid=(1,),
        in_specs=[spec],
        out_specs=[spec, spec],
        out_shape=[jax.ShapeDtypeStruct(x.shape, F32)] * 2,
        compiler_params=_params(("arbitrary",)),
        name="cumsum_rows",
    )(x)


def _wprep_kernel(w_hbm, o_ref, buf_ref, sem_ref):
    depth = pl.num_programs(1)
    tn = buf_ref.shape[1]
    s = pl.program_id(0) * depth + pl.program_id(1)
    n_steps = pl.num_programs(0) * depth

    def fetch(step, slot):
        i, layer = step // depth, step % depth
        return pltpu.make_async_copy(w_hbm.at[pl.ds(i * tn, tn), layer, :], buf_ref.at[slot], sem_ref.at[slot])

    @pl.when(s == 0)
    def _():
        fetch(s, 0).start()

    @pl.when(s + 1 < n_steps)
    def _():
        fetch(s + 1, (s + 1) % 2).start()

    fetch(s, s % 2).wait()
    o_ref[...] = buf_ref[s % 2].astype(o_ref.dtype)


def _wprep(w_ndk, n_rows, tn):
    _, depth, d = w_ndk.shape
    return pl.pallas_call(
        _wprep_kernel,
        grid=(n_rows // tn, depth),
        in_specs=[pl.BlockSpec(memory_space=pl.ANY)],
        out_specs=pl.BlockSpec((None, tn, d), lambda i, l: (l, i, 0)),
        out_shape=jax.ShapeDtypeStruct((depth, n_rows, d), BF16),
        scratch_shapes=[pltpu.VMEM((2, tn, d), F32), pltpu.SemaphoreType.DMA((2,))],
        compiler_params=_params(("arbitrary", "arbitrary")),
        name="wprep",
    )(w_ndk)


def _store_heads(ref, acc):
    n_heads = ref.shape[-3]
    for hd in range(n_heads):
        part = acc[:, hd * HEAD_DIM:(hd + 1) * HEAD_DIM]
        if len(ref.shape) == 3:
            ref[hd] = part
        else:
            seq = ref.shape[2]
            for b in range(ref.shape[0]):
                ref[b, hd] = part[b * seq:(b + 1) * seq, :]


KV_STEPS = (1, 2, 4, 5)


def _qkv_kernel(h_ref, w_ref, qkv_ref, *kv_refs):
    j = pl.program_id(1)
    acc = _dot_nt(h_ref[...], w_ref[...])
    @pl.when(jnp.logical_or(j == 0, j == 3))
    def _():
        qkv_ref[...] = (acc * QK_SCALE).astype(qkv_ref.dtype)

    for kind, idx in enumerate(KV_STEPS):
        @pl.when(j == idx)
        def _(kind=kind):
            qkv_ref[...] = acc.astype(qkv_ref.dtype)
            _store_heads(kv_refs[kind], acc)


def _qkv_stack_kernel(h_ref, w_ref, *refs, n_earlier, tiled):
    earlier = refs[:4 * n_earlier]
    qkv_ref = refs[4 * n_earlier]
    kv_out = refs[4 * n_earlier + 1:4 * n_earlier + 5]
    stage_ref, store_sem, copy_sem = refs[4 * n_earlier + 5:]
    i, j = pl.program_id(0), pl.program_id(1)
    last_i = pl.num_programs(0) - 1
    tm = h_ref.shape[0]

    def earlier_copy(p, kind):
        return pltpu.make_async_copy(earlier[4 * p + kind], kv_out[kind].at[p], copy_sem.at[4 * p + kind])

    def store_copy(kind, row_tile):
        slot = row_tile % 2
        dst = kv_out[kind].at[n_earlier]
        if tiled:
            dst = dst.at[0, :, pl.ds(row_tile * tm, tm), :]
        return pltpu.make_async_copy(stage_ref.at[kind, slot], dst, store_sem.at[kind, slot])

    @pl.when(jnp.logical_and(i == 0, j == 0))
    def _():
        for p in range(n_earlier):
            for kind in range(4):
                earlier_copy(p, kind).start()

    acc = _dot_nt(h_ref[...], w_ref[...])

    @pl.when(jnp.logical_or(j == 0, j == 3))
    def _():
        qkv_ref[...] = (acc * QK_SCALE).astype(qkv_ref.dtype)

    for kind, idx in enumerate(KV_STEPS):
        @pl.when(j == idx)
        def _(kind=kind):
            qkv_ref[...] = acc.astype(qkv_ref.dtype)

            @pl.when(i >= 2)
            def _():
                store_copy(kind, i - 2).wait()

            _store_heads(stage_ref.at[kind, i % 2], acc)
            store_copy(kind, i).start()

    @pl.when(jnp.logical_and(i == last_i, j == KV_STEPS[-1]))
    def _():
        for kind in range(4):
            @pl.when(last_i >= 1)
            def _(kind=kind):
                store_copy(kind, last_i - 1).wait()

            store_copy(kind, last_i).wait()
            for p in range(n_earlier):
                earlier_copy(p, kind).wait()


def _qkv_proj(h, wt, first_row, tm, layer, batch, n_heads, earlier=()):
    m, d = h.shape
    seq = m // batch
    tn = n_heads * HEAD_DIM
    assert first_row % tn == 0
    blk0 = first_row // tn
    tiled = m // tm > 1
    if tiled:
        assert batch == 1
        block, index = (None, n_heads, tm, HEAD_DIM), lambda i, j: (0, 0, i, 0)
    else:
        block, index = (batch, n_heads, seq, HEAD_DIM), lambda i, j: (0, 0, 0, 0)
    kv_shape = (batch, n_heads, seq, HEAD_DIM)
    n_earlier = len(earlier)
    common = dict(
        grid=(m // tm, 6),
        compiler_params=_params(("arbitrary", "arbitrary")),
    )
    in_specs = [pl.BlockSpec((tm, d), lambda i, j: (i, 0)),
                pl.BlockSpec((None, tn, d), lambda i, j: (layer, blk0 + j, 0))]
    qkv_spec = pl.BlockSpec((tm, tn), lambda i, j: (i, j))
    qkv_shape = jax.ShapeDtypeStruct((m, 6 * tn), BF16)
    if not n_earlier:
        return pl.pallas_call(
            _qkv_kernel,
            in_specs=in_specs,
            out_specs=[qkv_spec] + [pl.BlockSpec(block, index)] * 4,
            out_shape=[qkv_shape] + [jax.ShapeDtypeStruct(kv_shape, F32)] * 4,
            name="qkv_proj", **common,
        )(h, wt)
    hbm = pl.BlockSpec(memory_space=pl.ANY)
    stage = tuple(b for b in block if b is not None)
    return pl.pallas_call(
        functools.partial(_qkv_stack_kernel, n_earlier=n_earlier, tiled=tiled),
        in_specs=in_specs + [hbm] * (4 * n_earlier),
        out_specs=[qkv_spec] + [hbm] * 4,
        out_shape=[qkv_shape] + [jax.ShapeDtypeStruct((n_earlier + 1,) + kv_shape, F32)] * 4,
        scratch_shapes=[pltpu.VMEM((4, 2) + stage, F32), pltpu.SemaphoreType.DMA((4, 2)),
                        pltpu.SemaphoreType.DMA((4 * n_earlier,))],
        name="qkv_stack", **common,
    )(h, wt, *(a for kv in earlier for a in kv))


def _sb_tile(q, kb, vb, suffix, mask):
    z = _dot_nt(q, kb)
    sp = _softplus2(z)
    if mask is not None:
        sp = jnp.where(mask, sp, 0.0)
    s_incl = _dot(sp.astype(BF16), suffix)
    w = jnp.exp2(z - s_incl)
    if mask is not None:
        w = jnp.where(mask, w, 0.0)
    return _dot(w.astype(BF16), vb), s_incl[:, 0:1]


def _sb_live(r_ref):
    r_min = jnp.min(r_ref[...], axis=0, keepdims=True)
    return (jnp.max(jnp.exp2(-r_min)) > 0.0).astype(jnp.int32)


def _sb_accumulate(acc_ref, r_ref, pv, tot):
    r = r_ref[...]
    acc_ref[...] += jnp.exp2(-r) * pv
    r_ref[...] = r + tot


def _for_past_tiles(n_tiles, pair_fn, single_fn, live_fn=None):
    if live_fn is None:
        def body(jj, c):
            pair_fn(n_tiles - 1 - 2 * jj)
            return c

        lax.fori_loop(0, n_tiles // 2, body, 0)
        live = True
    else:
        def step(c):
            pair_fn(n_tiles - 1 - 2 * c[0])
            return c[0] + 1, live_fn()

        _, live = lax.while_loop(lambda c: jnp.logical_and(c[0] < n_tiles // 2, c[1] > 0), step,
                                 (jnp.int32(0), live_fn()))
        live = live > 0
    odd = n_tiles % 2 == 1
    if isinstance(odd, bool) and live is True:
        if odd:
            single_fn(0)
    else:
        pl.when(jnp.logical_and(odd, live))(lambda: single_fn(0))


def _sb_prompt_kernel(q_ref, k_ref, v_ref, o_ref, acc_ref, r_ref, *, tq):
    qi = pl.program_id(1)
    acc_ref[...] = jnp.zeros_like(acc_ref)
    r_ref[...] = jnp.zeros_like(r_ref)
    suffix = _suffix_incl(tq)
    row, col = _iota2(tq, tq)

    def kv(j):
        st = pl.multiple_of(j * tq, tq)
        return k_ref[pl.ds(st, tq), :], v_ref[pl.ds(st, tq), :]

    def block(s):
        q = q_ref[s * tq:(s + 1) * tq, :]
        acc, r = acc_ref.at[s], r_ref.at[s]

        def single(j, mask=None):
            _sb_accumulate(acc, r, *_sb_tile(q, *kv(j), suffix, mask))

        def pair(j, mask=None):
            pv_a, tot_a = _sb_tile(q, *kv(j), suffix, mask)
            pv_b, tot_b = _sb_tile(q, *kv(j - 1), suffix, None)
            _sb_accumulate(acc, r, pv_a + jnp.exp2(-tot_a) * pv_b, tot_a + tot_b)

        return single, pair, functools.partial(_sb_live, r)

    single0, pair0, live0 = block(0)
    single1, pair1, live1 = block(1)
    g0, g1 = 2 * qi, 2 * qi + 1

    @pl.when(qi == 0)
    def _():
        single0(g0, col < row)
        pair1(g1, col < row)

    @pl.when(qi > 0)
    def _():
        pair0(g0, col < row)
        pair1(g1, col < row)
        _for_past_tiles(g0 - 1, pair0, single0, live0)
        _for_past_tiles(g1 - 1, pair1, single1, live1)

    for s in range(2):
        o_ref[s * tq:(s + 1) * tq, :] = acc_ref[s].astype(o_ref.dtype)


def _sb_prompt(qkv, n_heads, tq):
    t = qkv.shape[0]
    return pl.pallas_call(
        functools.partial(_sb_prompt_kernel, tq=tq),
        grid=(n_heads, t // (2 * tq)),
        in_specs=[
            pl.BlockSpec((2 * tq, HEAD_DIM), lambda h, i: (i, h)),
            pl.BlockSpec((t, HEAD_DIM), lambda h, i: (0, n_heads + h)),
            pl.BlockSpec((t, HEAD_DIM), lambda h, i: (0, 2 * n_heads + h)),
        ],
        out_specs=pl.BlockSpec((2 * tq, HEAD_DIM), lambda h, i: (i, h)),
        out_shape=jax.ShapeDtypeStruct((t, n_heads * HEAD_DIM), BF16),
        scratch_shapes=[pltpu.VMEM((2, tq, HEAD_DIM), F32), pltpu.VMEM((2, tq, HEAD_DIM), F32)],
        compiler_params=_params(("arbitrary", "arbitrary")),
        name="sb_prompt",
    )(qkv, qkv, qkv)


def _sb_sample_kernel(q_ref, kn_ref, vn_ref, kc_ref, vc_ref, o_ref, acc_ref, r_ref, *, tk):
    q = q_ref[...]
    seq = q.shape[0]
    past = kc_ref.shape[0]
    acc_ref[...] = jnp.zeros_like(acc_ref)
    r_ref[...] = jnp.zeros_like(r_ref)
    row, col = _iota2(seq, seq)
    _sb_accumulate(acc_ref, r_ref, *_sb_tile(q, kn_ref[...], vn_ref[...], _suffix_incl(seq), col < row))
    suffix = _suffix_incl(tk)

    def kv(j):
        st = pl.multiple_of(j * tk, tk)
        return kc_ref[pl.ds(st, tk), :].astype(BF16), vc_ref[pl.ds(st, tk), :].astype(BF16)

    def single(j):
        _sb_accumulate(acc_ref, r_ref, *_sb_tile(q, *kv(j), suffix, None))

    def pair(j):
        pv_a, tot_a = _sb_tile(q, *kv(j), suffix, None)
        pv_b, tot_b = _sb_tile(q, *kv(j - 1), suffix, None)
        _sb_accumulate(acc_ref, r_ref, pv_a + jnp.exp2(-tot_a) * pv_b, tot_a + tot_b)

    _for_past_tiles(past // tk, pair, single, functools.partial(_sb_live, r_ref))
    o_ref[...] = acc_ref[...].astype(o_ref.dtype)


def _sample_specs(layer, n_heads, seq, past, first_col):
    qkv = [pl.BlockSpec((seq, HEAD_DIM), lambda b, h, k=k: (b, (first_col + k) * n_heads + h)) for k in range(3)]
    cache = pl.BlockSpec((None, None, None, past, HEAD_DIM), lambda b, h: (layer, b, h, 0, 0))
    return qkv + [cache, cache]


def _sb_sample(qkv, cache_k, cache_v, layer, seq, tk):
    n = qkv.shape[0]
    nb = n // seq
    _, _, n_heads, past, _ = cache_k.shape
    return pl.pallas_call(
        functools.partial(_sb_sample_kernel, tk=tk),
        grid=(nb, n_heads),
        in_specs=_sample_specs(layer, n_heads, seq, past, 0),
        out_specs=pl.BlockSpec((seq, HEAD_DIM), lambda b, h: (b, h)),
        out_shape=jax.ShapeDtypeStruct((n, n_heads * HEAD_DIM), BF16),
        scratch_shapes=[pltpu.VMEM((seq, HEAD_DIM), F32), pltpu.VMEM((seq, HEAD_DIM), F32)],
        compiler_params=_params(("arbitrary", "arbitrary")),
        name="sb_sample",
    )(qkv, qkv, qkv, cache_k, cache_v)


def _fox_init(m_ref, l_ref, acc_ref):
    m_ref[...] = jnp.full(m_ref.shape, NEG_BIG, F32)
    l_ref[...] = jnp.zeros_like(l_ref)
    acc_ref[...] = jnp.zeros_like(acc_ref)


def _fox_tile_t(qt, kb, vt, bias, mask, m_ref, l_ref, acc_ref):
    s = _dot(kb, qt) + jnp.concatenate([bias] * (qt.shape[1] // LANES), axis=1)
    if mask is not None:
        s = jnp.where(mask, s, NEG_BIG)
    m_prev = m_ref[...]
    m_new = jnp.maximum(m_prev, jnp.max(s, axis=0, keepdims=True))
    alpha = jnp.exp2(m_prev - m_new)
    p = jnp.exp2(s - m_new)
    l_ref[...] = alpha * l_ref[...] + jnp.sum(p, axis=0, keepdims=True)
    acc_ref[...] = alpha * acc_ref[...] + _dot(vt, p.astype(BF16))
    m_ref[...] = m_new


def _fox_prompt_kernel(q_ref, k_ref, v_ref, ck_ref, o_ref,
                       vt_ref, nb_ref, kmax_ref, edge_ref, m_ref, l_ref, acc_ref, *, tq, chunk):
    h = pl.program_id(0)
    qi = pl.program_id(1)
    t = k_ref.shape[0]
    reps = tq // LANES

    @pl.when(qi == 0)
    def _():
        sel_r, _ = _iota2(LANES, LANES)
        sel = jnp.where(sel_r == h, 1.0, 0.0).astype(BF16)
        kmax_ref[...] = jnp.zeros_like(kmax_ref)

        def body(c, carry):
            st = pl.multiple_of(c * chunk, chunk)
            vt_ref[:, pl.ds(st, chunk)] = v_ref[pl.ds(st, chunk), :].astype(F32).T.astype(BF16)
            ck_h = sum(_dot(p, sel) for p in _split_bf16(ck_ref[pl.ds(st, chunk), :], 3))
            nb_ref[pl.ds(st, chunk), :] = ck_h * (-LOG2E)
            kf = k_ref[pl.ds(st, chunk), :].astype(F32)
            kmax_ref[...] = jnp.maximum(kmax_ref[...], jnp.max(jnp.sum(kf * kf, axis=1, keepdims=True)))
            return carry

        lax.fori_loop(0, t // chunk, body, 0)

    def block(s):
        g = 2 * qi + s
        qtf = q_ref[s * tq:(s + 1) * tq, :].astype(F32).T
        qt = qtf.astype(BF16)
        m, l, acc, edge = m_ref.at[s], l_ref.at[s], acc_ref.at[s], edge_ref.at[s]
        _fox_init(m, l, acc)
        qs = pl.multiple_of(g * tq, tq)
        c0 = nb_ref[pl.ds(qs, 1), :]
        edge[...] = c0

        def tile(st, tk, mask):
            bias = nb_ref[pl.ds(st, tk), :] - c0
            _fox_tile_t(qt, k_ref[pl.ds(st, tk), :], vt_ref[:, pl.ds(st, tk)], bias, mask, m, l, acc)
            edge[...] = nb_ref[pl.ds(st, 1), :]

        q_norm2 = jnp.sum(qtf * qtf, axis=0, keepdims=True)
        reach = jnp.sqrt(q_norm2 * jnp.concatenate([kmax_ref[...]] * reps, axis=1)) * 1.001

        def live():
            left = jnp.concatenate([edge[...] - c0] * reps, axis=1)
            return (jnp.max(reach + left - m[...]) > -FOX_DEAD).astype(jnp.int32)

        def first(with_left):
            if with_left:
                krow, qcol = _iota2(2 * tq, tq)
                tile(pl.multiple_of(qs - tq, tq), 2 * tq, krow - tq <= qcol)
            else:
                krow, qcol = _iota2(tq, tq)
                tile(qs, tq, krow <= qcol)

        def rest():
            _for_past_tiles(g - 1,
                            lambda j: tile(pl.multiple_of((j - 1) * tq, tq), 2 * tq, None),
                            lambda j: tile(pl.multiple_of(j * tq, tq), tq, None),
                            live)

        return first, rest

    first0, rest0 = block(0)
    first1, rest1 = block(1)

    @pl.when(qi == 0)
    def _():
        first0(False)
        first1(True)

    @pl.when(qi > 0)
    def _():
        first0(True)
        first1(True)
        rest0()
        rest1()

    for s in range(2):
        o_ref[s * tq:(s + 1) * tq, :] = (acc_ref[s] / l_ref[s]).T.astype(o_ref.dtype)


def _fox_prompt(qkv, ck, n_heads, tq):
    t = qkv.shape[0]
    return pl.pallas_call(
        functools.partial(_fox_prompt_kernel, tq=tq, chunk=512),
        grid=(n_heads, t // (2 * tq)),
        in_specs=[
            pl.BlockSpec((2 * tq, HEAD_DIM), lambda h, i: (i, 3 * n_heads + h)),
            pl.BlockSpec((t, HEAD_DIM), lambda h, i: (0, 4 * n_heads + h)),
            pl.BlockSpec((t, HEAD_DIM), lambda h, i: (0, 5 * n_heads + h)),
            pl.BlockSpec((t, LANES), lambda h, i: (0, 0)),
        ],
        out_specs=pl.BlockSpec((2 * tq, HEAD_DIM), lambda h, i: (i, h)),
        out_shape=jax.ShapeDtypeStruct((t, n_heads * HEAD_DIM), BF16),
        scratch_shapes=[pltpu.VMEM((HEAD_DIM, t), BF16), pltpu.VMEM((t, LANES), F32),
                        pltpu.VMEM((1, LANES), F32), pltpu.VMEM((2, 1, LANES), F32),
                        pltpu.VMEM((2, 1, tq), F32), pltpu.VMEM((2, 1, tq), F32),
                        pltpu.VMEM((2, HEAD_DIM, tq), F32)],
        compiler_params=_params(("arbitrary", "arbitrary")),
        name="fox_prompt",
    )(qkv, qkv, qkv, ck)


def _fox_tile(q, kb, vb, bias, mask, m_ref, l_ref, acc_ref):
    s = _dot_nt(q, kb) + bias
    if mask is not None:
        s = jnp.where(mask, s, NEG_BIG)
    m_prev = m_ref[...]
    m_new = jnp.maximum(m_prev, jnp.max(s, axis=1, keepdims=True))
    alpha = jnp.exp2(m_prev - m_new)
    p = jnp.exp2(s - m_new)
    l_ref[...] = alpha * l_ref[...] + jnp.sum(p, axis=1, keepdims=True)
    acc_ref[...] = alpha * acc_ref[...] + _dot(p.astype(BF16), vb)
    m_ref[...] = m_new


def _fox_sample_kernel(q_ref, kn_ref, vn_ref, kc_ref, vc_ref, ckp_ref, nmax_ref, ckl_ref, o_ref,
                       m_ref, l_ref, acc_ref, *, tk):
    q = q_ref[...]
    seq = q.shape[0]
    past = kc_ref.shape[0]
    n_tiles = past // tk
    _fox_init(m_ref, l_ref, acc_ref)
    row, col = _iota2(seq, seq)
    _fox_tile(q, kn_ref[...], vn_ref[...], ckl_ref[...] * (-LOG2E), col <= row, m_ref, l_ref, acc_ref)
    c0 = ckp_ref[:, past - 1:past]

    qf = q.astype(F32)
    k_abs = jnp.max(jnp.abs(kc_ref[...]))
    reach = jnp.sqrt(jnp.sum(qf * qf, axis=1, keepdims=True) * HEAD_DIM) * (k_abs * 1.01)

    def step(c):
        st = pl.multiple_of((n_tiles - 1 - c[0]) * tk, tk)
        kb = kc_ref[pl.ds(st, tk), :].astype(BF16)
        vb = vc_ref[pl.ds(st, tk), :].astype(BF16)
        bias = (c0 - ckp_ref[:, pl.ds(st, tk)]) * LOG2E
        _fox_tile(q, kb, vb, bias, None, m_ref, l_ref, acc_ref)
        lo = pl.multiple_of(jnp.maximum(st - LANES, 0), LANES)
        left = nmax_ref[:, pl.ds(lo, LANES)][:, LANES - 1:LANES]
        live = jnp.max(reach + (c0 + left) * LOG2E - m_ref[...]) > -FOX_DEAD
        return c[0] + 1, live.astype(jnp.int32)

    lax.while_loop(lambda c: jnp.logical_and(c[0] < n_tiles, c[1] > 0), step, (jnp.int32(0), jnp.int32(1)))
    o_ref[...] = (acc_ref[...] / l_ref[...]).astype(o_ref.dtype)


def _fox_sample(qkv, cache_k, cache_v, ck_past, nmax_past, ck_local, layer, seq, tk):
    n = qkv.shape[0]
    nb = n // seq
    _, _, n_heads, past, _ = cache_k.shape
    return pl.pallas_call(
        functools.partial(_fox_sample_kernel, tk=tk),
        grid=(nb, n_heads),
        in_specs=_sample_specs(layer, n_heads, seq, past, 3) + [
            pl.BlockSpec((None, None, 1, past), lambda b, h: (h, b, 0, 0)),
            pl.BlockSpec((None, None, 1, past), lambda b, h: (h, b, 0, 0)),
            pl.BlockSpec((None, None, 1, seq), lambda b, h: (h, b, 0, 0)),
        ],
        out_specs=pl.BlockSpec((seq, HEAD_DIM), lambda b, h: (b, h)),
        out_shape=jax.ShapeDtypeStruct((n, n_heads * HEAD_DIM), BF16),
        scratch_shapes=[pltpu.VMEM((seq, 1), F32), pltpu.VMEM((seq, 1), F32), pltpu.VMEM((seq, HEAD_DIM), F32)],
        compiler_params=_params(("arbitrary", "arbitrary")),
        name="fox_sample",
    )(qkv, qkv, qkv, cache_k, cache_v, ck_past, nmax_past, ck_local)


def _out_proj_kernel(ya_ref, ysb_ref, yfox_ref, w_ref, x_ref, g_ref, xo_ref, h_ref):
    c = ya_ref.shape[1]
    s = ysb_ref.shape[1]
    acc = _dot(ya_ref[...], w_ref[0:c, :])
    acc += _dot(ysb_ref[...], w_ref[c:c + s, :])
    acc += _dot(yfox_ref[...], w_ref[c + s:, :])
    x = x_ref[...] + acc
    xo_ref[...] = x
    h_ref[...] = _rmsnorm_f32(x, g_ref[...]).astype(h_ref.dtype)


def _out_proj(ya, ysb, yfox, w, layer, x, g, tm):
    m, d = x.shape
    row = lambda width: pl.BlockSpec((tm, width), lambda i: (i, 0))
    return pl.pallas_call(
        _out_proj_kernel,
        grid=(m // tm,),
        in_specs=[row(ya.shape[1]), row(ysb.shape[1]), row(yfox.shape[1]),
                  pl.BlockSpec((None,) + w.shape[1:], lambda i: (layer, 0, 0)), row(d),
                  pl.BlockSpec((1, d), lambda i: (0, 0))],
        out_specs=[row(d), row(d)],
        out_shape=[jax.ShapeDtypeStruct((m, d), F32), jax.ShapeDtypeStruct((m, d), BF16)],
        compiler_params=_params(("arbitrary",)),
        name="out_proj",
    )(ya, ysb, yfox, w, x, g.reshape(1, d))


def _ffn_kernel(h_ref, wg_ref, wu_ref, wd_ref, x_ref, g_ref, *rest, emit_x):
    out_refs, acc_ref = rest[:-1], rest[-1]
    f = pl.program_id(1)

    @pl.when(f == 0)
    def _():
        acc_ref[...] = jnp.zeros_like(acc_ref)

    h = h_ref[...]
    gate = _dot(h, wg_ref[...])
    up = _dot(h, wu_ref[...])
    act = gate * jax.nn.sigmoid(gate) * up
    acc_ref[...] += _dot(act.astype(BF16), wd_ref[...])

    @pl.when(f == pl.num_programs(1) - 1)
    def _():
        x = x_ref[...] + acc_ref[...]
        normed = _rmsnorm_f32(x, g_ref[...])
        if emit_x:
            out_refs[0][...] = x
            out_refs[1][...] = normed.astype(out_refs[1].dtype)
        else:
            out_refs[0][...] = normed


def _ffn(h, wg, wu, wd, layer, x, g, tm, tf, emit_x):
    m, d = x.shape
    ff = wg.shape[2]
    row = pl.BlockSpec((tm, d), lambda i, f: (i, 0))
    if emit_x:
        out_specs = [row, row]
        out_shape = [jax.ShapeDtypeStruct((m, d), F32), jax.ShapeDtypeStruct((m, d), BF16)]
    else:
        out_specs = [row]
        out_shape = [jax.ShapeDtypeStruct((m, d), F32)]
    return pl.pallas_call(
        functools.partial(_ffn_kernel, emit_x=emit_x),
        grid=(m // tm, ff // tf),
        in_specs=[row,
                  pl.BlockSpec((None, d, tf), lambda i, f: (layer, 0, f)),
                  pl.BlockSpec((None, d, tf), lambda i, f: (layer, 0, f)),
                  pl.BlockSpec((None, tf, d), lambda i, f: (layer, f, 0)),
                  row,
                  pl.BlockSpec((1, d), lambda i, f: (0, 0))],
        out_specs=out_specs,
        out_shape=out_shape,
        scratch_shapes=[pltpu.VMEM((tm, d), F32)],
        compiler_params=_params(("arbitrary", "arbitrary")),
        name="ffn",
    )(h, wg, wu, wd, x, g.reshape(1, d))


def kernel(x_prompt, x_sample, state_conv, cache_sb_k, cache_sb_v, cache_fox_k, cache_fox_v, cache_fox_logf,
           norm1_g, w_in, b_f, conv_w, w_out, norm2_g, w_gate, w_up, w_down, final_g):
    depth = w_in.shape[0]
    bp, t, d = x_prompt.shape
    nb, seq, _ = x_sample.shape
    assert bp == 1, "prompt kernels carry conv rows and forget sums across row tiles of one stream"
    c = conv_w.shape[2]
    n_sb = cache_sb_k.shape[3]
    n_fox = cache_fox_k.shape[3]
    assert n_sb == n_fox and n_fox <= 8
    past = cache_sb_k.shape[2]
    sb_dim = n_sb * HEAD_DIM
    qkv0 = 3 * c
    qkv1 = qkv0 + 3 * sb_dim + 3 * n_fox * HEAD_DIM
    n_s = nb * seq
    per_head = lambda a: jnp.transpose(a, (0, 1, 3, 2, 4))
    csk, csv, cfk, cfv = (per_head(a) for a in (cache_sb_k, cache_sb_v, cache_fox_k, cache_fox_v))
    w_ndk = jnp.transpose(w_in, (2, 0, 1))
    wt = _wprep(w_ndk, qkv1, 512)
    wo, wg, wu, wd = (w.astype(BF16) for w in (w_out, w_gate, w_up, w_down))

    xp = x_prompt.reshape(t, d)
    xs = x_sample.reshape(n_s, d)
    hp = _rmsnorm(xp, norm1_g[0], 512)
    hs = _rmsnorm(xs, norm1_g[0], n_s)
    p_small, s_small = [], []
    p_kv, s_kv = [], []
    for l in range(depth):
        bf_row = jnp.pad(b_f[l], (0, LANES - n_fox)).reshape(1, LANES)
        bf_col = jnp.pad(b_f[l], (0, 8 - n_fox)).reshape(8, 1)
        last = l == depth - 1
        g_next = final_g if last else norm1_g[l + 1]

        ya, conv_p, logf_p, ck_p = _mix_a_prompt(hp, wt, w_ndk, qkv1, l, bf_row, conv_w[l], 512)
        stacking = last and l > 0
        qkv, *kv = _qkv_proj(hp, wt, qkv0, 1024, l, bp, n_sb, p_kv if stacking else ())
        p_kv.append(kv)
        ysb = _sb_prompt(qkv, n_sb, 256)
        yfox = _fox_prompt(qkv, ck_p, n_fox, 256)
        xp, h2 = _out_proj(ya, ysb, yfox, wo, l, xp, norm2_g[l], 512)
        outs = _ffn(h2, wg, wu, wd, l, xp, g_next, 512, 512, not last)
        if last:
            y_prompt = outs[0]
        else:
            xp, hp = outs
        p_small.append((conv_p, logf_p[:, :n_fox]))

        logf_past = jnp.transpose(cache_fox_logf[l], (2, 0, 1)).reshape(n_fox * nb, past)
        ck_past, nmax_past = (a.reshape(n_fox, nb, 1, past) for a in _cumsum_rows(logf_past))
        ya, conv_s, logf_s, ckl = _mix_a_sample(hs, wt, w_ndk, qkv1, l, bf_col, conv_w[l],
                                                state_conv[l].reshape(nb * (CONV_W - 1), c), seq)
        ckl = ckl[:n_fox].reshape(n_fox, nb, 1, seq)
        qkv, *kv = _qkv_proj(hs, wt, qkv0, n_s, l, nb, n_sb, s_kv if stacking else ())
        s_kv.append(kv)
        ysb = _sb_sample(qkv, csk, csv, l, seq, 256)
        yfox = _fox_sample(qkv, cfk, cfv, ck_past, nmax_past, ckl, l, seq, 512)
        xs, h2 = _out_proj(ya, ysb, yfox, wo, l, xs, norm2_g[l], n_s)
        outs = _ffn(h2, wg, wu, wd, l, xs, g_next, n_s, 512, not last)
        if last:
            y_sample = outs[0]
        else:
            xs, hs = outs
        logf_s = jnp.transpose(logf_s[:n_fox].reshape(n_fox, nb, seq), (1, 2, 0))
        s_small.append((conv_s, logf_s))

    stack = lambda news, i, shape: jnp.stack([n[i] for n in news]).reshape((depth,) + shape)
    stack_kv = lambda news: [per_head(a if depth > 1 else a[None]) for a in news[-1]]
    return (
        y_prompt.reshape(bp, t, d),
        y_sample.reshape(nb, seq, d),
        stack(p_small, 0, (bp, CONV_W - 1, c)),
        *stack_kv(p_kv),
        stack(p_small, 1, (bp, t, n_fox)),
        stack(s_small, 0, (nb, CONV_W - 1, c)),
        *stack_kv(s_kv),
        stack(s_small, 1, (nb, seq, n_fox)),
    )
```

```python
import functools

import jax
import jax.numpy as jnp
from jax import lax
from jax.experimental import pallas as pl
from jax.experimental.pallas import tpu as pltpu

F32 = jnp.float32
BF16 = jnp.bfloat16

NORM_EPS = 1e-5
HEAD_DIM = 128
CONV_W = 3
LANES = 128
VMEM_LIMIT = 56 * 1024 * 1024
NEG_BIG = -1e30
FOX_DEAD = 160.0
LOG2E = 1.4426950408889634
QK_SCALE = HEAD_DIM ** -0.5 * LOG2E


def _params(sem, vmem=VMEM_LIMIT):
    return pltpu.CompilerParams(dimension_semantics=sem, vmem_limit_bytes=vmem)


def _dot(a, b):
    return jnp.dot(a, b, preferred_element_type=F32)


def _dot_nt(a, b):
    return lax.dot_general(a, b, (((1,), (1,)), ((), ())), preferred_element_type=F32)


def _softplus2(z2):
    return jnp.maximum(z2, 0.0) + jnp.log(1.0 + jnp.exp2(-jnp.abs(z2))) * LOG2E


def _log_sigmoid(x):
    return jnp.minimum(x, 0.0) - jnp.log(1.0 + jnp.exp(-jnp.abs(x)))


def _split_bf16(x, parts):
    out = []
    rem = x
    for p in range(parts):
        hi = rem.astype(BF16)
        out.append(hi)
        if p + 1 < parts:
            rem = rem - hi.astype(F32)
    return out


def _iota2(n, m):
    return lax.broadcasted_iota(jnp.int32, (n, m), 0), lax.broadcasted_iota(jnp.int32, (n, m), 1)


def _tri_incl(n, seg=None):
    j, s = _iota2(n, n)
    t = jnp.where(j <= s, 1.0, 0.0)
    if seg is not None:
        t = jnp.where(j // seg == s // seg, t, 0.0)
    return t.astype(BF16)


def _suffix_incl(n):
    j, s = _iota2(n, n)
    return jnp.where(j >= s, 1.0, 0.0).astype(BF16)


def _prefix_sum_lanes(x, tri, carry):
    outs = []
    for c in range(x.shape[1] // LANES):
        xc = x[:, c * LANES:(c + 1) * LANES]
        loc = sum(_dot(p, tri) for p in _split_bf16(xc, 3))
        oc = loc if carry is None else loc + carry
        outs.append(oc)
        if carry is not None:
            carry = oc[:, LANES - 1:LANES]
    return jnp.concatenate(outs, axis=1), carry


def _rmsnorm_f32(x, g):
    ms = jnp.mean(x * x, axis=-1, keepdims=True)
    return x * lax.rsqrt(ms + NORM_EPS) * g


def _rmsnorm_kernel(x_ref, g_ref, o_ref):
    o_ref[...] = _rmsnorm_f32(x_ref[...], g_ref[...]).astype(o_ref.dtype)


def _rmsnorm(x, g, tm):
    m, d = x.shape
    return pl.pallas_call(
        _rmsnorm_kernel,
        grid=(m // tm,),
        in_specs=[pl.BlockSpec((tm, d), lambda i: (i, 0)), pl.BlockSpec((1, d), lambda i: (0, 0))],
        out_specs=pl.BlockSpec((tm, d), lambda i: (i, 0)),
        out_shape=jax.ShapeDtypeStruct((m, d), BF16),
        compiler_params=_params(("arbitrary",)),
        name="rmsnorm",
    )(x, g.reshape(1, d))


def _gated_conv(proj, cw, u_m1, u_m2):
    c = proj.shape[1] // 3
    g_b, g_c, hc = proj[:, :c], proj[:, c:2 * c], proj[:, 2 * c:]
    u = g_c * hc
    r2 = pltpu.roll(u, 2, 0)
    u1 = u_m1(pltpu.roll(u, 1, 0))
    u2 = u_m2(r2)
    conv = cw[0:1, :] * u2 + cw[1:2, :] * u1 + cw[2:3, :] * u
    return g_b * conv, u, r2


def _mix_a_prompt_kernel(h_ref, wa_ref, wtail_ref, bf_ref, cw_ref,
                         ya_ref, conv_ref, logf_ref, ck_ref, uprev_ref, carry_ref, wrows_ref, wf_ref, *, layer):
    i = pl.program_id(0)

    @pl.when(i == 0)
    def _():
        uprev_ref[...] = jnp.zeros_like(uprev_ref)
        carry_ref[...] = jnp.zeros_like(carry_ref)
        wrows_ref[...] = jnp.zeros_like(wrows_ref)
        wrows_ref[0:wtail_ref.shape[0], :] = wtail_ref[:, layer, :]
        wf_ref[...] = wrows_ref[...].T.astype(wf_ref.dtype)

    h = h_ref[...]
    proj = _dot_nt(h, wa_ref[...])
    tm = proj.shape[0]
    rows = lax.broadcasted_iota(jnp.int32, (tm, proj.shape[1] // 3), 0)
    prev = uprev_ref[...]
    m1 = lambda r1: jnp.where(rows == 0, prev[1:2, :], r1)
    m2 = lambda r2: jnp.where(rows == 0, prev[0:1, :], jnp.where(rows == 1, prev[1:2, :], r2))
    ya, _, r2 = _gated_conv(proj, cw_ref[...], m1, m2)
    ya_ref[...] = ya.astype(ya_ref.dtype)
    tail = r2[0:2, :]
    uprev_ref[0:2, :] = tail
    conv_ref[...] = tail

    logf = _log_sigmoid(_dot(h, wf_ref[...]) + bf_ref[...])
    logf_ref[...] = logf
    lower = _suffix_incl(tm)
    ck = sum(_dot(lower, p) for p in _split_bf16(logf, 3)) + carry_ref[...]
    ck_ref[...] = ck
    carry_ref[...] = ck[tm - 1:tm, :]


def _tail_spec(w_ndk, first_row):
    n, depth, d = w_ndk.shape
    assert first_row % (n - first_row) == 0
    return pl.BlockSpec((n - first_row, depth, d), lambda i: (first_row // (n - first_row), 0, 0))


def _mix_a_prompt(h, wt, w_ndk, tail_row, layer, bfp, cw, tm):
    t, d = h.shape
    c = cw.shape[1]
    c3 = 3 * c
    return pl.pallas_call(
        functools.partial(_mix_a_prompt_kernel, layer=layer),
        grid=(t // tm,),
        in_specs=[
            pl.BlockSpec((tm, d), lambda i: (i, 0)),
            pl.BlockSpec((None, c3, d), lambda i: (layer, 0, 0)),
            _tail_spec(w_ndk, tail_row),
            pl.BlockSpec((1, LANES), lambda i: (0, 0)),
            pl.BlockSpec((CONV_W, c), lambda i: (0, 0)),
        ],
        out_specs=[
            pl.BlockSpec((tm, c), lambda i: (i, 0)),
            pl.BlockSpec((CONV_W - 1, c), lambda i: (0, 0)),
            pl.BlockSpec((tm, LANES), lambda i: (i, 0)),
            pl.BlockSpec((tm, LANES), lambda i: (i, 0)),
        ],
        out_shape=[
            jax.ShapeDtypeStruct((t, c), BF16),
            jax.ShapeDtypeStruct((CONV_W - 1, c), F32),
            jax.ShapeDtypeStruct((t, LANES), F32),
            jax.ShapeDtypeStruct((t, LANES), F32),
        ],
        scratch_shapes=[pltpu.VMEM((8, c), F32), pltpu.VMEM((1, LANES), F32),
                        pltpu.VMEM((LANES, d), F32), pltpu.VMEM((d, LANES), BF16)],
        compiler_params=_params(("arbitrary",)),
        name="mix_a_prompt",
    )(h, wt, w_ndk, bfp, cw)


def _mix_a_sample_kernel(h_ref, wa_ref, wtail_ref, bf_ref, cw_ref, st_ref,
                         ya_ref, conv_ref, logf_ref, ckl_ref, u_ref, wrows_ref, *, seq, layer):
    h = h_ref[...]
    proj = _dot_nt(h, wa_ref[...])
    n = proj.shape[0]
    nb = n // seq
    rows = lax.broadcasted_iota(jnp.int32, (n, proj.shape[1] // 3), 0)

    def m1(r1):
        for b in range(nb):
            r1 = jnp.where(rows == b * seq, st_ref[2 * b + 1:2 * b + 2, :], r1)
        return r1

    def m2(r2):
        for b in range(nb):
            r2 = jnp.where(rows == b * seq, st_ref[2 * b:2 * b + 1, :], r2)
            r2 = jnp.where(rows == b * seq + 1, st_ref[2 * b + 1:2 * b + 2, :], r2)
        return r2

    ya, u, _ = _gated_conv(proj, cw_ref[...], m1, m2)
    ya_ref[...] = ya.astype(ya_ref.dtype)
    u_ref[...] = u
    for b in range(nb):
        conv_ref[2 * b:2 * b + 2, :] = u_ref[(b + 1) * seq - 2:(b + 1) * seq, :]

    wrows_ref[...] = jnp.zeros_like(wrows_ref)
    wrows_ref[0:wtail_ref.shape[0], :] = wtail_ref[:, layer, :]
    logit = _dot_nt(wrows_ref[...].astype(BF16), h)[0:8, :]
    logf = _log_sigmoid(logit + bf_ref[...])
    logf_ref[...] = logf
    ckl, _ = _prefix_sum_lanes(logf, _tri_incl(LANES, seg=seq), None)
    ckl_ref[...] = ckl


def _mix_a_sample(h, wt, w_ndk, tail_row, layer, bfp, cw, state, seq):
    n, d = h.shape
    c = cw.shape[1]
    c3 = 3 * c
    nb = n // seq
    full = lambda shape: pl.BlockSpec(shape, lambda i: tuple(0 for _ in shape))
    return pl.pallas_call(
        functools.partial(_mix_a_sample_kernel, seq=seq, layer=layer),
        grid=(1,),
        in_specs=[full((n, d)), pl.BlockSpec((None, c3, d), lambda i: (layer, 0, 0)),
                  _tail_spec(w_ndk, tail_row), full((8, 1)), full((CONV_W, c)),
                  full((nb * (CONV_W - 1), c))],
        out_specs=[full((n, c)), full((nb * (CONV_W - 1), c)), full((8, n)), full((8, n))],
        out_shape=[
            jax.ShapeDtypeStruct((n, c), BF16),
            jax.ShapeDtypeStruct((nb * (CONV_W - 1), c), F32),
            jax.ShapeDtypeStruct((8, n), F32),
            jax.ShapeDtypeStruct((8, n), F32),
        ],
        scratch_shapes=[pltpu.VMEM((n, c), F32), pltpu.VMEM((16, d), F32)],
        compiler_params=_params(("arbitrary",)),
        name="mix_a_sample",
    )(h, wt, w_ndk, bfp, cw, state)


def _prefix_max_lanes(x):
    lane = lax.broadcasted_iota(jnp.int32, (x.shape[0], LANES), 1)
    outs, carry = [], None
    for c in range(x.shape[1] // LANES):
        xc = x[:, c * LANES:(c + 1) * LANES]
        shift = 1
        while shift < LANES:
            xc = jnp.maximum(xc, jnp.where(lane >= shift, pltpu.roll(xc, shift, 1), NEG_BIG))
            shift *= 2
        if carry is not None:
            xc = jnp.maximum(xc, carry)
        outs.append(xc)
        carry = xc[:, LANES - 1:LANES]
    return jnp.concatenate(outs, axis=1)


def _cumsum_rows_kernel(x_ref, ck_ref, nmax_ref):
    ck, _ = _prefix_sum_lanes(x_ref[...], _tri_incl(LANES), jnp.zeros((x_ref.shape[0], 1), F32))
    ck_ref[...] = ck
    nmax_ref[...] = _prefix_max_lanes(-ck)


def _cumsum_rows(x):
    spec = pl.BlockSpec(x.shape, lambda i: (0, 0))
    return pl.pallas_call(
        _cumsum_rows_kernel,
        grid=(1,),
        in_specs=[spec],
        out_specs=[spec, spec],
        out_shape=[jax.ShapeDtypeStruct(x.shape, F32)] * 2,
        compiler_params=_params(("arbitrary",)),
        name="cumsum_rows",
    )(x)


def _wprep_kernel(w_hbm, o_ref, buf_ref, sem_ref):
    depth = pl.num_programs(1)
    tn = buf_ref.shape[1]
    s = pl.program_id(0) * depth + pl.program_id(1)
    n_steps = pl.num_programs(0) * depth

    def fetch(step, slot):
        i, layer = step // depth, step % depth
        return pltpu.make_async_copy(w_hbm.at[pl.ds(i * tn, tn), layer, :], buf_ref.at[slot], sem_ref.at[slot])

    @pl.when(s == 0)
    def _():
        fetch(s, 0).start()

    @pl.when(s + 1 < n_steps)
    def _():
        fetch(s + 1, (s + 1) % 2).start()

    fetch(s, s % 2).wait()
    o_ref[...] = buf_ref[s % 2].astype(o_ref.dtype)


def _wprep(w_ndk, n_rows, tn):
    _, depth, d = w_ndk.shape
    return pl.pallas_call(
        _wprep_kernel,
        grid=(n_rows // tn, depth),
        in_specs=[pl.BlockSpec(memory_space=pl.ANY)],
        out_specs=pl.BlockSpec((None, tn, d), lambda i, l: (l, i, 0)),
        out_shape=jax.ShapeDtypeStruct((depth, n_rows, d), BF16),
        scratch_shapes=[pltpu.VMEM((2, tn, d), F32), pltpu.SemaphoreType.DMA((2,))],
        compiler_params=_params(("arbitrary", "arbitrary")),
        name="wprep",
    )(w_ndk)


def _store_heads(ref, acc):
    n_heads = ref.shape[-3]
    for hd in range(n_heads):
        part = acc[:, hd * HEAD_DIM:(hd + 1) * HEAD_DIM]
        if len(ref.shape) == 3:
            ref[hd] = part
        else:
            seq = ref.shape[2]
            for b in range(ref.shape[0]):
                ref[b, hd] = part[b * seq:(b + 1) * seq, :]


def _qkv_kernel(h_ref, w_ref, *refs, n_earlier):
    earlier = refs[:4 * n_earlier]
    qkv_ref = refs[4 * n_earlier]
    kv_refs = refs[4 * n_earlier + 1:]
    j = pl.program_id(1)
    acc = _dot_nt(h_ref[...], w_ref[...])
    @pl.when(jnp.logical_or(j == 0, j == 3))
    def _():
        qkv_ref[...] = (acc * QK_SCALE).astype(qkv_ref.dtype)

    for kind, idx in enumerate((1, 2, 4, 5)):
        @pl.when(j == idx)
        def _(kind=kind):
            qkv_ref[...] = acc.astype(qkv_ref.dtype)
            ref = kv_refs[kind]
            if n_earlier:
                for p in range(n_earlier):
                    ref[p] = earlier[4 * p + kind][...]
                ref = ref.at[n_earlier]
            _store_heads(ref, acc)


def _qkv_proj(h, wt, first_row, tm, layer, batch, n_heads, earlier=()):
    m, d = h.shape
    seq = m // batch
    tn = n_heads * HEAD_DIM
    assert first_row % tn == 0
    blk0 = first_row // tn
    if m // tm == 1:
        block, index = (batch, n_heads, seq, HEAD_DIM), lambda i, j: (0, 0, 0, 0)
    else:
        assert batch == 1
        block, index = (None, n_heads, tm, HEAD_DIM), lambda i, j: (0, 0, i, 0)
    kv_in_spec = pl.BlockSpec(block, index)
    kv_shape = (batch, n_heads, seq, HEAD_DIM)
    n_earlier = len(earlier)
    if n_earlier:
        kv_out_spec = pl.BlockSpec((n_earlier + 1,) + block, lambda i, j: (0,) + index(i, j))
        kv_shape = (n_earlier + 1,) + kv_shape
    else:
        kv_out_spec = kv_in_spec
    return pl.pallas_call(
        functools.partial(_qkv_kernel, n_earlier=n_earlier),
        grid=(m // tm, 6),
        in_specs=[pl.BlockSpec((tm, d), lambda i, j: (i, 0)),
                  pl.BlockSpec((None, tn, d), lambda i, j: (layer, blk0 + j, 0))]
        + [kv_in_spec] * (4 * n_earlier),
        out_specs=[pl.BlockSpec((tm, tn), lambda i, j: (i, j))] + [kv_out_spec] * 4,
        out_shape=[jax.ShapeDtypeStruct((m, 6 * tn), BF16)] + [jax.ShapeDtypeStruct(kv_shape, F32)] * 4,
        compiler_params=_params(("arbitrary", "arbitrary")),
        name="qkv_proj",
    )(h, wt, *(a for kv in earlier for a in kv))


def _sb_tile(q, kb, vb, suffix, mask):
    z = _dot_nt(q, kb)
    sp = _softplus2(z)
    if mask is not None:
        sp = jnp.where(mask, sp, 0.0)
    s_incl = _dot(sp.astype(BF16), suffix)
    w = jnp.exp2(z - s_incl)
    if mask is not None:
        w = jnp.where(mask, w, 0.0)
    return _dot(w.astype(BF16), vb), s_incl[:, 0:1]


def _sb_live(r_ref):
    r_min = jnp.min(r_ref[...], axis=0, keepdims=True)
    return (jnp.max(jnp.exp2(-r_min)) > 0.0).astype(jnp.int32)


def _sb_accumulate(acc_ref, r_ref, pv, tot):
    r = r_ref[...]
    acc_ref[...] += jnp.exp2(-r) * pv
    r_ref[...] = r + tot


def _for_past_tiles(n_tiles, pair_fn, single_fn, live_fn=None):
    if live_fn is None:
        def body(jj, c):
            pair_fn(n_tiles - 1 - 2 * jj)
            return c

        lax.fori_loop(0, n_tiles // 2, body, 0)
        live = True
    else:
        def step(c):
            pair_fn(n_tiles - 1 - 2 * c[0])
            return c[0] + 1, live_fn()

        _, live = lax.while_loop(lambda c: jnp.logical_and(c[0] < n_tiles // 2, c[1] > 0), step,
                                 (jnp.int32(0), live_fn()))
        live = live > 0
    odd = n_tiles % 2 == 1
    if isinstance(odd, bool) and live is True:
        if odd:
            single_fn(0)
    else:
        pl.when(jnp.logical_and(odd, live))(lambda: single_fn(0))


def _sb_prompt_kernel(q_ref, k_ref, v_ref, o_ref, acc_ref, r_ref, *, tq):
    qi = pl.program_id(1)
    n_blocks = acc_ref.shape[0]
    acc_ref[...] = jnp.zeros_like(acc_ref)
    r_ref[...] = jnp.zeros_like(r_ref)
    suffix = _suffix_incl(tq)
    row, col = _iota2(tq, tq)

    def kv(j):
        st = pl.multiple_of(j * tq, tq)
        return k_ref[pl.ds(st, tq), :], v_ref[pl.ds(st, tq), :]

    def block(s):
        q = q_ref[s * tq:(s + 1) * tq, :]
        acc, r = acc_ref.at[s], r_ref.at[s]

        def single(j, mask=None):
            _sb_accumulate(acc, r, *_sb_tile(q, *kv(j), suffix, mask))

        def pair(j, mask=None):
            pv_a, tot_a = _sb_tile(q, *kv(j), suffix, mask)
            pv_b, tot_b = _sb_tile(q, *kv(j - 1), suffix, None)
            _sb_accumulate(acc, r, pv_a + jnp.exp2(-tot_a) * pv_b, tot_a + tot_b)

        return single, pair, functools.partial(_sb_live, r)

    blocks = [block(s) for s in range(n_blocks)]
    diag = [n_blocks * qi + s for s in range(n_blocks)]

    @pl.when(qi == 0)
    def _():
        blocks[0][0](diag[0], col < row)
        for (_, pair, _), g in zip(blocks[1:], diag[1:]):
            pair(g, col < row)
        for s, (single, pair, live) in enumerate(blocks):
            if s >= 2:
                _for_past_tiles(s - 1, pair, single, live)

    @pl.when(qi > 0)
    def _():
        for (_, pair, _), g in zip(blocks, diag):
            pair(g, col < row)
        for (single, pair, live), g in zip(blocks, diag):
            _for_past_tiles(g - 1, pair, single, live)

    for s in range(n_blocks):
        o_ref[s * tq:(s + 1) * tq, :] = acc_ref[s].astype(o_ref.dtype)


def _sb_prompt(qkv, n_heads, tq, n_blocks):
    t = qkv.shape[0]
    rows = n_blocks * tq
    return pl.pallas_call(
        functools.partial(_sb_prompt_kernel, tq=tq),
        grid=(n_heads, t // rows),
        in_specs=[
            pl.BlockSpec((rows, HEAD_DIM), lambda h, i: (i, h)),
            pl.BlockSpec((t, HEAD_DIM), lambda h, i: (0, n_heads + h)),
            pl.BlockSpec((t, HEAD_DIM), lambda h, i: (0, 2 * n_heads + h)),
        ],
        out_specs=pl.BlockSpec((rows, HEAD_DIM), lambda h, i: (i, h)),
        out_shape=jax.ShapeDtypeStruct((t, n_heads * HEAD_DIM), BF16),
        scratch_shapes=[pltpu.VMEM((n_blocks, tq, HEAD_DIM), F32), pltpu.VMEM((n_blocks, tq, HEAD_DIM), F32)],
        compiler_params=_params(("arbitrary", "arbitrary")),
        name="sb_prompt",
    )(qkv, qkv, qkv)


def _sb_sample_kernel(q_ref, kn_ref, vn_ref, kc_ref, vc_ref, o_ref, acc_ref, r_ref, *, tk):
    q = q_ref[...]
    seq = q.shape[0]
    past = kc_ref.shape[0]
    acc_ref[...] = jnp.zeros_like(acc_ref)
    r_ref[...] = jnp.zeros_like(r_ref)
    row, col = _iota2(seq, seq)
    _sb_accumulate(acc_ref, r_ref, *_sb_tile(q, kn_ref[...], vn_ref[...], _suffix_incl(seq), col < row))
    suffix = _suffix_incl(tk)

    def kv(j):
        st = pl.multiple_of(j * tk, tk)
        return kc_ref[pl.ds(st, tk), :].astype(BF16), vc_ref[pl.ds(st, tk), :].astype(BF16)

    def single(j):
        _sb_accumulate(acc_ref, r_ref, *_sb_tile(q, *kv(j), suffix, None))

    def pair(j):
        pv_a, tot_a = _sb_tile(q, *kv(j), suffix, None)
        pv_b, tot_b = _sb_tile(q, *kv(j - 1), suffix, None)
        _sb_accumulate(acc_ref, r_ref, pv_a + jnp.exp2(-tot_a) * pv_b, tot_a + tot_b)

    _for_past_tiles(past // tk, pair, single, functools.partial(_sb_live, r_ref))
    o_ref[...] = acc_ref[...].astype(o_ref.dtype)


def _sample_specs(layer, n_heads, seq, past, first_col):
    qkv = [pl.BlockSpec((seq, HEAD_DIM), lambda b, h, k=k: (b, (first_col + k) * n_heads + h)) for k in range(3)]
    cache = pl.BlockSpec((None, None, None, past, HEAD_DIM), lambda b, h: (layer, b, h, 0, 0))
    return qkv + [cache, cache]


def _sb_sample(qkv, cache_k, cache_v, layer, seq, tk):
    n = qkv.shape[0]
    nb = n // seq
    _, _, n_heads, past, _ = cache_k.shape
    return pl.pallas_call(
        functools.partial(_sb_sample_kernel, tk=tk),
        grid=(nb, n_heads),
        in_specs=_sample_specs(layer, n_heads, seq, past, 0),
        out_specs=pl.BlockSpec((seq, HEAD_DIM), lambda b, h: (b, h)),
        out_shape=jax.ShapeDtypeStruct((n, n_heads * HEAD_DIM), BF16),
        scratch_shapes=[pltpu.VMEM((seq, HEAD_DIM), F32), pltpu.VMEM((seq, HEAD_DIM), F32)],
        compiler_params=_params(("arbitrary", "arbitrary")),
        name="sb_sample",
    )(qkv, qkv, qkv, cache_k, cache_v)


def _fox_init(m_ref, l_ref, acc_ref):
    m_ref[...] = jnp.full(m_ref.shape, NEG_BIG, F32)
    l_ref[...] = jnp.zeros_like(l_ref)
    acc_ref[...] = jnp.zeros_like(acc_ref)


def _fox_tile_t(qt, kb, vt, bias, mask, m_ref, l_ref, acc_ref):
    s = _dot(kb, qt) + jnp.concatenate([bias] * (qt.shape[1] // LANES), axis=1)
    if mask is not None:
        s = jnp.where(mask, s, NEG_BIG)
    m_prev = m_ref[...]
    m_new = jnp.maximum(m_prev, jnp.max(s, axis=0, keepdims=True))
    alpha = jnp.exp2(m_prev - m_new)
    p = jnp.exp2(s - m_new)
    l_ref[...] = alpha * l_ref[...] + jnp.sum(p, axis=0, keepdims=True)
    acc_ref[...] = alpha * acc_ref[...] + _dot(vt, p.astype(BF16))
    m_ref[...] = m_new


def _fox_prompt_kernel(q_ref, k_ref, v_ref, ck_ref, o_ref,
                       vt_ref, nb_ref, kmax_ref, edge_ref, m_ref, l_ref, acc_ref, *, tq, chunk):
    h = pl.program_id(0)
    qi = pl.program_id(1)
    t = k_ref.shape[0]
    reps = tq // LANES
    n_blocks = acc_ref.shape[0]

    @pl.when(qi == 0)
    def _():
        sel_r, _ = _iota2(LANES, LANES)
        sel = jnp.where(sel_r == h, 1.0, 0.0).astype(BF16)
        kmax_ref[...] = jnp.zeros_like(kmax_ref)

        def body(c, carry):
            st = pl.multiple_of(c * chunk, chunk)
            vt_ref[:, pl.ds(st, chunk)] = v_ref[pl.ds(st, chunk), :].astype(F32).T.astype(BF16)
            ck_h = sum(_dot(p, sel) for p in _split_bf16(ck_ref[pl.ds(st, chunk), :], 3))
            nb_ref[pl.ds(st, chunk), :] = ck_h * (-LOG2E)
            kf = k_ref[pl.ds(st, chunk), :].astype(F32)
            kmax_ref[...] = jnp.maximum(kmax_ref[...], jnp.max(jnp.sum(kf * kf, axis=1, keepdims=True)))
            return carry

        lax.fori_loop(0, t // chunk, body, 0)

    def block(s):
        g = n_blocks * qi + s
        qtf = q_ref[s * tq:(s + 1) * tq, :].astype(F32).T
        qt = qtf.astype(BF16)
        m, l, acc, edge = m_ref.at[s], l_ref.at[s], acc_ref.at[s], edge_ref.at[s]
        _fox_init(m, l, acc)
        qs = pl.multiple_of(g * tq, tq)
        c0 = nb_ref[pl.ds(qs, 1), :]
        edge[...] = c0

        def tile(st, tk, mask):
            bias = nb_ref[pl.ds(st, tk), :] - c0
            _fox_tile_t(qt, k_ref[pl.ds(st, tk), :], vt_ref[:, pl.ds(st, tk)], bias, mask, m, l, acc)
            edge[...] = nb_ref[pl.ds(st, 1), :]

        q_norm2 = jnp.sum(qtf * qtf, axis=0, keepdims=True)
        reach = jnp.sqrt(q_norm2 * jnp.concatenate([kmax_ref[...]] * reps, axis=1)) * 1.001

        def live():
            left = jnp.concatenate([edge[...] - c0] * reps, axis=1)
            return (jnp.max(reach + left - m[...]) > -FOX_DEAD).astype(jnp.int32)

        def first(with_left):
            if with_left:
                krow, qcol = _iota2(2 * tq, tq)
                tile(pl.multiple_of(qs - tq, tq), 2 * tq, krow - tq <= qcol)
            else:
                krow, qcol = _iota2(tq, tq)
                tile(qs, tq, krow <= qcol)

        def rest():
            _for_past_tiles(g - 1,
                            lambda j: tile(pl.multiple_of((j - 1) * tq, tq), 2 * tq, None),
                            lambda j: tile(pl.multiple_of(j * tq, tq), tq, None),
                            live)

        return first, rest

    blocks = [block(s) for s in range(n_blocks)]

    @pl.when(qi == 0)
    def _():
        for s, (first, _) in enumerate(blocks):
            first(s > 0)
        for _, rest in blocks[2:]:
            rest()

    @pl.when(qi > 0)
    def _():
        for first, _ in blocks:
            first(True)
        for _, rest in blocks:
            rest()

    for s in range(n_blocks):
        o_ref[s * tq:(s + 1) * tq, :] = (acc_ref[s] / l_ref[s]).T.astype(o_ref.dtype)


def _fox_prompt(qkv, ck, n_heads, tq, n_blocks):
    t = qkv.shape[0]
    rows = n_blocks * tq
    return pl.pallas_call(
        functools.partial(_fox_prompt_kernel, tq=tq, chunk=512),
        grid=(n_heads, t // rows),
        in_specs=[
            pl.BlockSpec((rows, HEAD_DIM), lambda h, i: (i, 3 * n_heads + h)),
            pl.BlockSpec((t, HEAD_DIM), lambda h, i: (0, 4 * n_heads + h)),
            pl.BlockSpec((t, HEAD_DIM), lambda h, i: (0, 5 * n_heads + h)),
            pl.BlockSpec((t, LANES), lambda h, i: (0, 0)),
        ],
        out_specs=pl.BlockSpec((rows, HEAD_DIM), lambda h, i: (i, h)),
        out_shape=jax.ShapeDtypeStruct((t, n_heads * HEAD_DIM), BF16),
        scratch_shapes=[pltpu.VMEM((HEAD_DIM, t), BF16), pltpu.VMEM((t, LANES), F32),
                        pltpu.VMEM((1, LANES), F32), pltpu.VMEM((n_blocks, 1, LANES), F32),
                        pltpu.VMEM((n_blocks, 1, tq), F32), pltpu.VMEM((n_blocks, 1, tq), F32),
                        pltpu.VMEM((n_blocks, HEAD_DIM, tq), F32)],
        compiler_params=_params(("arbitrary", "arbitrary")),
        name="fox_prompt",
    )(qkv, qkv, qkv, ck)


def _fox_tile(q, kb, vb, bias, mask, m_ref, l_ref, acc_ref):
    s = _dot_nt(q, kb) + bias
    if mask is not None:
        s = jnp.where(mask, s, NEG_BIG)
    m_prev = m_ref[...]
    m_new = jnp.maximum(m_prev, jnp.max(s, axis=1, keepdims=True))
    alpha = jnp.exp2(m_prev - m_new)
    p = jnp.exp2(s - m_new)
    l_ref[...] = alpha * l_ref[...] + jnp.sum(p, axis=1, keepdims=True)
    acc_ref[...] = alpha * acc_ref[...] + _dot(p.astype(BF16), vb)
    m_ref[...] = m_new


def _fox_sample_kernel(q_ref, kn_ref, vn_ref, kc_ref, vc_ref, ckp_ref, nmax_ref, ckl_ref, o_ref,
                       m_ref, l_ref, acc_ref, *, tk):
    q = q_ref[...]
    seq = q.shape[0]
    past = kc_ref.shape[0]
    n_tiles = past // tk
    _fox_init(m_ref, l_ref, acc_ref)
    row, col = _iota2(seq, seq)
    _fox_tile(q, kn_ref[...], vn_ref[...], ckl_ref[...] * (-LOG2E), col <= row, m_ref, l_ref, acc_ref)
    c0 = ckp_ref[:, past - 1:past]

    qf = q.astype(F32)
    k_abs = jnp.max(jnp.abs(kc_ref[...]))
    reach = jnp.sqrt(jnp.sum(qf * qf, axis=1, keepdims=True) * HEAD_DIM) * (k_abs * 1.01)

    def step(c):
        st = pl.multiple_of((n_tiles - 1 - c[0]) * tk, tk)
        kb = kc_ref[pl.ds(st, tk), :].astype(BF16)
        vb = vc_ref[pl.ds(st, tk), :].astype(BF16)
        bias = (c0 - ckp_ref[:, pl.ds(st, tk)]) * LOG2E
        _fox_tile(q, kb, vb, bias, None, m_ref, l_ref, acc_ref)
        lo = pl.multiple_of(jnp.maximum(st - LANES, 0), LANES)
        left = nmax_ref[:, pl.ds(lo, LANES)][:, LANES - 1:LANES]
        live = jnp.max(reach + (c0 + left) * LOG2E - m_ref[...]) > -FOX_DEAD
        return c[0] + 1, live.astype(jnp.int32)

    lax.while_loop(lambda c: jnp.logical_and(c[0] < n_tiles, c[1] > 0), step, (jnp.int32(0), jnp.int32(1)))
    o_ref[...] = (acc_ref[...] / l_ref[...]).astype(o_ref.dtype)


def _fox_sample(qkv, cache_k, cache_v, ck_past, nmax_past, ck_local, layer, seq, tk):
    n = qkv.shape[0]
    nb = n // seq
    _, _, n_heads, past, _ = cache_k.shape
    return pl.pallas_call(
        functools.partial(_fox_sample_kernel, tk=tk),
        grid=(nb, n_heads),
        in_specs=_sample_specs(layer, n_heads, seq, past, 3) + [
            pl.BlockSpec((None, None, 1, past), lambda b, h: (h, b, 0, 0)),
            pl.BlockSpec((None, None, 1, past), lambda b, h: (h, b, 0, 0)),
            pl.BlockSpec((None, None, 1, seq), lambda b, h: (h, b, 0, 0)),
        ],
        out_specs=pl.BlockSpec((seq, HEAD_DIM), lambda b, h: (b, h)),
        out_shape=jax.ShapeDtypeStruct((n, n_heads * HEAD_DIM), BF16),
        scratch_shapes=[pltpu.VMEM((seq, 1), F32), pltpu.VMEM((seq, 1), F32), pltpu.VMEM((seq, HEAD_DIM), F32)],
        compiler_params=_params(("arbitrary", "arbitrary")),
        name="fox_sample",
    )(qkv, qkv, qkv, cache_k, cache_v, ck_past, nmax_past, ck_local)


def _out_proj_kernel(ya_ref, ysb_ref, yfox_ref, w_ref, x_ref, g_ref, xo_ref, h_ref):
    c = ya_ref.shape[1]
    s = ysb_ref.shape[1]
    acc = _dot(ya_ref[...], w_ref[0:c, :])
    acc += _dot(ysb_ref[...], w_ref[c:c + s, :])
    acc += _dot(yfox_ref[...], w_ref[c + s:, :])
    x = x_ref[...] + acc
    xo_ref[...] = x
    h_ref[...] = _rmsnorm_f32(x, g_ref[...]).astype(h_ref.dtype)


def _out_proj(ya, ysb, yfox, w, layer, x, g, tm):
    m, d = x.shape
    row = lambda width: pl.BlockSpec((tm, width), lambda i: (i, 0))
    return pl.pallas_call(
        _out_proj_kernel,
        grid=(m // tm,),
        in_specs=[row(ya.shape[1]), row(ysb.shape[1]), row(yfox.shape[1]),
                  pl.BlockSpec((None,) + w.shape[1:], lambda i: (layer, 0, 0)), row(d),
                  pl.BlockSpec((1, d), lambda i: (0, 0))],
        out_specs=[row(d), row(d)],
        out_shape=[jax.ShapeDtypeStruct((m, d), F32), jax.ShapeDtypeStruct((m, d), BF16)],
        compiler_params=_params(("arbitrary",)),
        name="out_proj",
    )(ya, ysb, yfox, w, x, g.reshape(1, d))


def _ffn_kernel(h_ref, wg_ref, wu_ref, wd_ref, x_ref, g_ref, *rest, emit_x):
    out_refs, acc_ref = rest[:-1], rest[-1]
    f = pl.program_id(1)

    @pl.when(f == 0)
    def _():
        acc_ref[...] = jnp.zeros_like(acc_ref)

    h = h_ref[...]
    gate = _dot(h, wg_ref[...])
    up = _dot(h, wu_ref[...])
    act = gate * jax.nn.sigmoid(gate) * up
    acc_ref[...] += _dot(act.astype(BF16), wd_ref[...])

    @pl.when(f == pl.num_programs(1) - 1)
    def _():
        x = x_ref[...] + acc_ref[...]
        normed = _rmsnorm_f32(x, g_ref[...])
        if emit_x:
            out_refs[0][...] = x
            out_refs[1][...] = normed.astype(out_refs[1].dtype)
        else:
            out_refs[0][...] = normed


def _ffn(h, wg, wu, wd, layer, x, g, tm, tf, emit_x):
    m, d = x.shape
    ff = wg.shape[2]
    row = pl.BlockSpec((tm, d), lambda i, f: (i, 0))
    if emit_x:
        out_specs = [row, row]
        out_shape = [jax.ShapeDtypeStruct((m, d), F32), jax.ShapeDtypeStruct((m, d), BF16)]
    else:
        out_specs = [row]
        out_shape = [jax.ShapeDtypeStruct((m, d), F32)]
    return pl.pallas_call(
        functools.partial(_ffn_kernel, emit_x=emit_x),
        grid=(m // tm, ff // tf),
        in_specs=[row,
                  pl.BlockSpec((None, d, tf), lambda i, f: (layer, 0, f)),
                  pl.BlockSpec((None, d, tf), lambda i, f: (layer, 0, f)),
                  pl.BlockSpec((None, tf, d), lambda i, f: (layer, f, 0)),
                  row,
                  pl.BlockSpec((1, d), lambda i, f: (0, 0))],
        out_specs=out_specs,
        out_shape=out_shape,
        scratch_shapes=[pltpu.VMEM((tm, d), F32)],
        compiler_params=_params(("arbitrary", "arbitrary")),
        name="ffn",
    )(h, wg, wu, wd, x, g.reshape(1, d))


def kernel(x_prompt, x_sample, state_conv, cache_sb_k, cache_sb_v, cache_fox_k, cache_fox_v, cache_fox_logf,
           norm1_g, w_in, b_f, conv_w, w_out, norm2_g, w_gate, w_up, w_down, final_g):
    depth = w_in.shape[0]
    bp, t, d = x_prompt.shape
    nb, seq, _ = x_sample.shape
    assert bp == 1, "prompt kernels carry conv rows and forget sums across row tiles of one stream"
    c = conv_w.shape[2]
    n_sb = cache_sb_k.shape[3]
    n_fox = cache_fox_k.shape[3]
    assert n_sb == n_fox and n_fox <= 8
    past = cache_sb_k.shape[2]
    sb_dim = n_sb * HEAD_DIM
    qkv0 = 3 * c
    qkv1 = qkv0 + 3 * sb_dim + 3 * n_fox * HEAD_DIM
    n_s = nb * seq
    per_head = lambda a: jnp.transpose(a, (0, 1, 3, 2, 4))
    csk, csv, cfk, cfv = (per_head(a) for a in (cache_sb_k, cache_sb_v, cache_fox_k, cache_fox_v))
    w_ndk = jnp.transpose(w_in, (2, 0, 1))
    wt = _wprep(w_ndk, qkv1, 512)
    wo, wg, wu, wd = (w.astype(BF16) for w in (w_out, w_gate, w_up, w_down))

    xp = x_prompt.reshape(t, d)
    xs = x_sample.reshape(n_s, d)
    hp = _rmsnorm(xp, norm1_g[0], 512)
    hs = _rmsnorm(xs, norm1_g[0], n_s)
    p_small, s_small = [], []
    p_kv, s_kv = [], []
    for l in range(depth):
        bf_row = jnp.pad(b_f[l], (0, LANES - n_fox)).reshape(1, LANES)
        bf_col = jnp.pad(b_f[l], (0, 8 - n_fox)).reshape(8, 1)
        last = l == depth - 1
        g_next = final_g if last else norm1_g[l + 1]

        ya, conv_p, logf_p, ck_p = _mix_a_prompt(hp, wt, w_ndk, qkv1, l, bf_row, conv_w[l], 512)
        stacking = last and l > 0
        qkv, *kv = _qkv_proj(hp, wt, qkv0, 512 if stacking else 1024, l, bp, n_sb, p_kv if stacking else ())
        p_kv.append(kv)
        ysb = _sb_prompt(qkv, n_sb, 256, 4)
        yfox = _fox_prompt(qkv, ck_p, n_fox, 256, 4)
        xp, h2 = _out_proj(ya, ysb, yfox, wo, l, xp, norm2_g[l], 512)
        outs = _ffn(h2, wg, wu, wd, l, xp, g_next, 512, 512, not last)
        if last:
            y_prompt = outs[0]
        else:
            xp, hp = outs
        p_small.append((conv_p, logf_p[:, :n_fox]))

        logf_past = jnp.transpose(cache_fox_logf[l], (2, 0, 1)).reshape(n_fox * nb, past)
        ck_past, nmax_past = (a.reshape(n_fox, nb, 1, past) for a in _cumsum_rows(logf_past))
        ya, conv_s, logf_s, ckl = _mix_a_sample(hs, wt, w_ndk, qkv1, l, bf_col, conv_w[l],
                                                state_conv[l].reshape(nb * (CONV_W - 1), c), seq)
        ckl = ckl[:n_fox].reshape(n_fox, nb, 1, seq)
        qkv, *kv = _qkv_proj(hs, wt, qkv0, n_s, l, nb, n_sb, s_kv if stacking else ())
        s_kv.append(kv)
        ysb = _sb_sample(qkv, csk, csv, l, seq, 256)
        yfox = _fox_sample(qkv, cfk, cfv, ck_past, nmax_past, ckl, l, seq, 512)
        xs, h2 = _out_proj(ya, ysb, yfox, wo, l, xs, norm2_g[l], n_s)
        outs = _ffn(h2, wg, wu, wd, l, xs, g_next, n_s, 512, not last)
        if last:
            y_sample = outs[0]
        else:
            xs, hs = outs
        logf_s = jnp.transpose(logf_s[:n_fox].reshape(n_fox, nb, seq), (1, 2, 0))
        s_small.append((conv_s, logf_s))

    stack = lambda news, i, shape: jnp.stack([n[i] for n in news]).reshape((depth,) + shape)
    stack_kv = lambda news: [per_head(a if depth > 1 else a[None]) for a in news[-1]]
    return (
        y_prompt.reshape(bp, t, d),
        y_sample.reshape(nb, seq, d),
        stack(p_small, 0, (bp, CONV_W - 1, c)),
        *stack_kv(p_kv),
        stack(p_small, 1, (bp, t, n_fox)),
        stack(s_small, 0, (nb, CONV_W - 1, c)),
        *stack_kv(s_kv),
        stack(s_small, 1, (nb, seq, n_fox)),
    )
```

```python
import functools

import jax
import jax.numpy as jnp
from jax import lax
from jax.experimental import pallas as pl
from jax.experimental.pallas import tpu as pltpu

F32 = jnp.float32
BF16 = jnp.bfloat16

NORM_EPS = 1e-5
HEAD_DIM = 128
CONV_W = 3
LANES = 128
VMEM_LIMIT = 56 * 1024 * 1024
NEG_BIG = -1e30
FOX_DEAD = 160.0
LOG2E = 1.4426950408889634
QK_SCALE = HEAD_DIM ** -0.5 * LOG2E


def _params(sem, vmem=VMEM_LIMIT):
    return pltpu.CompilerParams(dimension_semantics=sem, vmem_limit_bytes=vmem)


def _dot(a, b):
    return jnp.dot(a, b, preferred_element_type=F32)


def _dot_nt(a, b):
    return lax.dot_general(a, b, (((1,), (1,)), ((), ())), preferred_element_type=F32)


def _softplus2(z2):
    return jnp.maximum(z2, 0.0) + jnp.log(1.0 + jnp.exp2(-jnp.abs(z2))) * LOG2E


def _log_sigmoid(x):
    return jnp.minimum(x, 0.0) - jnp.log(1.0 + jnp.exp(-jnp.abs(x)))


def _split_bf16(x, parts):
    out = []
    rem = x
    for p in range(parts):
        hi = rem.astype(BF16)
        out.append(hi)
        if p + 1 < parts:
            rem = rem - hi.astype(F32)
    return out


def _iota2(n, m):
    return lax.broadcasted_iota(jnp.int32, (n, m), 0), lax.broadcasted_iota(jnp.int32, (n, m), 1)


def _tri_incl(n, seg=None):
    j, s = _iota2(n, n)
    t = jnp.where(j <= s, 1.0, 0.0)
    if seg is not None:
        t = jnp.where(j // seg == s // seg, t, 0.0)
    return t.astype(BF16)


def _suffix_incl(n):
    j, s = _iota2(n, n)
    return jnp.where(j >= s, 1.0, 0.0).astype(BF16)


def _prefix_sum_lanes(x, tri, carry):
    outs = []
    for c in range(x.shape[1] // LANES):
        xc = x[:, c * LANES:(c + 1) * LANES]
        loc = sum(_dot(p, tri) for p in _split_bf16(xc, 3))
        oc = loc if carry is None else loc + carry
        outs.append(oc)
        if carry is not None:
            carry = oc[:, LANES - 1:LANES]
    return jnp.concatenate(outs, axis=1), carry


def _rmsnorm_f32(x, g):
    ms = jnp.mean(x * x, axis=-1, keepdims=True)
    return x * lax.rsqrt(ms + NORM_EPS) * g


def _rmsnorm_kernel(x_ref, g_ref, o_ref):
    o_ref[...] = _rmsnorm_f32(x_ref[...], g_ref[...]).astype(o_ref.dtype)


def _rmsnorm(x, g, tm):
    m, d = x.shape
    return pl.pallas_call(
        _rmsnorm_kernel,
        grid=(m // tm,),
        in_specs=[pl.BlockSpec((tm, d), lambda i: (i, 0)), pl.BlockSpec((1, d), lambda i: (0, 0))],
        out_specs=pl.BlockSpec((tm, d), lambda i: (i, 0)),
        out_shape=jax.ShapeDtypeStruct((m, d), BF16),
        compiler_params=_params(("arbitrary",)),
        name="rmsnorm",
    )(x, g.reshape(1, d))


def _gated_conv(proj, cw, u_m1, u_m2):
    c = proj.shape[1] // 3
    g_b, g_c, hc = proj[:, :c], proj[:, c:2 * c], proj[:, 2 * c:]
    u = g_c * hc
    r2 = pltpu.roll(u, 2, 0)
    u1 = u_m1(pltpu.roll(u, 1, 0))
    u2 = u_m2(r2)
    conv = cw[0:1, :] * u2 + cw[1:2, :] * u1 + cw[2:3, :] * u
    return g_b * conv, u, r2


def _mix_a_prompt_kernel(h_ref, wa_ref, wtail_ref, bf_ref, cw_ref,
                         ya_ref, conv_ref, logf_ref, ck_ref, uprev_ref, carry_ref, wrows_ref, wf_ref, *, layer):
    i = pl.program_id(0)

    @pl.when(i == 0)
    def _():
        uprev_ref[...] = jnp.zeros_like(uprev_ref)
        carry_ref[...] = jnp.zeros_like(carry_ref)
        wrows_ref[...] = jnp.zeros_like(wrows_ref)
        wrows_ref[0:wtail_ref.shape[0], :] = wtail_ref[:, layer, :]
        wf_ref[...] = wrows_ref[...].T.astype(wf_ref.dtype)

    h = h_ref[...]
    proj = _dot_nt(h, wa_ref[...])
    tm = proj.shape[0]
    rows = lax.broadcasted_iota(jnp.int32, (tm, proj.shape[1] // 3), 0)
    prev = uprev_ref[...]
    m1 = lambda r1: jnp.where(rows == 0, prev[1:2, :], r1)
    m2 = lambda r2: jnp.where(rows == 0, prev[0:1, :], jnp.where(rows == 1, prev[1:2, :], r2))
    ya, _, r2 = _gated_conv(proj, cw_ref[...], m1, m2)
    ya_ref[...] = ya.astype(ya_ref.dtype)
    tail = r2[0:2, :]
    uprev_ref[0:2, :] = tail
    conv_ref[...] = tail

    logf = _log_sigmoid(_dot(h, wf_ref[...]) + bf_ref[...])
    logf_ref[...] = logf
    lower = _suffix_incl(tm)
    ck = sum(_dot(lower, p) for p in _split_bf16(logf, 3)) + carry_ref[...]
    ck_ref[...] = ck
    carry_ref[...] = ck[tm - 1:tm, :]


def _tail_spec(w_ndk, first_row):
    n, depth, d = w_ndk.shape
    assert first_row % (n - first_row) == 0
    return pl.BlockSpec((n - first_row, depth, d), lambda i: (first_row // (n - first_row), 0, 0))


def _mix_a_prompt(h, wt, w_ndk, tail_row, layer, bfp, cw, tm):
    t, d = h.shape
    c = cw.shape[1]
    c3 = 3 * c
    return pl.pallas_call(
        functools.partial(_mix_a_prompt_kernel, layer=layer),
        grid=(t // tm,),
        in_specs=[
            pl.BlockSpec((tm, d), lambda i: (i, 0)),
            pl.BlockSpec((None, c3, d), lambda i: (layer, 0, 0)),
            _tail_spec(w_ndk, tail_row),
            pl.BlockSpec((1, LANES), lambda i: (0, 0)),
            pl.BlockSpec((CONV_W, c), lambda i: (0, 0)),
        ],
        out_specs=[
            pl.BlockSpec((tm, c), lambda i: (i, 0)),
            pl.BlockSpec((CONV_W - 1, c), lambda i: (0, 0)),
            pl.BlockSpec((tm, LANES), lambda i: (i, 0)),
            pl.BlockSpec((tm, LANES), lambda i: (i, 0)),
        ],
        out_shape=[
            jax.ShapeDtypeStruct((t, c), BF16),
            jax.ShapeDtypeStruct((CONV_W - 1, c), F32),
            jax.ShapeDtypeStruct((t, LANES), F32),
            jax.ShapeDtypeStruct((t, LANES), F32),
        ],
        scratch_shapes=[pltpu.VMEM((8, c), F32), pltpu.VMEM((1, LANES), F32),
                        pltpu.VMEM((LANES, d), F32), pltpu.VMEM((d, LANES), BF16)],
        compiler_params=_params(("arbitrary",)),
        name="mix_a_prompt",
    )(h, wt, w_ndk, bfp, cw)


def _mix_a_sample_kernel(h_ref, wa_ref, wtail_ref, bf_ref, cw_ref, st_ref,
                         ya_ref, conv_ref, logf_ref, ckl_ref, u_ref, wrows_ref, *, seq, layer):
    h = h_ref[...]
    proj = _dot_nt(h, wa_ref[...])
    n = proj.shape[0]
    nb = n // seq
    rows = lax.broadcasted_iota(jnp.int32, (n, proj.shape[1] // 3), 0)

    def m1(r1):
        for b in range(nb):
            r1 = jnp.where(rows == b * seq, st_ref[2 * b + 1:2 * b + 2, :], r1)
        return r1

    def m2(r2):
        for b in range(nb):
            r2 = jnp.where(rows == b * seq, st_ref[2 * b:2 * b + 1, :], r2)
            r2 = jnp.where(rows == b * seq + 1, st_ref[2 * b + 1:2 * b + 2, :], r2)
        return r2

    ya, u, _ = _gated_conv(proj, cw_ref[...], m1, m2)
    ya_ref[...] = ya.astype(ya_ref.dtype)
    u_ref[...] = u
    for b in range(nb):
        conv_ref[2 * b:2 * b + 2, :] = u_ref[(b + 1) * seq - 2:(b + 1) * seq, :]

    wrows_ref[...] = jnp.zeros_like(wrows_ref)
    wrows_ref[0:wtail_ref.shape[0], :] = wtail_ref[:, layer, :]
    logit = _dot_nt(wrows_ref[...].astype(BF16), h)[0:8, :]
    logf = _log_sigmoid(logit + bf_ref[...])
    logf_ref[...] = logf
    ckl, _ = _prefix_sum_lanes(logf, _tri_incl(LANES, seg=seq), None)
    ckl_ref[...] = ckl


def _mix_a_sample(h, wt, w_ndk, tail_row, layer, bfp, cw, state, seq):
    n, d = h.shape
    c = cw.shape[1]
    c3 = 3 * c
    nb = n // seq
    full = lambda shape: pl.BlockSpec(shape, lambda i: tuple(0 for _ in shape))
    return pl.pallas_call(
        functools.partial(_mix_a_sample_kernel, seq=seq, layer=layer),
        grid=(1,),
        in_specs=[full((n, d)), pl.BlockSpec((None, c3, d), lambda i: (layer, 0, 0)),
                  _tail_spec(w_ndk, tail_row), full((8, 1)), full((CONV_W, c)),
                  full((nb * (CONV_W - 1), c))],
        out_specs=[full((n, c)), full((nb * (CONV_W - 1), c)), full((8, n)), full((8, n))],
        out_shape=[
            jax.ShapeDtypeStruct((n, c), BF16),
            jax.ShapeDtypeStruct((nb * (CONV_W - 1), c), F32),
            jax.ShapeDtypeStruct((8, n), F32),
            jax.ShapeDtypeStruct((8, n), F32),
        ],
        scratch_shapes=[pltpu.VMEM((n, c), F32), pltpu.VMEM((16, d), F32)],
        compiler_params=_params(("arbitrary",)),
        name="mix_a_sample",
    )(h, wt, w_ndk, bfp, cw, state)


def _prefix_max_lanes(x):
    lane = lax.broadcasted_iota(jnp.int32, (x.shape[0], LANES), 1)
    outs, carry = [], None
    for c in range(x.shape[1] // LANES):
        xc = x[:, c * LANES:(c + 1) * LANES]
        shift = 1
        while shift < LANES:
            xc = jnp.maximum(xc, jnp.where(lane >= shift, pltpu.roll(xc, shift, 1), NEG_BIG))
            shift *= 2
        if carry is not None:
            xc = jnp.maximum(xc, carry)
        outs.append(xc)
        carry = xc[:, LANES - 1:LANES]
    return jnp.concatenate(outs, axis=1)


def _cumsum_rows_kernel(x_ref, ck_ref, nmax_ref):
    ck, _ = _prefix_sum_lanes(x_ref[...], _tri_incl(LANES), jnp.zeros((x_ref.shape[0], 1), F32))
    ck_ref[...] = ck
    nmax_ref[...] = _prefix_max_lanes(-ck)


def _cumsum_rows(x):
    spec = pl.BlockSpec(x.shape, lambda i: (0, 0))
    return pl.pallas_call(
        _cumsum_rows_kernel,
        grid=(1,),
        in_specs=[spec],
        out_specs=[spec, spec],
        out_shape=[jax.ShapeDtypeStruct(x.shape, F32)] * 2,
        compiler_params=_params(("arbitrary",)),
        name="cumsum_rows",
    )(x)


def _wprep_kernel(w_hbm, o_ref, buf_ref, sem_ref):
    depth = pl.num_programs(1)
    tn = buf_ref.shape[1]
    s = pl.program_id(0) * depth + pl.program_id(1)
    n_steps = pl.num_programs(0) * depth

    def fetch(step, slot):
        i, layer = step // depth, step % depth
        return pltpu.make_async_copy(w_hbm.at[pl.ds(i * tn, tn), layer, :], buf_ref.at[slot], sem_ref.at[slot])

    @pl.when(s == 0)
    def _():
        fetch(s, 0).start()

    @pl.when(s + 1 < n_steps)
    def _():
        fetch(s + 1, (s + 1) % 2).start()

    fetch(s, s % 2).wait()
    o_ref[...] = buf_ref[s % 2].astype(o_ref.dtype)


def _wprep(w_ndk, n_rows, tn):
    _, depth, d = w_ndk.shape
    return pl.pallas_call(
        _wprep_kernel,
        grid=(n_rows // tn, depth),
        in_specs=[pl.BlockSpec(memory_space=pl.ANY)],
        out_specs=pl.BlockSpec((None, tn, d), lambda i, l: (l, i, 0)),
        out_shape=jax.ShapeDtypeStruct((depth, n_rows, d), BF16),
        scratch_shapes=[pltpu.VMEM((2, tn, d), F32), pltpu.SemaphoreType.DMA((2,))],
        compiler_params=_params(("arbitrary", "arbitrary")),
        name="wprep",
    )(w_ndk)


def _store_heads(ref, acc):
    n_heads = ref.shape[-3]
    for hd in range(n_heads):
        part = acc[:, hd * HEAD_DIM:(hd + 1) * HEAD_DIM]
        if len(ref.shape) == 3:
            ref[hd] = part
        else:
            seq = ref.shape[2]
            for b in range(ref.shape[0]):
                ref[b, hd] = part[b * seq:(b + 1) * seq, :]


def _qkv_kernel(h_ref, w_ref, *refs, n_earlier):
    earlier = refs[:4 * n_earlier]
    qkv_ref = refs[4 * n_earlier]
    kv_refs = refs[4 * n_earlier + 1:]
    j = pl.program_id(1)
    acc = _dot_nt(h_ref[...], w_ref[...])
    @pl.when(jnp.logical_or(j == 0, j == 3))
    def _():
        qkv_ref[...] = (acc * QK_SCALE).astype(qkv_ref.dtype)

    for kind, idx in enumerate((1, 2, 4, 5)):
        @pl.when(j == idx)
        def _(kind=kind):
            qkv_ref[...] = acc.astype(qkv_ref.dtype)
            ref = kv_refs[kind]
            if n_earlier:
                for p in range(n_earlier):
                    ref[p] = earlier[4 * p + kind][...]
                ref = ref.at[n_earlier]
            _store_heads(ref, acc)


def _qkv_proj(h, wt, first_row, tm, layer, batch, n_heads, earlier=()):
    m, d = h.shape
    seq = m // batch
    tn = n_heads * HEAD_DIM
    assert first_row % tn == 0
    blk0 = first_row // tn
    if m // tm == 1:
        block, index = (batch, n_heads, seq, HEAD_DIM), lambda i, j: (0, 0, 0, 0)
    else:
        assert batch == 1
        block, index = (None, n_heads, tm, HEAD_DIM), lambda i, j: (0, 0, i, 0)
    kv_in_spec = pl.BlockSpec(block, index)
    kv_shape = (batch, n_heads, seq, HEAD_DIM)
    n_earlier = len(earlier)
    if n_earlier:
        kv_out_spec = pl.BlockSpec((n_earlier + 1,) + block, lambda i, j: (0,) + index(i, j))
        kv_shape = (n_earlier + 1,) + kv_shape
    else:
        kv_out_spec = kv_in_spec
    return pl.pallas_call(
        functools.partial(_qkv_kernel, n_earlier=n_earlier),
        grid=(m // tm, 6),
        in_specs=[pl.BlockSpec((tm, d), lambda i, j: (i, 0)),
                  pl.BlockSpec((None, tn, d), lambda i, j: (layer, blk0 + j, 0))]
        + [kv_in_spec] * (4 * n_earlier),
        out_specs=[pl.BlockSpec((tm, tn), lambda i, j: (i, j))] + [kv_out_spec] * 4,
        out_shape=[jax.ShapeDtypeStruct((m, 6 * tn), BF16)] + [jax.ShapeDtypeStruct(kv_shape, F32)] * 4,
        compiler_params=_params(("arbitrary", "arbitrary")),
        name="qkv_proj",
    )(h, wt, *(a for kv in earlier for a in kv))


def _sb_tile(q, kb, vb, suffix, mask):
    z = _dot_nt(q, kb)
    sp = _softplus2(z)
    if mask is not None:
        sp = jnp.where(mask, sp, 0.0)
    s_incl = _dot(sp.astype(BF16), suffix)
    w = jnp.exp2(z - s_incl)
    if mask is not None:
        w = jnp.where(mask, w, 0.0)
    return _dot(w.astype(BF16), vb), s_incl[:, 0:1]


def _sb_live(r_ref):
    r_min = jnp.min(r_ref[...], axis=0, keepdims=True)
    return (jnp.max(jnp.exp2(-r_min)) > 0.0).astype(jnp.int32)


def _sb_accumulate(acc_ref, r_ref, pv, tot):
    r = r_ref[...]
    acc_ref[...] += jnp.exp2(-r) * pv
    r_ref[...] = r + tot


def _for_past_tiles(n_tiles, pair_fn, single_fn, live_fn=None):
    if live_fn is None:
        def body(jj, c):
            pair_fn(n_tiles - 1 - 2 * jj)
            return c

        lax.fori_loop(0, n_tiles // 2, body, 0)
        live = True
    else:
        def step(c):
            pair_fn(n_tiles - 1 - 2 * c[0])
            return c[0] + 1, live_fn()

        _, live = lax.while_loop(lambda c: jnp.logical_and(c[0] < n_tiles // 2, c[1] > 0), step,
                                 (jnp.int32(0), live_fn()))
        live = live > 0
    odd = n_tiles % 2 == 1
    if isinstance(odd, bool) and live is True:
        if odd:
            single_fn(0)
    else:
        pl.when(jnp.logical_and(odd, live))(lambda: single_fn(0))


def _sb_prompt_kernel(q_ref, k_ref, v_ref, o_ref, acc_ref, r_ref, *, tq):
    qi = pl.program_id(1)
    n_blocks = acc_ref.shape[0]
    acc_ref[...] = jnp.zeros_like(acc_ref)
    r_ref[...] = jnp.zeros_like(r_ref)
    suffix = _suffix_incl(tq)
    row, col = _iota2(tq, tq)

    def kv(j):
        st = pl.multiple_of(j * tq, tq)
        return k_ref[pl.ds(st, tq), :], v_ref[pl.ds(st, tq), :]

    def block(s):
        q = q_ref[s * tq:(s + 1) * tq, :]
        acc, r = acc_ref.at[s], r_ref.at[s]

        def single(j, mask=None):
            _sb_accumulate(acc, r, *_sb_tile(q, *kv(j), suffix, mask))

        def pair(j, mask=None):
            pv_a, tot_a = _sb_tile(q, *kv(j), suffix, mask)
            pv_b, tot_b = _sb_tile(q, *kv(j - 1), suffix, None)
            _sb_accumulate(acc, r, pv_a + jnp.exp2(-tot_a) * pv_b, tot_a + tot_b)

        return single, pair, functools.partial(_sb_live, r)

    blocks = [block(s) for s in range(n_blocks)]
    diag = [n_blocks * qi + s for s in range(n_blocks)]

    @pl.when(qi == 0)
    def _():
        blocks[0][0](diag[0], col < row)
        for (_, pair, _), g in zip(blocks[1:], diag[1:]):
            pair(g, col < row)
        for s, (single, pair, live) in enumerate(blocks):
            if s >= 2:
                _for_past_tiles(s - 1, pair, single, live)

    @pl.when(qi > 0)
    def _():
        for (_, pair, _), g in zip(blocks, diag):
            pair(g, col < row)
        for (single, pair, live), g in zip(blocks, diag):
            _for_past_tiles(g - 1, pair, single, live)

    for s in range(n_blocks):
        o_ref[s * tq:(s + 1) * tq, :] = acc_ref[s].astype(o_ref.dtype)


def _sb_prompt(qkv, n_heads, tq, n_blocks):
    t = qkv.shape[0]
    rows = n_blocks * tq
    return pl.pallas_call(
        functools.partial(_sb_prompt_kernel, tq=tq),
        grid=(n_heads, t // rows),
        in_specs=[
            pl.BlockSpec((rows, HEAD_DIM), lambda h, i: (i, h)),
            pl.BlockSpec((t, HEAD_DIM), lambda h, i: (0, n_heads + h)),
            pl.BlockSpec((t, HEAD_DIM), lambda h, i: (0, 2 * n_heads + h)),
        ],
        out_specs=pl.BlockSpec((rows, HEAD_DIM), lambda h, i: (i, h)),
        out_shape=jax.ShapeDtypeStruct((t, n_heads * HEAD_DIM), BF16),
        scratch_shapes=[pltpu.VMEM((n_blocks, tq, HEAD_DIM), F32), pltpu.VMEM((n_blocks, tq, HEAD_DIM), F32)],
        compiler_params=_params(("arbitrary", "arbitrary")),
        name="sb_prompt",
    )(qkv, qkv, qkv)


def _sb_sample_kernel(q_ref, kn_ref, vn_ref, kc_ref, vc_ref, o_ref, acc_ref, r_ref, *, tk):
    group = kc_ref.shape[0]
    seq = q_ref.shape[0]
    n_tiles = kc_ref.shape[1] // tk
    acc_ref[...] = jnp.zeros_like(acc_ref)
    r_ref[...] = jnp.zeros_like(r_ref)
    row, col = _iota2(seq, seq)
    suffix_new = _suffix_incl(seq)
    suffix = _suffix_incl(tk)

    def head(s):
        cols = slice(s * HEAD_DIM, (s + 1) * HEAD_DIM)
        q = q_ref[:, cols]
        acc, r = acc_ref.at[s], r_ref.at[s]

        def kv(j):
            st = pl.multiple_of(j * tk, tk)
            return kc_ref[s, pl.ds(st, tk), :].astype(BF16), vc_ref[s, pl.ds(st, tk), :].astype(BF16)

        def new_rows():
            _sb_accumulate(acc, r, *_sb_tile(q, kn_ref[:, cols], vn_ref[:, cols], suffix_new, col < row))

        def single(j):
            _sb_accumulate(acc, r, *_sb_tile(q, *kv(j), suffix, None))

        def pair(j):
            pv_a, tot_a = _sb_tile(q, *kv(j), suffix, None)
            pv_b, tot_b = _sb_tile(q, *kv(j - 1), suffix, None)
            _sb_accumulate(acc, r, pv_a + jnp.exp2(-tot_a) * pv_b, tot_a + tot_b)

        return new_rows, single, pair, functools.partial(_sb_live, r)

    heads = [head(s) for s in range(group)]
    first = 2 if n_tiles >= 2 else n_tiles
    for new_rows, single, pair, _ in heads:
        new_rows()
        if first == 2:
            pair(n_tiles - 1)
        elif first == 1:
            single(0)
    for _, single, pair, live in heads:
        _for_past_tiles(n_tiles - first, pair, single, live)
    for s in range(group):
        o_ref[:, s * HEAD_DIM:(s + 1) * HEAD_DIM] = acc_ref[s].astype(o_ref.dtype)


def _sample_specs(layer, n_heads, group, seq, past, first_col):
    width = group * HEAD_DIM
    per_part = n_heads // group
    qkv = [pl.BlockSpec((seq, width), lambda b, h, k=k: (b, (first_col + k) * per_part + h)) for k in range(3)]
    cache = pl.BlockSpec((None, None, group, past, HEAD_DIM), lambda b, h: (layer, b, h, 0, 0))
    return qkv + [cache, cache]


def _sb_sample(qkv, cache_k, cache_v, layer, seq, tk, group):
    n = qkv.shape[0]
    nb = n // seq
    _, _, n_heads, past, _ = cache_k.shape
    assert n_heads % group == 0
    return pl.pallas_call(
        functools.partial(_sb_sample_kernel, tk=tk),
        grid=(nb, n_heads // group),
        in_specs=_sample_specs(layer, n_heads, group, seq, past, 0),
        out_specs=pl.BlockSpec((seq, group * HEAD_DIM), lambda b, h: (b, h)),
        out_shape=jax.ShapeDtypeStruct((n, n_heads * HEAD_DIM), BF16),
        scratch_shapes=[pltpu.VMEM((group, seq, HEAD_DIM), F32), pltpu.VMEM((group, seq, HEAD_DIM), F32)],
        compiler_params=_params(("arbitrary", "arbitrary")),
        name="sb_sample",
    )(qkv, qkv, qkv, cache_k, cache_v)


def _fox_init(m_ref, l_ref, acc_ref):
    m_ref[...] = jnp.full(m_ref.shape, NEG_BIG, F32)
    l_ref[...] = jnp.zeros_like(l_ref)
    acc_ref[...] = jnp.zeros_like(acc_ref)


def _fox_tile_t(qt, kb, vt, bias, mask, m_ref, l_ref, acc_ref):
    s = _dot(kb, qt) + jnp.concatenate([bias] * (qt.shape[1] // LANES), axis=1)
    if mask is not None:
        s = jnp.where(mask, s, NEG_BIG)
    m_prev = m_ref[...]
    m_new = jnp.maximum(m_prev, jnp.max(s, axis=0, keepdims=True))
    alpha = jnp.exp2(m_prev - m_new)
    p = jnp.exp2(s - m_new)
    l_ref[...] = alpha * l_ref[...] + jnp.sum(p, axis=0, keepdims=True)
    acc_ref[...] = alpha * acc_ref[...] + _dot(vt, p.astype(BF16))
    m_ref[...] = m_new


def _fox_prompt_kernel(q_ref, k_ref, v_ref, ck_ref, o_ref,
                       vt_ref, nb_ref, kmax_ref, edge_ref, m_ref, l_ref, acc_ref, *, tq, chunk):
    h = pl.program_id(0)
    qi = pl.program_id(1)
    t = k_ref.shape[0]
    reps = tq // LANES
    n_blocks = acc_ref.shape[0]

    @pl.when(qi == 0)
    def _():
        sel_r, _ = _iota2(LANES, LANES)
        sel = jnp.where(sel_r == h, 1.0, 0.0).astype(BF16)
        kmax_ref[...] = jnp.zeros_like(kmax_ref)

        def body(c, carry):
            st = pl.multiple_of(c * chunk, chunk)
            vt_ref[:, pl.ds(st, chunk)] = v_ref[pl.ds(st, chunk), :].astype(F32).T.astype(BF16)
            ck_h = sum(_dot(p, sel) for p in _split_bf16(ck_ref[pl.ds(st, chunk), :], 3))
            nb_ref[pl.ds(st, chunk), :] = ck_h * (-LOG2E)
            kf = k_ref[pl.ds(st, chunk), :].astype(F32)
            kmax_ref[...] = jnp.maximum(kmax_ref[...], jnp.max(jnp.sum(kf * kf, axis=1, keepdims=True)))
            return carry

        lax.fori_loop(0, t // chunk, body, 0)

    def block(s):
        g = n_blocks * qi + s
        qtf = q_ref[s * tq:(s + 1) * tq, :].astype(F32).T
        qt = qtf.astype(BF16)
        m, l, acc, edge = m_ref.at[s], l_ref.at[s], acc_ref.at[s], edge_ref.at[s]
        _fox_init(m, l, acc)
        qs = pl.multiple_of(g * tq, tq)
        c0 = nb_ref[pl.ds(qs, 1), :]
        edge[...] = c0

        def tile(st, tk, mask):
            bias = nb_ref[pl.ds(st, tk), :] - c0
            _fox_tile_t(qt, k_ref[pl.ds(st, tk), :], vt_ref[:, pl.ds(st, tk)], bias, mask, m, l, acc)
            edge[...] = nb_ref[pl.ds(st, 1), :]

        q_norm2 = jnp.sum(qtf * qtf, axis=0, keepdims=True)
        reach = jnp.sqrt(q_norm2 * jnp.concatenate([kmax_ref[...]] * reps, axis=1)) * 1.001

        def live():
            left = jnp.concatenate([edge[...] - c0] * reps, axis=1)
            return (jnp.max(reach + left - m[...]) > -FOX_DEAD).astype(jnp.int32)

        def first(with_left):
            if with_left:
                krow, qcol = _iota2(2 * tq, tq)
                tile(pl.multiple_of(qs - tq, tq), 2 * tq, krow - tq <= qcol)
            else:
                krow, qcol = _iota2(tq, tq)
                tile(qs, tq, krow <= qcol)

        def rest():
            _for_past_tiles(g - 1,
                            lambda j: tile(pl.multiple_of((j - 1) * tq, tq), 2 * tq, None),
                            lambda j: tile(pl.multiple_of(j * tq, tq), tq, None),
                            live)

        return first, rest

    blocks = [block(s) for s in range(n_blocks)]

    @pl.when(qi == 0)
    def _():
        for s, (first, _) in enumerate(blocks):
            first(s > 0)
        for _, rest in blocks[2:]:
            rest()

    @pl.when(qi > 0)
    def _():
        for first, _ in blocks:
            first(True)
        for _, rest in blocks:
            rest()

    for s in range(n_blocks):
        o_ref[s * tq:(s + 1) * tq, :] = (acc_ref[s] / l_ref[s]).T.astype(o_ref.dtype)


def _fox_prompt(qkv, ck, n_heads, tq, n_blocks):
    t = qkv.shape[0]
    rows = n_blocks * tq
    return pl.pallas_call(
        functools.partial(_fox_prompt_kernel, tq=tq, chunk=512),
        grid=(n_heads, t // rows),
        in_specs=[
            pl.BlockSpec((rows, HEAD_DIM), lambda h, i: (i, 3 * n_heads + h)),
            pl.BlockSpec((t, HEAD_DIM), lambda h, i: (0, 4 * n_heads + h)),
            pl.BlockSpec((t, HEAD_DIM), lambda h, i: (0, 5 * n_heads + h)),
            pl.BlockSpec((t, LANES), lambda h, i: (0, 0)),
        ],
        out_specs=pl.BlockSpec((rows, HEAD_DIM), lambda h, i: (i, h)),
        out_shape=jax.ShapeDtypeStruct((t, n_heads * HEAD_DIM), BF16),
        scratch_shapes=[pltpu.VMEM((HEAD_DIM, t), BF16), pltpu.VMEM((t, LANES), F32),
                        pltpu.VMEM((1, LANES), F32), pltpu.VMEM((n_blocks, 1, LANES), F32),
                        pltpu.VMEM((n_blocks, 1, tq), F32), pltpu.VMEM((n_blocks, 1, tq), F32),
                        pltpu.VMEM((n_blocks, HEAD_DIM, tq), F32)],
        compiler_params=_params(("arbitrary", "arbitrary")),
        name="fox_prompt",
    )(qkv, qkv, qkv, ck)


def _fox_tile(q, kb, vb, bias, mask, m_ref, l_ref, acc_ref):
    s = _dot_nt(q, kb) + bias
    if mask is not None:
        s = jnp.where(mask, s, NEG_BIG)
    m_prev = m_ref[...]
    m_new = jnp.maximum(m_prev, jnp.max(s, axis=1, keepdims=True))
    alpha = jnp.exp2(m_prev - m_new)
    p = jnp.exp2(s - m_new)
    l_ref[...] = alpha * l_ref[...] + jnp.sum(p, axis=1, keepdims=True)
    acc_ref[...] = alpha * acc_ref[...] + _dot(p.astype(BF16), vb)
    m_ref[...] = m_new


def _fox_sample_kernel(q_ref, kn_ref, vn_ref, kc_ref, vc_ref, ckp_ref, nmax_ref, ckl_ref, o_ref,
                       m_ref, l_ref, acc_ref, *, tk):
    group = kc_ref.shape[0]
    seq = q_ref.shape[0]
    past = kc_ref.shape[1]
    n_tiles = past // tk
    _fox_init(m_ref, l_ref, acc_ref)
    row, col = _iota2(seq, seq)

    def head(s):
        cols = slice(s * HEAD_DIM, (s + 1) * HEAD_DIM)
        q = q_ref[:, cols]
        m, l, acc = m_ref.at[s], l_ref.at[s], acc_ref.at[s]
        c0 = ckp_ref[s, :, past - 1:past]
        qf = q.astype(F32)
        k_abs = jnp.max(jnp.abs(kc_ref[s]))
        reach = jnp.sqrt(jnp.sum(qf * qf, axis=1, keepdims=True) * HEAD_DIM) * (k_abs * 1.01)

        def new_rows():
            _fox_tile(q, kn_ref[:, cols], vn_ref[:, cols], ckl_ref[s] * (-LOG2E), col <= row, m, l, acc)

        def tile(j):
            st = pl.multiple_of(j * tk, tk)
            kb = kc_ref[s, pl.ds(st, tk), :].astype(BF16)
            vb = vc_ref[s, pl.ds(st, tk), :].astype(BF16)
            bias = (c0 - ckp_ref[s, :, pl.ds(st, tk)]) * LOG2E
            _fox_tile(q, kb, vb, bias, None, m, l, acc)
            lo = pl.multiple_of(jnp.maximum(st - LANES, 0), LANES)
            left = nmax_ref[s, :, pl.ds(lo, LANES)][:, LANES - 1:LANES]
            return (jnp.max(reach + (c0 + left) * LOG2E - m[...]) > -FOX_DEAD).astype(jnp.int32)

        return new_rows, tile

    heads = [head(s) for s in range(group)]
    live = []
    for new_rows, tile in heads:
        new_rows()
        live.append(tile(n_tiles - 1))
    for (_, tile), live0 in zip(heads, live):
        lax.while_loop(lambda c: jnp.logical_and(c[0] >= 0, c[1] > 0),
                       lambda c, tile=tile: (c[0] - 1, tile(c[0])),
                       (jnp.int32(n_tiles - 2), live0))
    for s in range(group):
        o_ref[:, s * HEAD_DIM:(s + 1) * HEAD_DIM] = (acc_ref[s] / l_ref[s]).astype(o_ref.dtype)


def _fox_sample(qkv, cache_k, cache_v, ck_past, nmax_past, ck_local, layer, seq, tk, group):
    n = qkv.shape[0]
    nb = n // seq
    _, _, n_heads, past, _ = cache_k.shape
    assert n_heads % group == 0
    return pl.pallas_call(
        functools.partial(_fox_sample_kernel, tk=tk),
        grid=(nb, n_heads // group),
        in_specs=_sample_specs(layer, n_heads, group, seq, past, 3) + [
            pl.BlockSpec((group, None, 1, past), lambda b, h: (h, b, 0, 0)),
            pl.BlockSpec((group, None, 1, past), lambda b, h: (h, b, 0, 0)),
            pl.BlockSpec((group, None, 1, seq), lambda b, h: (h, b, 0, 0)),
        ],
        out_specs=pl.BlockSpec((seq, group * HEAD_DIM), lambda b, h: (b, h)),
        out_shape=jax.ShapeDtypeStruct((n, n_heads * HEAD_DIM), BF16),
        scratch_shapes=[pltpu.VMEM((group, seq, 1), F32), pltpu.VMEM((group, seq, 1), F32),
                        pltpu.VMEM((group, seq, HEAD_DIM), F32)],
        compiler_params=_params(("arbitrary", "arbitrary")),
        name="fox_sample",
    )(qkv, qkv, qkv, cache_k, cache_v, ck_past, nmax_past, ck_local)


def _out_proj_kernel(ya_ref, ysb_ref, yfox_ref, w_ref, x_ref, g_ref, xo_ref, h_ref):
    c = ya_ref.shape[1]
    s = ysb_ref.shape[1]
    acc = _dot(ya_ref[...], w_ref[0:c, :])
    acc += _dot(ysb_ref[...], w_ref[c:c + s, :])
    acc += _dot(yfox_ref[...], w_ref[c + s:, :])
    x = x_ref[...] + acc
    xo_ref[...] = x
    h_ref[...] = _rmsnorm_f32(x, g_ref[...]).astype(h_ref.dtype)


def _out_proj(ya, ysb, yfox, w, layer, x, g, tm):
    m, d = x.shape
    row = lambda width: pl.BlockSpec((tm, width), lambda i: (i, 0))
    return pl.pallas_call(
        _out_proj_kernel,
        grid=(m // tm,),
        in_specs=[row(ya.shape[1]), row(ysb.shape[1]), row(yfox.shape[1]),
                  pl.BlockSpec((None,) + w.shape[1:], lambda i: (layer, 0, 0)), row(d),
                  pl.BlockSpec((1, d), lambda i: (0, 0))],
        out_specs=[row(d), row(d)],
        out_shape=[jax.ShapeDtypeStruct((m, d), F32), jax.ShapeDtypeStruct((m, d), BF16)],
        compiler_params=_params(("arbitrary",)),
        name="out_proj",
    )(ya, ysb, yfox, w, x, g.reshape(1, d))


def _ffn_kernel(h_ref, wg_ref, wu_ref, wd_ref, x_ref, g_ref, *rest, emit_x):
    out_refs, acc_ref = rest[:-1], rest[-1]
    f = pl.program_id(1)

    @pl.when(f == 0)
    def _():
        acc_ref[...] = jnp.zeros_like(acc_ref)

    h = h_ref[...]
    gate = _dot(h, wg_ref[...])
    up = _dot(h, wu_ref[...])
    act = gate * jax.nn.sigmoid(gate) * up
    acc_ref[...] += _dot(act.astype(BF16), wd_ref[...])

    @pl.when(f == pl.num_programs(1) - 1)
    def _():
        x = x_ref[...] + acc_ref[...]
        normed = _rmsnorm_f32(x, g_ref[...])
        if emit_x:
            out_refs[0][...] = x
            out_refs[1][...] = normed.astype(out_refs[1].dtype)
        else:
            out_refs[0][...] = normed


def _ffn(h, wg, wu, wd, layer, x, g, tm, tf, emit_x):
    m, d = x.shape
    ff = wg.shape[2]
    row = pl.BlockSpec((tm, d), lambda i, f: (i, 0))
    if emit_x:
        out_specs = [row, row]
        out_shape = [jax.ShapeDtypeStruct((m, d), F32), jax.ShapeDtypeStruct((m, d), BF16)]
    else:
        out_specs = [row]
        out_shape = [jax.ShapeDtypeStruct((m, d), F32)]
    return pl.pallas_call(
        functools.partial(_ffn_kernel, emit_x=emit_x),
        grid=(m // tm, ff // tf),
        in_specs=[row,
                  pl.BlockSpec((None, d, tf), lambda i, f: (layer, 0, f)),
                  pl.BlockSpec((None, d, tf), lambda i, f: (layer, 0, f)),
                  pl.BlockSpec((None, tf, d), lambda i, f: (layer, f, 0)),
                  row,
                  pl.BlockSpec((1, d), lambda i, f: (0, 0))],
        out_specs=out_specs,
        out_shape=out_shape,
        scratch_shapes=[pltpu.VMEM((tm, d), F32)],
        compiler_params=_params(("arbitrary", "arbitrary")),
        name="ffn",
    )(h, wg, wu, wd, x, g.reshape(1, d))


def kernel(x_prompt, x_sample, state_conv, cache_sb_k, cache_sb_v, cache_fox_k, cache_fox_v, cache_fox_logf,
           norm1_g, w_in, b_f, conv_w, w_out, norm2_g, w_gate, w_up, w_down, final_g):
    depth = w_in.shape[0]
    bp, t, d = x_prompt.shape
    nb, seq, _ = x_sample.shape
    assert bp == 1, "prompt kernels carry conv rows and forget sums across row tiles of one stream"
    c = conv_w.shape[2]
    n_sb = cache_sb_k.shape[3]
    n_fox = cache_fox_k.shape[3]
    assert n_sb == n_fox and n_fox <= 8
    past = cache_sb_k.shape[2]
    sb_dim = n_sb * HEAD_DIM
    qkv0 = 3 * c
    qkv1 = qkv0 + 3 * sb_dim + 3 * n_fox * HEAD_DIM
    n_s = nb * seq
    per_head = lambda a: jnp.transpose(a, (0, 1, 3, 2, 4))
    csk, csv, cfk, cfv = (per_head(a) for a in (cache_sb_k, cache_sb_v, cache_fox_k, cache_fox_v))
    w_ndk = jnp.transpose(w_in, (2, 0, 1))
    wt = _wprep(w_ndk, qkv1, 512)
    wo, wg, wu, wd = (w.astype(BF16) for w in (w_out, w_gate, w_up, w_down))

    xp = x_prompt.reshape(t, d)
    xs = x_sample.reshape(n_s, d)
    hp = _rmsnorm(xp, norm1_g[0], 512)
    hs = _rmsnorm(xs, norm1_g[0], n_s)
    p_small, s_small = [], []
    p_kv, s_kv = [], []
    for l in range(depth):
        bf_row = jnp.pad(b_f[l], (0, LANES - n_fox)).reshape(1, LANES)
        bf_col = jnp.pad(b_f[l], (0, 8 - n_fox)).reshape(8, 1)
        last = l == depth - 1
        g_next = final_g if last else norm1_g[l + 1]

        ya, conv_p, logf_p, ck_p = _mix_a_prompt(hp, wt, w_ndk, qkv1, l, bf_row, conv_w[l], 512)
        stacking = last and l > 0
        qkv, *kv = _qkv_proj(hp, wt, qkv0, 512 if stacking else 1024, l, bp, n_sb, p_kv if stacking else ())
        p_kv.append(kv)
        ysb = _sb_prompt(qkv, n_sb, 256, 4)
        yfox = _fox_prompt(qkv, ck_p, n_fox, 256, 4)
        xp, h2 = _out_proj(ya, ysb, yfox, wo, l, xp, norm2_g[l], 512)
        outs = _ffn(h2, wg, wu, wd, l, xp, g_next, 512, 512, not last)
        if last:
            y_prompt = outs[0]
        else:
            xp, hp = outs
        p_small.append((conv_p, logf_p[:, :n_fox]))

        logf_past = jnp.transpose(cache_fox_logf[l], (2, 0, 1)).reshape(n_fox * nb, past)
        ck_past, nmax_past = (a.reshape(n_fox, nb, 1, past) for a in _cumsum_rows(logf_past))
        ya, conv_s, logf_s, ckl = _mix_a_sample(hs, wt, w_ndk, qkv1, l, bf_col, conv_w[l],
                                                state_conv[l].reshape(nb * (CONV_W - 1), c), seq)
        ckl = ckl[:n_fox].reshape(n_fox, nb, 1, seq)
        qkv, *kv = _qkv_proj(hs, wt, qkv0, n_s, l, nb, n_sb, s_kv if stacking else ())
        s_kv.append(kv)
        ysb = _sb_sample(qkv, csk, csv, l, seq, 256, 3)
        yfox = _fox_sample(qkv, cfk, cfv, ck_past, nmax_past, ckl, l, seq, 512, 3)
        xs, h2 = _out_proj(ya, ysb, yfox, wo, l, xs, norm2_g[l], n_s)
        outs = _ffn(h2, wg, wu, wd, l, xs, g_next, n_s, 512, not last)
        if last:
            y_sample = outs[0]
        else:
            xs, hs = outs
        logf_s = jnp.transpose(logf_s[:n_fox].reshape(n_fox, nb, seq), (1, 2, 0))
        s_small.append((conv_s, logf_s))

    stack = lambda news, i, shape: jnp.stack([n[i] for n in news]).reshape((depth,) + shape)
    stack_kv = lambda news: [per_head(a if depth > 1 else a[None]) for a in news[-1]]
    return (
        y_prompt.reshape(bp, t, d),
        y_sample.reshape(nb, seq, d),
        stack(p_small, 0, (bp, CONV_W - 1, c)),
        *stack_kv(p_kv),
        stack(p_small, 1, (bp, t, n_fox)),
        stack(s_small, 0, (nb, CONV_W - 1, c)),
        *stack_kv(s_kv),
        stack(s_small, 1, (nb, seq, n_fox)),
    )
```

```python
import functools

import jax
import jax.numpy as jnp
from jax import lax
from jax.experimental import pallas as pl
from jax.experimental.pallas import tpu as pltpu

F32 = jnp.float32
BF16 = jnp.bfloat16

NORM_EPS = 1e-5
HEAD_DIM = 128
CONV_W = 3
LANES = 128
VMEM_LIMIT = 56 * 1024 * 1024
NEG_BIG = -1e30
FOX_DEAD = 160.0
LOG2E = 1.4426950408889634
QK_SCALE = HEAD_DIM ** -0.5 * LOG2E

ROW_TILE = 512
QKV_ROW_TILE = 1024
QKV_STACK_ROW_TILE = 512
FF_TILE = 512
W_PREP_TILE = 512
ATTN_TILE = 2 * LANES
ATTN_BLOCKS = 8
DECODE_SB_TILE = 256
DECODE_FOX_TILE = 512
DECODE_HEADS = 3


def _params(sem, vmem=VMEM_LIMIT):
    return pltpu.CompilerParams(dimension_semantics=sem, vmem_limit_bytes=vmem)


def _dot(a, b):
    return jnp.dot(a, b, preferred_element_type=F32)


def _dot_nt(a, b):
    return lax.dot_general(a, b, (((1,), (1,)), ((), ())), preferred_element_type=F32)


def _softplus2(z2):
    return jnp.maximum(z2, 0.0) + jnp.log(1.0 + jnp.exp2(-jnp.abs(z2))) * LOG2E


def _log_sigmoid(x):
    return jnp.minimum(x, 0.0) - jnp.log(1.0 + jnp.exp(-jnp.abs(x)))


def _split_bf16(x, parts):
    out = []
    rem = x
    for p in range(parts):
        hi = rem.astype(BF16)
        out.append(hi)
        if p + 1 < parts:
            rem = rem - hi.astype(F32)
    return out


def _iota2(n, m):
    return lax.broadcasted_iota(jnp.int32, (n, m), 0), lax.broadcasted_iota(jnp.int32, (n, m), 1)


def _tri_incl(n, seg=None):
    j, s = _iota2(n, n)
    t = jnp.where(j <= s, 1.0, 0.0)
    if seg is not None:
        t = jnp.where(j // seg == s // seg, t, 0.0)
    return t.astype(BF16)


def _suffix_incl(n):
    j, s = _iota2(n, n)
    return jnp.where(j >= s, 1.0, 0.0).astype(BF16)


def _prefix_sum_lanes(x, tri, carry):
    outs = []
    for c in range(x.shape[1] // LANES):
        xc = x[:, c * LANES:(c + 1) * LANES]
        loc = sum(_dot(p, tri) for p in _split_bf16(xc, 3))
        oc = loc if carry is None else loc + carry
        outs.append(oc)
        if carry is not None:
            carry = oc[:, LANES - 1:LANES]
    return jnp.concatenate(outs, axis=1), carry


def _rmsnorm_f32(x, g):
    ms = jnp.mean(x * x, axis=-1, keepdims=True)
    return x * lax.rsqrt(ms + NORM_EPS) * g


def _rmsnorm_kernel(x_ref, g_ref, o_ref):
    o_ref[...] = _rmsnorm_f32(x_ref[...], g_ref[...]).astype(o_ref.dtype)


def _rmsnorm(x, g, tm):
    m, d = x.shape
    return pl.pallas_call(
        _rmsnorm_kernel,
        grid=(m // tm,),
        in_specs=[pl.BlockSpec((tm, d), lambda i: (i, 0)), pl.BlockSpec((1, d), lambda i: (0, 0))],
        out_specs=pl.BlockSpec((tm, d), lambda i: (i, 0)),
        out_shape=jax.ShapeDtypeStruct((m, d), BF16),
        compiler_params=_params(("arbitrary",)),
        name="rmsnorm",
    )(x, g.reshape(1, d))


def _gated_conv(proj, cw, u_m1, u_m2):
    c = proj.shape[1] // 3
    g_b, g_c, hc = proj[:, :c], proj[:, c:2 * c], proj[:, 2 * c:]
    u = g_c * hc
    r2 = pltpu.roll(u, 2, 0)
    u1 = u_m1(pltpu.roll(u, 1, 0))
    u2 = u_m2(r2)
    conv = cw[0:1, :] * u2 + cw[1:2, :] * u1 + cw[2:3, :] * u
    return g_b * conv, u, r2


def _mix_a_prompt_kernel(h_ref, wa_ref, wtail_ref, bf_ref, cw_ref,
                         ya_ref, conv_ref, logf_ref, ck_ref, uprev_ref, carry_ref, wrows_ref, wf_ref, *, layer):
    i = pl.program_id(0)

    @pl.when(i == 0)
    def _():
        uprev_ref[...] = jnp.zeros_like(uprev_ref)
        carry_ref[...] = jnp.zeros_like(carry_ref)
        wrows_ref[...] = jnp.zeros_like(wrows_ref)
        wrows_ref[0:wtail_ref.shape[0], :] = wtail_ref[:, layer, :]
        wf_ref[...] = wrows_ref[...].T.astype(wf_ref.dtype)

    h = h_ref[...]
    proj = _dot_nt(h, wa_ref[...])
    tm = proj.shape[0]
    rows = lax.broadcasted_iota(jnp.int32, (tm, proj.shape[1] // 3), 0)
    prev = uprev_ref[...]
    m1 = lambda r1: jnp.where(rows == 0, prev[1:2, :], r1)
    m2 = lambda r2: jnp.where(rows == 0, prev[0:1, :], jnp.where(rows == 1, prev[1:2, :], r2))
    ya, _, r2 = _gated_conv(proj, cw_ref[...], m1, m2)
    ya_ref[...] = ya.astype(ya_ref.dtype)
    tail = r2[0:2, :]
    uprev_ref[0:2, :] = tail
    conv_ref[...] = tail

    logf = _log_sigmoid(_dot(h, wf_ref[...]) + bf_ref[...])
    logf_ref[...] = logf
    rows_per_sum = 2 * LANES
    lower = _suffix_incl(rows_per_sum)
    carry = carry_ref[...]
    for c in range(tm // rows_per_sum):
        part = logf[c * rows_per_sum:(c + 1) * rows_per_sum, :]
        ck = sum(_dot(lower, p) for p in _split_bf16(part, 3)) + carry
        ck_ref[c * rows_per_sum:(c + 1) * rows_per_sum, :] = ck
        carry = ck[rows_per_sum - 1:rows_per_sum, :]
    carry_ref[...] = carry


def _tail_spec(w_ndk, first_row):
    n, depth, d = w_ndk.shape
    assert first_row % (n - first_row) == 0
    return pl.BlockSpec((n - first_row, depth, d), lambda i: (first_row // (n - first_row), 0, 0))


def _mix_a_prompt(h, wt, w_ndk, tail_row, layer, bfp, cw, tm):
    t, d = h.shape
    c = cw.shape[1]
    c3 = 3 * c
    return pl.pallas_call(
        functools.partial(_mix_a_prompt_kernel, layer=layer),
        grid=(t // tm,),
        in_specs=[
            pl.BlockSpec((tm, d), lambda i: (i, 0)),
            pl.BlockSpec((None, c3, d), lambda i: (layer, 0, 0)),
            _tail_spec(w_ndk, tail_row),
            pl.BlockSpec((1, LANES), lambda i: (0, 0)),
            pl.BlockSpec((CONV_W, c), lambda i: (0, 0)),
        ],
        out_specs=[
            pl.BlockSpec((tm, c), lambda i: (i, 0)),
            pl.BlockSpec((CONV_W - 1, c), lambda i: (0, 0)),
            pl.BlockSpec((tm, LANES), lambda i: (i, 0)),
            pl.BlockSpec((tm, LANES), lambda i: (i, 0)),
        ],
        out_shape=[
            jax.ShapeDtypeStruct((t, c), BF16),
            jax.ShapeDtypeStruct((CONV_W - 1, c), F32),
            jax.ShapeDtypeStruct((t, LANES), F32),
            jax.ShapeDtypeStruct((t, LANES), F32),
        ],
        scratch_shapes=[pltpu.VMEM((8, c), F32), pltpu.VMEM((1, LANES), F32),
                        pltpu.VMEM((LANES, d), F32), pltpu.VMEM((d, LANES), BF16)],
        compiler_params=_params(("arbitrary",)),
        name="mix_a_prompt",
    )(h, wt, w_ndk, bfp, cw)


def _mix_a_sample_kernel(h_ref, wa_ref, wtail_ref, bf_ref, cw_ref, st_ref,
                         ya_ref, conv_ref, logf_ref, ckl_ref, u_ref, wrows_ref, *, seq, layer):
    h = h_ref[...]
    proj = _dot_nt(h, wa_ref[...])
    n = proj.shape[0]
    nb = n // seq
    rows = lax.broadcasted_iota(jnp.int32, (n, proj.shape[1] // 3), 0)

    def m1(r1):
        for b in range(nb):
            r1 = jnp.where(rows == b * seq, st_ref[2 * b + 1:2 * b + 2, :], r1)
        return r1

    def m2(r2):
        for b in range(nb):
            r2 = jnp.where(rows == b * seq, st_ref[2 * b:2 * b + 1, :], r2)
            r2 = jnp.where(rows == b * seq + 1, st_ref[2 * b + 1:2 * b + 2, :], r2)
        return r2

    ya, u, _ = _gated_conv(proj, cw_ref[...], m1, m2)
    ya_ref[...] = ya.astype(ya_ref.dtype)
    u_ref[...] = u
    for b in range(nb):
        conv_ref[2 * b:2 * b + 2, :] = u_ref[(b + 1) * seq - 2:(b + 1) * seq, :]

    wrows_ref[...] = jnp.zeros_like(wrows_ref)
    wrows_ref[0:wtail_ref.shape[0], :] = wtail_ref[:, layer, :]
    logit = _dot_nt(wrows_ref[...].astype(BF16), h)[0:8, :]
    logf = _log_sigmoid(logit + bf_ref[...])
    logf_ref[...] = logf
    ckl, _ = _prefix_sum_lanes(logf, _tri_incl(LANES, seg=seq), None)
    ckl_ref[...] = ckl


def _mix_a_sample(h, wt, w_ndk, tail_row, layer, bfp, cw, state, seq):
    n, d = h.shape
    c = cw.shape[1]
    c3 = 3 * c
    nb = n // seq
    full = lambda shape: pl.BlockSpec(shape, lambda i: tuple(0 for _ in shape))
    return pl.pallas_call(
        functools.partial(_mix_a_sample_kernel, seq=seq, layer=layer),
        grid=(1,),
        in_specs=[full((n, d)), pl.BlockSpec((None, c3, d), lambda i: (layer, 0, 0)),
                  _tail_spec(w_ndk, tail_row), full((8, 1)), full((CONV_W, c)),
                  full((nb * (CONV_W - 1), c))],
        out_specs=[full((n, c)), full((nb * (CONV_W - 1), c)), full((8, n)), full((8, n))],
        out_shape=[
            jax.ShapeDtypeStruct((n, c), BF16),
            jax.ShapeDtypeStruct((nb * (CONV_W - 1), c), F32),
            jax.ShapeDtypeStruct((8, n), F32),
            jax.ShapeDtypeStruct((8, n), F32),
        ],
        scratch_shapes=[pltpu.VMEM((n, c), F32), pltpu.VMEM((16, d), F32)],
        compiler_params=_params(("arbitrary",)),
        name="mix_a_sample",
    )(h, wt, w_ndk, bfp, cw, state)


def _prefix_max_lanes(x):
    lane = lax.broadcasted_iota(jnp.int32, (x.shape[0], LANES), 1)
    outs, carry = [], None
    for c in range(x.shape[1] // LANES):
        xc = x[:, c * LANES:(c + 1) * LANES]
        shift = 1
        while shift < LANES:
            xc = jnp.maximum(xc, jnp.where(lane >= shift, pltpu.roll(xc, shift, 1), NEG_BIG))
            shift *= 2
        if carry is not None:
            xc = jnp.maximum(xc, carry)
        outs.append(xc)
        carry = xc[:, LANES - 1:LANES]
    return jnp.concatenate(outs, axis=1)


def _cumsum_rows_kernel(x_ref, ck_ref, nmax_ref):
    ck, _ = _prefix_sum_lanes(x_ref[...], _tri_incl(LANES), jnp.zeros((x_ref.shape[0], 1), F32))
    ck_ref[...] = ck
    nmax_ref[...] = _prefix_max_lanes(-ck)


def _cumsum_rows(x):
    spec = pl.BlockSpec(x.shape, lambda i: (0, 0))
    return pl.pallas_call(
        _cumsum_rows_kernel,
        grid=(1,),
        in_specs=[spec],
        out_specs=[spec, spec],
        out_shape=[jax.ShapeDtypeStruct(x.shape, F32)] * 2,
        compiler_params=_params(("arbitrary",)),
        name="cumsum_rows",
    )(x)


def _wprep_kernel(w_hbm, o_ref, buf_ref, sem_ref):
    depth = pl.num_programs(1)
    tn = buf_ref.shape[1]
    s = pl.program_id(0) * depth + pl.program_id(1)
    n_steps = pl.num_programs(0) * depth

    def fetch(step, slot):
        i, layer = step // depth, step % depth
        return pltpu.make_async_copy(w_hbm.at[pl.ds(i * tn, tn), layer, :], buf_ref.at[slot], sem_ref.at[slot])

    @pl.when(s == 0)
    def _():
        fetch(s, 0).start()

    @pl.when(s + 1 < n_steps)
    def _():
        fetch(s + 1, (s + 1) % 2).start()

    fetch(s, s % 2).wait()
    o_ref[...] = buf_ref[s % 2].astype(o_ref.dtype)


def _wprep(w_ndk, n_rows, tn):
    _, depth, d = w_ndk.shape
    return pl.pallas_call(
        _wprep_kernel,
        grid=(n_rows // tn, depth),
        in_specs=[pl.BlockSpec(memory_space=pl.ANY)],
        out_specs=pl.BlockSpec((None, tn, d), lambda i, l: (l, i, 0)),
        out_shape=jax.ShapeDtypeStruct((depth, n_rows, d), BF16),
        scratch_shapes=[pltpu.VMEM((2, tn, d), F32), pltpu.SemaphoreType.DMA((2,))],
        compiler_params=_params(("arbitrary", "arbitrary")),
        name="wprep",
    )(w_ndk)


def _store_heads(ref, acc):
    n_heads = ref.shape[-3]
    for hd in range(n_heads):
        part = acc[:, hd * HEAD_DIM:(hd + 1) * HEAD_DIM]
        if len(ref.shape) == 3:
            ref[hd] = part
        else:
            seq = ref.shape[2]
            for b in range(ref.shape[0]):
                ref[b, hd] = part[b * seq:(b + 1) * seq, :]


def _qkv_kernel(h_ref, w_ref, *refs, n_earlier):
    earlier = refs[:4 * n_earlier]
    qkv_ref = refs[4 * n_earlier]
    kv_refs = refs[4 * n_earlier + 1:]
    j = pl.program_id(1)
    acc = _dot_nt(h_ref[...], w_ref[...])
    @pl.when(jnp.logical_or(j == 0, j == 3))
    def _():
        qkv_ref[...] = (acc * QK_SCALE).astype(qkv_ref.dtype)

    for kind, idx in enumerate((1, 2, 4, 5)):
        @pl.when(j == idx)
        def _(kind=kind):
            qkv_ref[...] = acc.astype(qkv_ref.dtype)
            ref = kv_refs[kind]
            if n_earlier:
                for p in range(n_earlier):
                    ref[p] = earlier[4 * p + kind][...]
                ref = ref.at[n_earlier]
            _store_heads(ref, acc)


def _qkv_proj(h, wt, first_row, tm, layer, batch, n_heads, earlier=()):
    m, d = h.shape
    seq = m // batch
    tn = n_heads * HEAD_DIM
    assert first_row % tn == 0
    blk0 = first_row // tn
    if m // tm == 1:
        block, index = (batch, n_heads, seq, HEAD_DIM), lambda i, j: (0, 0, 0, 0)
    else:
        assert batch == 1
        block, index = (None, n_heads, tm, HEAD_DIM), lambda i, j: (0, 0, i, 0)
    kv_in_spec = pl.BlockSpec(block, index)
    kv_shape = (batch, n_heads, seq, HEAD_DIM)
    n_earlier = len(earlier)
    if n_earlier:
        kv_out_spec = pl.BlockSpec((n_earlier + 1,) + block, lambda i, j: (0,) + index(i, j))
        kv_shape = (n_earlier + 1,) + kv_shape
    else:
        kv_out_spec = kv_in_spec
    return pl.pallas_call(
        functools.partial(_qkv_kernel, n_earlier=n_earlier),
        grid=(m // tm, 6),
        in_specs=[pl.BlockSpec((tm, d), lambda i, j: (i, 0)),
                  pl.BlockSpec((None, tn, d), lambda i, j: (layer, blk0 + j, 0))]
        + [kv_in_spec] * (4 * n_earlier),
        out_specs=[pl.BlockSpec((tm, tn), lambda i, j: (i, j))] + [kv_out_spec] * 4,
        out_shape=[jax.ShapeDtypeStruct((m, 6 * tn), BF16)] + [jax.ShapeDtypeStruct(kv_shape, F32)] * 4,
        compiler_params=_params(("arbitrary", "arbitrary")),
        name="qkv_proj",
    )(h, wt, *(a for kv in earlier for a in kv))


def _sb_tile(q, kb, vb, suffix, mask):
    z = _dot_nt(q, kb)
    sp = _softplus2(z)
    if mask is not None:
        sp = jnp.where(mask, sp, 0.0)
    s_incl = _dot(sp.astype(BF16), suffix)
    w = jnp.exp2(z - s_incl)
    if mask is not None:
        w = jnp.where(mask, w, 0.0)
    return _dot(w.astype(BF16), vb), s_incl[:, 0:1]


def _sb_live(r_ref):
    r_min = jnp.min(r_ref[...], axis=0, keepdims=True)
    return (jnp.max(jnp.exp2(-r_min)) > 0.0).astype(jnp.int32)


def _sb_accumulate(acc_ref, r_ref, pv, tot):
    r = r_ref[...]
    acc_ref[...] += jnp.exp2(-r) * pv
    r_ref[...] = r + tot


def _for_past_tiles(n_tiles, pair_fn, single_fn, live_fn=None):
    if True:
        def body(jj, c):
            pair_fn(n_tiles - 1 - 2 * jj)
            return c

        lax.fori_loop(0, n_tiles // 2, body, 0)
        live = True
    else:
        def step(c):
            pair_fn(n_tiles - 1 - 2 * c[0])
            return c[0] + 1, live_fn()

        _, live = lax.while_loop(lambda c: jnp.logical_and(c[0] < n_tiles // 2, c[1] > 0), step,
                                 (jnp.int32(0), live_fn()))
        live = live > 0
    odd = n_tiles % 2 == 1
    if isinstance(odd, bool) and live is True:
        if odd:
            single_fn(0)
    else:
        pl.when(jnp.logical_and(odd, live))(lambda: single_fn(0))


def _sb_prompt_kernel(q_ref, k_ref, v_ref, o_ref, acc_ref, r_ref, *, tq):
    qi = pl.program_id(1)
    n_blocks = acc_ref.shape[0]
    acc_ref[...] = jnp.zeros_like(acc_ref)
    r_ref[...] = jnp.zeros_like(r_ref)
    suffix = _suffix_incl(tq)
    row, col = _iota2(tq, tq)

    def kv(j):
        st = pl.multiple_of(j * tq, tq)
        return k_ref[pl.ds(st, tq), :], v_ref[pl.ds(st, tq), :]

    def block(s):
        q = q_ref[s * tq:(s + 1) * tq, :]
        acc, r = acc_ref.at[s], r_ref.at[s]

        def single(j, mask=None):
            _sb_accumulate(acc, r, *_sb_tile(q, *kv(j), suffix, mask))

        def pair(j, mask=None):
            pv_a, tot_a = _sb_tile(q, *kv(j), suffix, mask)
            pv_b, tot_b = _sb_tile(q, *kv(j - 1), suffix, None)
            _sb_accumulate(acc, r, pv_a + jnp.exp2(-tot_a) * pv_b, tot_a + tot_b)

        return single, pair, functools.partial(_sb_live, r)

    blocks = [block(s) for s in range(n_blocks)]
    diag = [n_blocks * qi + s for s in range(n_blocks)]

    @pl.when(qi == 0)
    def _():
        blocks[0][0](diag[0], col < row)
        for (_, pair, _), g in zip(blocks[1:], diag[1:]):
            pair(g, col < row)
        for s, (single, pair, live) in enumerate(blocks):
            if s >= 2:
                _for_past_tiles(s - 1, pair, single, live)

    @pl.when(qi > 0)
    def _():
        for (_, pair, _), g in zip(blocks, diag):
            pair(g, col < row)
        for (single, pair, live), g in zip(blocks, diag):
            _for_past_tiles(g - 1, pair, single, live)

    for s in range(n_blocks):
        o_ref[s * tq:(s + 1) * tq, :] = acc_ref[s].astype(o_ref.dtype)


def _sb_prompt(qkv, n_heads, tq, n_blocks):
    t = qkv.shape[0]
    rows = n_blocks * tq
    return pl.pallas_call(
        functools.partial(_sb_prompt_kernel, tq=tq),
        grid=(n_heads, t // rows),
        in_specs=[
            pl.BlockSpec((rows, HEAD_DIM), lambda h, i: (i, h)),
            pl.BlockSpec((t, HEAD_DIM), lambda h, i: (0, n_heads + h)),
            pl.BlockSpec((t, HEAD_DIM), lambda h, i: (0, 2 * n_heads + h)),
        ],
        out_specs=pl.BlockSpec((rows, HEAD_DIM), lambda h, i: (i, h)),
        out_shape=jax.ShapeDtypeStruct((t, n_heads * HEAD_DIM), BF16),
        scratch_shapes=[pltpu.VMEM((n_blocks, tq, HEAD_DIM), F32), pltpu.VMEM((n_blocks, tq, HEAD_DIM), F32)],
        compiler_params=_params(("arbitrary", "arbitrary")),
        name="sb_prompt",
    )(qkv, qkv, qkv)


def _sb_sample_kernel(q_ref, kn_ref, vn_ref, kc_ref, vc_ref, o_ref, acc_ref, r_ref, *, tk):
    group = kc_ref.shape[0]
    seq = q_ref.shape[0]
    n_tiles = kc_ref.shape[1] // tk
    acc_ref[...] = jnp.zeros_like(acc_ref)
    r_ref[...] = jnp.zeros_like(r_ref)
    row, col = _iota2(seq, seq)
    suffix_new = _suffix_incl(seq)
    suffix = _suffix_incl(tk)

    def head(s):
        cols = slice(s * HEAD_DIM, (s + 1) * HEAD_DIM)
        q = q_ref[:, cols]
        acc, r = acc_ref.at[s], r_ref.at[s]

        def kv(j):
            st = pl.multiple_of(j * tk, tk)
            return kc_ref[s, pl.ds(st, tk), :].astype(BF16), vc_ref[s, pl.ds(st, tk), :].astype(BF16)

        def new_rows():
            _sb_accumulate(acc, r, *_sb_tile(q, kn_ref[:, cols], vn_ref[:, cols], suffix_new, col < row))

        def single(j):
            _sb_accumulate(acc, r, *_sb_tile(q, *kv(j), suffix, None))

        def pair(j):
            pv_a, tot_a = _sb_tile(q, *kv(j), suffix, None)
            pv_b, tot_b = _sb_tile(q, *kv(j - 1), suffix, None)
            _sb_accumulate(acc, r, pv_a + jnp.exp2(-tot_a) * pv_b, tot_a + tot_b)

        return new_rows, single, pair, functools.partial(_sb_live, r)

    heads = [head(s) for s in range(group)]
    first = 2 if n_tiles >= 2 else n_tiles
    for new_rows, single, pair, _ in heads:
        new_rows()
        if first == 2:
            pair(n_tiles - 1)
        elif first == 1:
            single(0)
    for _, single, pair, live in heads:
        _for_past_tiles(n_tiles - first, pair, single, live)
    for s in range(group):
        o_ref[:, s * HEAD_DIM:(s + 1) * HEAD_DIM] = acc_ref[s].astype(o_ref.dtype)


def _sample_specs(layer, n_heads, group, seq, past, first_col):
    width = group * HEAD_DIM
    per_part = n_heads // group
    qkv = [pl.BlockSpec((seq, width), lambda b, h, k=k: (b, (first_col + k) * per_part + h)) for k in range(3)]
    cache = pl.BlockSpec((None, None, group, past, HEAD_DIM), lambda b, h: (layer, b, h, 0, 0))
    return qkv + [cache, cache]


def _sb_sample(qkv, cache_k, cache_v, layer, seq, tk, group):
    n = qkv.shape[0]
    nb = n // seq
    _, _, n_heads, past, _ = cache_k.shape
    assert n_heads % group == 0
    return pl.pallas_call(
        functools.partial(_sb_sample_kernel, tk=tk),
        grid=(nb, n_heads // group),
        in_specs=_sample_specs(layer, n_heads, group, seq, past, 0),
        out_specs=pl.BlockSpec((seq, group * HEAD_DIM), lambda b, h: (b, h)),
        out_shape=jax.ShapeDtypeStruct((n, n_heads * HEAD_DIM), BF16),
        scratch_shapes=[pltpu.VMEM((group, seq, HEAD_DIM), F32), pltpu.VMEM((group, seq, HEAD_DIM), F32)],
        compiler_params=_params(("arbitrary", "arbitrary")),
        name="sb_sample",
    )(qkv, qkv, qkv, cache_k, cache_v)


def _fox_init(m_ref, l_ref, acc_ref):
    m_ref[...] = jnp.full(m_ref.shape, NEG_BIG, F32)
    l_ref[...] = jnp.zeros_like(l_ref)
    acc_ref[...] = jnp.zeros_like(acc_ref)


def _fox_tile_t(qt, kb, vt, bias, mask, m_ref, l_ref, acc_ref):
    s = _dot(kb, qt) + jnp.concatenate([bias] * (qt.shape[1] // LANES), axis=1)
    if mask is not None:
        s = jnp.where(mask, s, NEG_BIG)
    m_prev = m_ref[...]
    m_new = jnp.maximum(m_prev, jnp.max(s, axis=0, keepdims=True))
    alpha = jnp.exp2(m_prev - m_new)
    p = jnp.exp2(s - m_new)
    l_ref[...] = alpha * l_ref[...] + jnp.sum(p, axis=0, keepdims=True)
    acc_ref[...] = alpha * acc_ref[...] + _dot(vt, p.astype(BF16))
    m_ref[...] = m_new


def _fox_prompt_kernel(q_ref, k_ref, v_ref, ck_ref, o_ref,
                       vt_ref, nb_ref, kmax_ref, edge_ref, m_ref, l_ref, acc_ref, *, tq, chunk):
    h = pl.program_id(0)
    qi = pl.program_id(1)
    t = k_ref.shape[0]
    reps = tq // LANES
    n_blocks = acc_ref.shape[0]

    @pl.when(qi == 0)
    def _():
        sel_r, _ = _iota2(LANES, LANES)
        sel = jnp.where(sel_r == h, 1.0, 0.0).astype(BF16)
        kmax_ref[...] = jnp.zeros_like(kmax_ref)

        def body(c, carry):
            st = pl.multiple_of(c * chunk, chunk)
            vt_ref[:, pl.ds(st, chunk)] = v_ref[pl.ds(st, chunk), :].astype(F32).T.astype(BF16)
            ck_h = sum(_dot(p, sel) for p in _split_bf16(ck_ref[pl.ds(st, chunk), :], 3))
            nb_ref[pl.ds(st, chunk), :] = ck_h * (-LOG2E)
            kf = k_ref[pl.ds(st, chunk), :].astype(F32)
            kmax_ref[...] = jnp.maximum(kmax_ref[...], jnp.max(jnp.sum(kf * kf, axis=1, keepdims=True)))
            return carry

        lax.fori_loop(0, t // chunk, body, 0)

    def block(s):
        g = n_blocks * qi + s
        qtf = q_ref[s * tq:(s + 1) * tq, :].astype(F32).T
        qt = qtf.astype(BF16)
        m, l, acc, edge = m_ref.at[s], l_ref.at[s], acc_ref.at[s], edge_ref.at[s]
        _fox_init(m, l, acc)
        qs = pl.multiple_of(g * tq, tq)
        c0 = nb_ref[pl.ds(qs, 1), :]
        edge[...] = c0

        def tile(st, tk, mask):
            bias = nb_ref[pl.ds(st, tk), :] - c0
            _fox_tile_t(qt, k_ref[pl.ds(st, tk), :], vt_ref[:, pl.ds(st, tk)], bias, mask, m, l, acc)
            edge[...] = nb_ref[pl.ds(st, 1), :]

        q_norm2 = jnp.sum(qtf * qtf, axis=0, keepdims=True)
        reach = jnp.sqrt(q_norm2 * jnp.concatenate([kmax_ref[...]] * reps, axis=1)) * 1.001

        def live():
            left = jnp.concatenate([edge[...] - c0] * reps, axis=1)
            return (jnp.max(reach + left - m[...]) > -FOX_DEAD).astype(jnp.int32)

        def first(with_left):
            if with_left:
                krow, qcol = _iota2(2 * tq, tq)
                tile(pl.multiple_of(qs - tq, tq), 2 * tq, krow - tq <= qcol)
            else:
                krow, qcol = _iota2(tq, tq)
                tile(qs, tq, krow <= qcol)

        def rest():
            _for_past_tiles(g - 1,
                            lambda j: tile(pl.multiple_of((j - 1) * tq, tq), 2 * tq, None),
                            lambda j: tile(pl.multiple_of(j * tq, tq), tq, None),
                            live)

        return first, rest

    blocks = [block(s) for s in range(n_blocks)]

    @pl.when(qi == 0)
    def _():
        for s, (first, _) in enumerate(blocks):
            first(s > 0)
        for _, rest in blocks[2:]:
            rest()

    @pl.when(qi > 0)
    def _():
        for first, _ in blocks:
            first(True)
        for _, rest in blocks:
            rest()

    for s in range(n_blocks):
        o_ref[s * tq:(s + 1) * tq, :] = (acc_ref[s] / l_ref[s]).T.astype(o_ref.dtype)


def _fox_prompt(qkv, ck, n_heads, tq, n_blocks):
    t = qkv.shape[0]
    rows = n_blocks * tq
    return pl.pallas_call(
        functools.partial(_fox_prompt_kernel, tq=tq, chunk=512),
        grid=(n_heads, t // rows),
        in_specs=[
            pl.BlockSpec((rows, HEAD_DIM), lambda h, i: (i, 3 * n_heads + h)),
            pl.BlockSpec((t, HEAD_DIM), lambda h, i: (0, 4 * n_heads + h)),
            pl.BlockSpec((t, HEAD_DIM), lambda h, i: (0, 5 * n_heads + h)),
            pl.BlockSpec((t, LANES), lambda h, i: (0, 0)),
        ],
        out_specs=pl.BlockSpec((rows, HEAD_DIM), lambda h, i: (i, h)),
        out_shape=jax.ShapeDtypeStruct((t, n_heads * HEAD_DIM), BF16),
        scratch_shapes=[pltpu.VMEM((HEAD_DIM, t), BF16), pltpu.VMEM((t, LANES), F32),
                        pltpu.VMEM((1, LANES), F32), pltpu.VMEM((n_blocks, 1, LANES), F32),
                        pltpu.VMEM((n_blocks, 1, tq), F32), pltpu.VMEM((n_blocks, 1, tq), F32),
                        pltpu.VMEM((n_blocks, HEAD_DIM, tq), F32)],
        compiler_params=_params(("arbitrary", "arbitrary")),
        name="fox_prompt",
    )(qkv, qkv, qkv, ck)


def _fox_tile(q, kb, vb, bias, mask, m_ref, l_ref, acc_ref):
    s = _dot_nt(q, kb) + bias
    if mask is not None:
        s = jnp.where(mask, s, NEG_BIG)
    m_prev = m_ref[...]
    m_new = jnp.maximum(m_prev, jnp.max(s, axis=1, keepdims=True))
    alpha = jnp.exp2(m_prev - m_new)
    p = jnp.exp2(s - m_new)
    l_ref[...] = alpha * l_ref[...] + jnp.sum(p, axis=1, keepdims=True)
    acc_ref[...] = alpha * acc_ref[...] + _dot(p.astype(BF16), vb)
    m_ref[...] = m_new


def _fox_sample_kernel(q_ref, kn_ref, vn_ref, kc_ref, vc_ref, ckp_ref, nmax_ref, ckl_ref, o_ref,
                       m_ref, l_ref, acc_ref, *, tk):
    group = kc_ref.shape[0]
    seq = q_ref.shape[0]
    past = kc_ref.shape[1]
    n_tiles = past // tk
    _fox_init(m_ref, l_ref, acc_ref)
    row, col = _iota2(seq, seq)

    def head(s):
        cols = slice(s * HEAD_DIM, (s + 1) * HEAD_DIM)
        q = q_ref[:, cols]
        m, l, acc = m_ref.at[s], l_ref.at[s], acc_ref.at[s]
        c0 = ckp_ref[s, :, past - 1:past]
        qf = q.astype(F32)
        k_abs = jnp.max(jnp.abs(kc_ref[s]))
        reach = jnp.sqrt(jnp.sum(qf * qf, axis=1, keepdims=True) * HEAD_DIM) * (k_abs * 1.01)

        def new_rows():
            _fox_tile(q, kn_ref[:, cols], vn_ref[:, cols], ckl_ref[s] * (-LOG2E), col <= row, m, l, acc)

        def tile(j):
            st = pl.multiple_of(j * tk, tk)
            kb = kc_ref[s, pl.ds(st, tk), :].astype(BF16)
            vb = vc_ref[s, pl.ds(st, tk), :].astype(BF16)
            bias = (c0 - ckp_ref[s, :, pl.ds(st, tk)]) * LOG2E
            _fox_tile(q, kb, vb, bias, None, m, l, acc)
            lo = pl.multiple_of(jnp.maximum(st - LANES, 0), LANES)
            left = nmax_ref[s, :, pl.ds(lo, LANES)][:, LANES - 1:LANES]
            return (jnp.max(reach + (c0 + left) * LOG2E - m[...]) > -FOX_DEAD).astype(jnp.int32)

        return new_rows, tile

    heads = [head(s) for s in range(group)]
    live = []
    for new_rows, tile in heads:
        new_rows()
        live.append(tile(n_tiles - 1))
    for (_, tile), live0 in zip(heads, live):
        lax.while_loop(lambda c: c[0] >= 0,
                       lambda c, tile=tile: (c[0] - 1, tile(c[0])),
                       (jnp.int32(n_tiles - 2), live0))
    for s in range(group):
        o_ref[:, s * HEAD_DIM:(s + 1) * HEAD_DIM] = (acc_ref[s] / l_ref[s]).astype(o_ref.dtype)


def _fox_sample(qkv, cache_k, cache_v, ck_past, nmax_past, ck_local, layer, seq, tk, group):
    n = qkv.shape[0]
    nb = n // seq
    _, _, n_heads, past, _ = cache_k.shape
    assert n_heads % group == 0
    return pl.pallas_call(
        functools.partial(_fox_sample_kernel, tk=tk),
        grid=(nb, n_heads // group),
        in_specs=_sample_specs(layer, n_heads, group, seq, past, 3) + [
            pl.BlockSpec((group, None, 1, past), lambda b, h: (h, b, 0, 0)),
            pl.BlockSpec((group, None, 1, past), lambda b, h: (h, b, 0, 0)),
            pl.BlockSpec((group, None, 1, seq), lambda b, h: (h, b, 0, 0)),
        ],
        out_specs=pl.BlockSpec((seq, group * HEAD_DIM), lambda b, h: (b, h)),
        out_shape=jax.ShapeDtypeStruct((n, n_heads * HEAD_DIM), BF16),
        scratch_shapes=[pltpu.VMEM((group, seq, 1), F32), pltpu.VMEM((group, seq, 1), F32),
                        pltpu.VMEM((group, seq, HEAD_DIM), F32)],
        compiler_params=_params(("arbitrary", "arbitrary")),
        name="fox_sample",
    )(qkv, qkv, qkv, cache_k, cache_v, ck_past, nmax_past, ck_local)


def _out_proj_kernel(ya_ref, ysb_ref, yfox_ref, w_ref, x_ref, g_ref, xo_ref, h_ref):
    c = ya_ref.shape[1]
    s = ysb_ref.shape[1]
    acc = _dot(ya_ref[...], w_ref[0:c, :])
    acc += _dot(ysb_ref[...], w_ref[c:c + s, :])
    acc += _dot(yfox_ref[...], w_ref[c + s:, :])
    x = x_ref[...] + acc
    xo_ref[...] = x
    h_ref[...] = _rmsnorm_f32(x, g_ref[...]).astype(h_ref.dtype)


def _out_proj(ya, ysb, yfox, w, layer, x, g, tm):
    m, d = x.shape
    row = lambda width: pl.BlockSpec((tm, width), lambda i: (i, 0))
    return pl.pallas_call(
        _out_proj_kernel,
        grid=(m // tm,),
        in_specs=[row(ya.shape[1]), row(ysb.shape[1]), row(yfox.shape[1]),
                  pl.BlockSpec((None,) + w.shape[1:], lambda i: (layer, 0, 0)), row(d),
                  pl.BlockSpec((1, d), lambda i: (0, 0))],
        out_specs=[row(d), row(d)],
        out_shape=[jax.ShapeDtypeStruct((m, d), F32), jax.ShapeDtypeStruct((m, d), BF16)],
        compiler_params=_params(("arbitrary",)),
        name="out_proj",
    )(ya, ysb, yfox, w, x, g.reshape(1, d))


def _ffn_kernel(h_ref, wg_ref, wu_ref, wd_ref, x_ref, g_ref, *rest, emit_x):
    out_refs, acc_ref = rest[:-1], rest[-1]
    f = pl.program_id(1)

    @pl.when(f == 0)
    def _():
        acc_ref[...] = jnp.zeros_like(acc_ref)

    h = h_ref[...]
    gate = _dot(h, wg_ref[...])
    up = _dot(h, wu_ref[...])
    act = gate * jax.nn.sigmoid(gate) * up
    acc_ref[...] += _dot(act.astype(BF16), wd_ref[...])

    @pl.when(f == pl.num_programs(1) - 1)
    def _():
        x = x_ref[...] + acc_ref[...]
        normed = _rmsnorm_f32(x, g_ref[...])
        if emit_x:
            out_refs[0][...] = x
            out_refs[1][...] = normed.astype(out_refs[1].dtype)
        else:
            out_refs[0][...] = normed


def _ffn(h, wg, wu, wd, layer, x, g, tm, tf, emit_x):
    m, d = x.shape
    ff = wg.shape[2]
    row = pl.BlockSpec((tm, d), lambda i, f: (i, 0))
    if emit_x:
        out_specs = [row, row]
        out_shape = [jax.ShapeDtypeStruct((m, d), F32), jax.ShapeDtypeStruct((m, d), BF16)]
    else:
        out_specs = [row]
        out_shape = [jax.ShapeDtypeStruct((m, d), F32)]
    return pl.pallas_call(
        functools.partial(_ffn_kernel, emit_x=emit_x),
        grid=(m // tm, ff // tf),
        in_specs=[row,
                  pl.BlockSpec((None, d, tf), lambda i, f: (layer, 0, f)),
                  pl.BlockSpec((None, d, tf), lambda i, f: (layer, 0, f)),
                  pl.BlockSpec((None, tf, d), lambda i, f: (layer, f, 0)),
                  row,
                  pl.BlockSpec((1, d), lambda i, f: (0, 0))],
        out_specs=out_specs,
        out_shape=out_shape,
        scratch_shapes=[pltpu.VMEM((tm, d), F32)],
        compiler_params=_params(("arbitrary", "arbitrary")),
        name="ffn",
    )(h, wg, wu, wd, x, g.reshape(1, d))


def kernel(x_prompt, x_sample, state_conv, cache_sb_k, cache_sb_v, cache_fox_k, cache_fox_v, cache_fox_logf,
           norm1_g, w_in, b_f, conv_w, w_out, norm2_g, w_gate, w_up, w_down, final_g):
    depth = w_in.shape[0]
    bp, t, d = x_prompt.shape
    nb, seq, _ = x_sample.shape
    assert bp == 1, "prompt kernels carry conv rows and forget sums across row tiles of one stream"
    c = conv_w.shape[2]
    n_sb = cache_sb_k.shape[3]
    n_fox = cache_fox_k.shape[3]
    assert n_sb == n_fox and n_fox <= 8
    past = cache_sb_k.shape[2]
    sb_dim = n_sb * HEAD_DIM
    qkv0 = 3 * c
    qkv1 = qkv0 + 3 * sb_dim + 3 * n_fox * HEAD_DIM
    n_s = nb * seq
    per_head = lambda a: jnp.transpose(a, (0, 1, 3, 2, 4))
    csk, csv, cfk, cfv = (per_head(a) for a in (cache_sb_k, cache_sb_v, cache_fox_k, cache_fox_v))
    w_ndk = jnp.transpose(w_in, (2, 0, 1))
    wt = _wprep(w_ndk, qkv1, W_PREP_TILE)
    wo, wg, wu, wd = (w.astype(BF16) for w in (w_out, w_gate, w_up, w_down))

    xp = x_prompt.reshape(t, d)
    xs = x_sample.reshape(n_s, d)
    hp = _rmsnorm(xp, norm1_g[0], ROW_TILE)
    hs = _rmsnorm(xs, norm1_g[0], n_s)
    p_small, s_small = [], []
    p_kv, s_kv = [], []
    for l in range(depth):
        bf_row = jnp.pad(b_f[l], (0, LANES - n_fox)).reshape(1, LANES)
        bf_col = jnp.pad(b_f[l], (0, 8 - n_fox)).reshape(8, 1)
        last = l == depth - 1
        g_next = final_g if last else norm1_g[l + 1]

        ya, conv_p, logf_p, ck_p = _mix_a_prompt(hp, wt, w_ndk, qkv1, l, bf_row, conv_w[l], ROW_TILE)
        stacking = last and l > 0
        qkv, *kv = _qkv_proj(hp, wt, qkv0, QKV_STACK_ROW_TILE if stacking else QKV_ROW_TILE, l, bp, n_sb,
                             p_kv if stacking else ())
        p_kv.append(kv)
        ysb = _sb_prompt(qkv, n_sb, ATTN_TILE, ATTN_BLOCKS)
        yfox = _fox_prompt(qkv, ck_p, n_fox, ATTN_TILE, ATTN_BLOCKS)
        xp, h2 = _out_proj(ya, ysb, yfox, wo, l, xp, norm2_g[l], ROW_TILE)
        outs = _ffn(h2, wg, wu, wd, l, xp, g_next, ROW_TILE, FF_TILE, not last)
        if last:
            y_prompt = outs[0]
        else:
            xp, hp = outs
        p_small.append((conv_p, logf_p[:, :n_fox]))

        logf_past = jnp.transpose(cache_fox_logf[l], (2, 0, 1)).reshape(n_fox * nb, past)
        ck_past, nmax_past = (a.reshape(n_fox, nb, 1, past) for a in _cumsum_rows(logf_past))
        ya, conv_s, logf_s, ckl = _mix_a_sample(hs, wt, w_ndk, qkv1, l, bf_col, conv_w[l],
                                                state_conv[l].reshape(nb * (CONV_W - 1), c), seq)
        ckl = ckl[:n_fox].reshape(n_fox, nb, 1, seq)
        qkv, *kv = _qkv_proj(hs, wt, qkv0, n_s, l, nb, n_sb, s_kv if stacking else ())
        s_kv.append(kv)
        ysb = _sb_sample(qkv, csk, csv, l, seq, DECODE_SB_TILE, DECODE_HEADS)
        yfox = _fox_sample(qkv, cfk, cfv, ck_past, nmax_past, ckl, l, seq, DECODE_FOX_TILE, DECODE_HEADS)
        xs, h2 = _out_proj(ya, ysb, yfox, wo, l, xs, norm2_g[l], n_s)
        outs = _ffn(h2, wg, wu, wd, l, xs, g_next, n_s, FF_TILE, not last)
        if last:
            y_sample = outs[0]
        else:
            xs, hs = outs
        logf_s = jnp.transpose(logf_s[:n_fox].reshape(n_fox, nb, seq), (1, 2, 0))
        s_small.append((conv_s, logf_s))

    stack = lambda news, i, shape: jnp.stack([n[i] for n in news]).reshape((depth,) + shape)
    stack_kv = lambda news: [per_head(a if depth > 1 else a[None]) for a in news[-1]]
    return (
        y_prompt.reshape(bp, t, d),
        y_sample.reshape(nb, seq, d),
        stack(p_small, 0, (bp, CONV_W - 1, c)),
        *stack_kv(p_kv),
        stack(p_small, 1, (bp, t, n_fox)),
        stack(s_small, 0, (nb, CONV_W - 1, c)),
        *stack_kv(s_kv),
        stack(s_small, 1, (nb, seq, n_fox)),
    )
```

```python
import functools

import jax
import jax.numpy as jnp
from jax import lax
from jax.experimental import pallas as pl
from jax.experimental.pallas import tpu as pltpu

F32 = jnp.float32
BF16 = jnp.bfloat16

NORM_EPS = 1e-5
HEAD_DIM = 128
CONV_W = 3
LANES = 128
VMEM_LIMIT = 56 * 1024 * 1024
NEG_BIG = -1e30
FOX_DEAD = 160.0
LOG2E = 1.4426950408889634
QK_SCALE = HEAD_DIM ** -0.5 * LOG2E

ROW_TILE = 512
QKV_ROW_TILE = 1024
QKV_STACK_ROW_TILE = 512
FF_TILE = 512
W_PREP_TILE = 512
ATTN_TILE = 2 * LANES
ATTN_BLOCKS = 8
DECODE_SB_TILE = 256
DECODE_FOX_TILE = 512
DECODE_HEADS = 3
DECODE_FF_TILE = 1408


def _params(sem, vmem=VMEM_LIMIT):
    return pltpu.CompilerParams(dimension_semantics=sem, vmem_limit_bytes=vmem)


def _dot(a, b):
    return jnp.dot(a, b, preferred_element_type=F32)


def _dot_nt(a, b):
    return lax.dot_general(a, b, (((1,), (1,)), ((), ())), preferred_element_type=F32)


def _softplus2(z2):
    return jnp.maximum(z2, 0.0) + jnp.log(1.0 + jnp.exp2(-jnp.abs(z2))) * LOG2E


def _log_sigmoid(x):
    return jnp.minimum(x, 0.0) - jnp.log(1.0 + jnp.exp(-jnp.abs(x)))


def _split_bf16(x, parts):
    out = []
    rem = x
    for p in range(parts):
        hi = rem.astype(BF16)
        out.append(hi)
        if p + 1 < parts:
            rem = rem - hi.astype(F32)
    return out


def _iota2(n, m):
    return lax.broadcasted_iota(jnp.int32, (n, m), 0), lax.broadcasted_iota(jnp.int32, (n, m), 1)


def _tri_incl(n, seg=None):
    j, s = _iota2(n, n)
    t = jnp.where(j <= s, 1.0, 0.0)
    if seg is not None:
        t = jnp.where(j // seg == s // seg, t, 0.0)
    return t.astype(BF16)


def _suffix_incl(n):
    j, s = _iota2(n, n)
    return jnp.where(j >= s, 1.0, 0.0).astype(BF16)


def _prefix_sum_lanes(x, tri, carry):
    outs = []
    for c in range(x.shape[1] // LANES):
        xc = x[:, c * LANES:(c + 1) * LANES]
        loc = sum(_dot(p, tri) for p in _split_bf16(xc, 3))
        oc = loc if carry is None else loc + carry
        outs.append(oc)
        if carry is not None:
            carry = oc[:, LANES - 1:LANES]
    return jnp.concatenate(outs, axis=1), carry


def _rmsnorm_f32(x, g):
    ms = jnp.mean(x * x, axis=-1, keepdims=True)
    return x * lax.rsqrt(ms + NORM_EPS) * g


def _rmsnorm_kernel(x_ref, g_ref, o_ref):
    o_ref[...] = _rmsnorm_f32(x_ref[...], g_ref[...]).astype(o_ref.dtype)


def _rmsnorm(x, g, tm):
    m, d = x.shape
    return pl.pallas_call(
        _rmsnorm_kernel,
        grid=(m // tm,),
        in_specs=[pl.BlockSpec((tm, d), lambda i: (i, 0)), pl.BlockSpec((1, d), lambda i: (0, 0))],
        out_specs=pl.BlockSpec((tm, d), lambda i: (i, 0)),
        out_shape=jax.ShapeDtypeStruct((m, d), BF16),
        compiler_params=_params(("arbitrary",)),
        name="rmsnorm",
    )(x, g.reshape(1, d))


def _gated_conv(proj, cw, u_m1, u_m2):
    c = proj.shape[1] // 3
    g_b, g_c, hc = proj[:, :c], proj[:, c:2 * c], proj[:, 2 * c:]
    u = g_c * hc
    r2 = pltpu.roll(u, 2, 0)
    u1 = u_m1(pltpu.roll(u, 1, 0))
    u2 = u_m2(r2)
    conv = cw[0:1, :] * u2 + cw[1:2, :] * u1 + cw[2:3, :] * u
    return g_b * conv, u, r2


def _mix_a_prompt_kernel(h_ref, wa_ref, wtail_ref, bf_ref, cw_ref,
                         ya_ref, conv_ref, logf_ref, ck_ref, uprev_ref, carry_ref, wrows_ref, wf_ref, *, layer):
    i = pl.program_id(0)

    @pl.when(i == 0)
    def _():
        uprev_ref[...] = jnp.zeros_like(uprev_ref)
        carry_ref[...] = jnp.zeros_like(carry_ref)
        wrows_ref[...] = jnp.zeros_like(wrows_ref)
        wrows_ref[0:wtail_ref.shape[0], :] = wtail_ref[:, layer, :]
        wf_ref[...] = wrows_ref[...].T.astype(wf_ref.dtype)

    h = h_ref[...]
    proj = _dot_nt(h, wa_ref[...])
    tm = proj.shape[0]
    rows = lax.broadcasted_iota(jnp.int32, (tm, proj.shape[1] // 3), 0)
    prev = uprev_ref[...]
    m1 = lambda r1: jnp.where(rows == 0, prev[1:2, :], r1)
    m2 = lambda r2: jnp.where(rows == 0, prev[0:1, :], jnp.where(rows == 1, prev[1:2, :], r2))
    ya, _, r2 = _gated_conv(proj, cw_ref[...], m1, m2)
    ya_ref[...] = ya.astype(ya_ref.dtype)
    tail = r2[0:2, :]
    uprev_ref[0:2, :] = tail
    conv_ref[...] = tail

    logf = _log_sigmoid(_dot(h, wf_ref[...]) + bf_ref[...])
    logf_ref[...] = logf
    rows_per_sum = 2 * LANES
    lower = _suffix_incl(rows_per_sum)
    carry = carry_ref[...]
    for c in range(tm // rows_per_sum):
        part = logf[c * rows_per_sum:(c + 1) * rows_per_sum, :]
        ck = sum(_dot(lower, p) for p in _split_bf16(part, 3)) + carry
        ck_ref[c * rows_per_sum:(c + 1) * rows_per_sum, :] = ck
        carry = ck[rows_per_sum - 1:rows_per_sum, :]
    carry_ref[...] = carry


def _tail_spec(w_ndk, first_row):
    n, depth, d = w_ndk.shape
    assert first_row % (n - first_row) == 0
    return pl.BlockSpec((n - first_row, depth, d), lambda i: (first_row // (n - first_row), 0, 0))


def _mix_a_prompt(h, wt, w_ndk, tail_row, layer, bfp, cw, tm):
    t, d = h.shape
    c = cw.shape[1]
    c3 = 3 * c
    return pl.pallas_call(
        functools.partial(_mix_a_prompt_kernel, layer=layer),
        grid=(t // tm,),
        in_specs=[
            pl.BlockSpec((tm, d), lambda i: (i, 0)),
            pl.BlockSpec((None, c3, d), lambda i: (layer, 0, 0)),
            _tail_spec(w_ndk, tail_row),
            pl.BlockSpec((1, LANES), lambda i: (0, 0)),
            pl.BlockSpec((CONV_W, c), lambda i: (0, 0)),
        ],
        out_specs=[
            pl.BlockSpec((tm, c), lambda i: (i, 0)),
            pl.BlockSpec((CONV_W - 1, c), lambda i: (0, 0)),
            pl.BlockSpec((tm, LANES), lambda i: (i, 0)),
            pl.BlockSpec((tm, LANES), lambda i: (i, 0)),
        ],
        out_shape=[
            jax.ShapeDtypeStruct((t, c), BF16),
            jax.ShapeDtypeStruct((CONV_W - 1, c), F32),
            jax.ShapeDtypeStruct((t, LANES), F32),
            jax.ShapeDtypeStruct((t, LANES), F32),
        ],
        scratch_shapes=[pltpu.VMEM((8, c), F32), pltpu.VMEM((1, LANES), F32),
                        pltpu.VMEM((LANES, d), F32), pltpu.VMEM((d, LANES), BF16)],
        compiler_params=_params(("arbitrary",)),
        name="mix_a_prompt",
    )(h, wt, w_ndk, bfp, cw)


def _mix_a_sample_kernel(h_ref, wa_ref, wtail_ref, bf_ref, cw_ref, st_ref,
                         ya_ref, conv_ref, logf_ref, ckl_ref, u_ref, wrows_ref, *, seq, layer):
    h = h_ref[...]
    proj = _dot_nt(h, wa_ref[...])
    n = proj.shape[0]
    nb = n // seq
    rows = lax.broadcasted_iota(jnp.int32, (n, proj.shape[1] // 3), 0)

    def m1(r1):
        for b in range(nb):
            r1 = jnp.where(rows == b * seq, st_ref[2 * b + 1:2 * b + 2, :], r1)
        return r1

    def m2(r2):
        for b in range(nb):
            r2 = jnp.where(rows == b * seq, st_ref[2 * b:2 * b + 1, :], r2)
            r2 = jnp.where(rows == b * seq + 1, st_ref[2 * b + 1:2 * b + 2, :], r2)
        return r2

    ya, u, _ = _gated_conv(proj, cw_ref[...], m1, m2)
    ya_ref[...] = ya.astype(ya_ref.dtype)
    u_ref[...] = u
    for b in range(nb):
        conv_ref[2 * b:2 * b + 2, :] = u_ref[(b + 1) * seq - 2:(b + 1) * seq, :]

    wrows_ref[...] = jnp.zeros_like(wrows_ref)
    wrows_ref[0:wtail_ref.shape[0], :] = wtail_ref[:, layer, :]
    logit = _dot_nt(wrows_ref[...].astype(BF16), h)[0:8, :]
    logf = _log_sigmoid(logit + bf_ref[...])
    logf_ref[...] = logf
    ckl, _ = _prefix_sum_lanes(logf, _tri_incl(LANES, seg=seq), None)
    ckl_ref[...] = ckl


def _mix_a_sample(h, wt, w_ndk, tail_row, layer, bfp, cw, state, seq):
    n, d = h.shape
    c = cw.shape[1]
    c3 = 3 * c
    nb = n // seq
    full = lambda shape: pl.BlockSpec(shape, lambda i: tuple(0 for _ in shape))
    return pl.pallas_call(
        functools.partial(_mix_a_sample_kernel, seq=seq, layer=layer),
        grid=(1,),
        in_specs=[full((n, d)), pl.BlockSpec((None, c3, d), lambda i: (layer, 0, 0)),
                  _tail_spec(w_ndk, tail_row), full((8, 1)), full((CONV_W, c)),
                  full((nb * (CONV_W - 1), c))],
        out_specs=[full((n, c)), full((nb * (CONV_W - 1), c)), full((8, n)), full((8, n))],
        out_shape=[
            jax.ShapeDtypeStruct((n, c), BF16),
            jax.ShapeDtypeStruct((nb * (CONV_W - 1), c), F32),
            jax.ShapeDtypeStruct((8, n), F32),
            jax.ShapeDtypeStruct((8, n), F32),
        ],
        scratch_shapes=[pltpu.VMEM((n, c), F32), pltpu.VMEM((16, d), F32)],
        compiler_params=_params(("arbitrary",)),
        name="mix_a_sample",
    )(h, wt, w_ndk, bfp, cw, state)


def _prefix_max_lanes(x):
    lane = lax.broadcasted_iota(jnp.int32, (x.shape[0], LANES), 1)
    outs, carry = [], None
    for c in range(x.shape[1] // LANES):
        xc = x[:, c * LANES:(c + 1) * LANES]
        shift = 1
        while shift < LANES:
            xc = jnp.maximum(xc, jnp.where(lane >= shift, pltpu.roll(xc, shift, 1), NEG_BIG))
            shift *= 2
        if carry is not None:
            xc = jnp.maximum(xc, carry)
        outs.append(xc)
        carry = xc[:, LANES - 1:LANES]
    return jnp.concatenate(outs, axis=1)


def _cumsum_rows_kernel(x_ref, ck_ref, nmax_ref):
    ck, _ = _prefix_sum_lanes(x_ref[...], _tri_incl(LANES), jnp.zeros((x_ref.shape[0], 1), F32))
    ck_ref[...] = ck
    nmax_ref[...] = _prefix_max_lanes(-ck)


def _cumsum_rows(x):
    spec = pl.BlockSpec(x.shape, lambda i: (0, 0))
    return pl.pallas_call(
        _cumsum_rows_kernel,
        grid=(1,),
        in_specs=[spec],
        out_specs=[spec, spec],
        out_shape=[jax.ShapeDtypeStruct(x.shape, F32)] * 2,
        compiler_params=_params(("arbitrary",)),
        name="cumsum_rows",
    )(x)


def _wprep_kernel(w_hbm, o_ref, buf_ref, sem_ref):
    depth = pl.num_programs(1)
    tn = buf_ref.shape[1]
    s = pl.program_id(0) * depth + pl.program_id(1)
    n_steps = pl.num_programs(0) * depth

    def fetch(step, slot):
        i, layer = step // depth, step % depth
        return pltpu.make_async_copy(w_hbm.at[pl.ds(i * tn, tn), layer, :], buf_ref.at[slot], sem_ref.at[slot])

    @pl.when(s == 0)
    def _():
        fetch(s, 0).start()

    @pl.when(s + 1 < n_steps)
    def _():
        fetch(s + 1, (s + 1) % 2).start()

    fetch(s, s % 2).wait()
    o_ref[...] = buf_ref[s % 2].astype(o_ref.dtype)


def _wprep(w_ndk, n_rows, tn):
    _, depth, d = w_ndk.shape
    return pl.pallas_call(
        _wprep_kernel,
        grid=(n_rows // tn, depth),
        in_specs=[pl.BlockSpec(memory_space=pl.ANY)],
        out_specs=pl.BlockSpec((None, tn, d), lambda i, l: (l, i, 0)),
        out_shape=jax.ShapeDtypeStruct((depth, n_rows, d), BF16),
        scratch_shapes=[pltpu.VMEM((2, tn, d), F32), pltpu.SemaphoreType.DMA((2,))],
        compiler_params=_params(("arbitrary", "arbitrary")),
        name="wprep",
    )(w_ndk)


def _store_heads(ref, acc):
    n_heads = ref.shape[-3]
    for hd in range(n_heads):
        part = acc[:, hd * HEAD_DIM:(hd + 1) * HEAD_DIM]
        if len(ref.shape) == 3:
            ref[hd] = part
        else:
            seq = ref.shape[2]
            for b in range(ref.shape[0]):
                ref[b, hd] = part[b * seq:(b + 1) * seq, :]


def _qkv_kernel(h_ref, w_ref, *refs, n_earlier):
    earlier = refs[:4 * n_earlier]
    qkv_ref = refs[4 * n_earlier]
    kv_refs = refs[4 * n_earlier + 1:]
    j = pl.program_id(1)
    acc = _dot_nt(h_ref[...], w_ref[...])
    @pl.when(jnp.logical_or(j == 0, j == 3))
    def _():
        qkv_ref[...] = (acc * QK_SCALE).astype(qkv_ref.dtype)

    for kind, idx in enumerate((1, 2, 4, 5)):
        @pl.when(j == idx)
        def _(kind=kind):
            qkv_ref[...] = acc.astype(qkv_ref.dtype)
            ref = kv_refs[kind]
            if n_earlier:
                for p in range(n_earlier):
                    ref[p] = earlier[4 * p + kind][...]
                ref = ref.at[n_earlier]
            _store_heads(ref, acc)


def _qkv_proj(h, wt, first_row, tm, layer, batch, n_heads, earlier=()):
    m, d = h.shape
    seq = m // batch
    tn = n_heads * HEAD_DIM
    assert first_row % tn == 0
    blk0 = first_row // tn
    if m // tm == 1:
        block, index = (batch, n_heads, seq, HEAD_DIM), lambda i, j: (0, 0, 0, 0)
    else:
        assert batch == 1
        block, index = (None, n_heads, tm, HEAD_DIM), lambda i, j: (0, 0, i, 0)
    kv_in_spec = pl.BlockSpec(block, index)
    kv_shape = (batch, n_heads, seq, HEAD_DIM)
    n_earlier = len(earlier)
    if n_earlier:
        kv_out_spec = pl.BlockSpec((n_earlier + 1,) + block, lambda i, j: (0,) + index(i, j))
        kv_shape = (n_earlier + 1,) + kv_shape
    else:
        kv_out_spec = kv_in_spec
    return pl.pallas_call(
        functools.partial(_qkv_kernel, n_earlier=n_earlier),
        grid=(m // tm, 6),
        in_specs=[pl.BlockSpec((tm, d), lambda i, j: (i, 0)),
                  pl.BlockSpec((None, tn, d), lambda i, j: (layer, blk0 + j, 0))]
        + [kv_in_spec] * (4 * n_earlier),
        out_specs=[pl.BlockSpec((tm, tn), lambda i, j: (i, j))] + [kv_out_spec] * 4,
        out_shape=[jax.ShapeDtypeStruct((m, 6 * tn), BF16)] + [jax.ShapeDtypeStruct(kv_shape, F32)] * 4,
        compiler_params=_params(("arbitrary", "arbitrary")),
        name="qkv_proj",
    )(h, wt, *(a for kv in earlier for a in kv))


def _sb_tile(q, kb, vb, suffix, mask):
    z = _dot_nt(q, kb)
    sp = _softplus2(z)
    if mask is not None:
        sp = jnp.where(mask, sp, 0.0)
    s_incl = _dot(sp.astype(BF16), suffix)
    w = jnp.exp2(z - s_incl)
    if mask is not None:
        w = jnp.where(mask, w, 0.0)
    return _dot(w.astype(BF16), vb), s_incl[:, 0:1]


def _sb_live(r_ref):
    r_min = jnp.min(r_ref[...], axis=0, keepdims=True)
    return (jnp.max(jnp.exp2(-r_min)) > 0.0).astype(jnp.int32)


def _sb_accumulate(acc_ref, r_ref, pv, tot):
    r = r_ref[...]
    acc_ref[...] += jnp.exp2(-r) * pv
    r_ref[...] = r + tot


def _for_past_tiles(n_tiles, pair_fn, single_fn, live_fn=None):
    if live_fn is None:
        def body(jj, c):
            pair_fn(n_tiles - 1 - 2 * jj)
            return c

        lax.fori_loop(0, n_tiles // 2, body, 0)
        live = True
    else:
        def step(c):
            pair_fn(n_tiles - 1 - 2 * c[0])
            return c[0] + 1, live_fn()

        _, live = lax.while_loop(lambda c: jnp.logical_and(c[0] < n_tiles // 2, c[1] > 0), step,
                                 (jnp.int32(0), live_fn()))
        live = live > 0
    odd = n_tiles % 2 == 1
    if isinstance(odd, bool) and live is True:
        if odd:
            single_fn(0)
    else:
        pl.when(jnp.logical_and(odd, live))(lambda: single_fn(0))


def _sb_prompt_kernel(q_ref, k_ref, v_ref, o_ref, acc_ref, r_ref, *, tq):
    qi = pl.program_id(1)
    n_blocks = acc_ref.shape[0]
    acc_ref[...] = jnp.zeros_like(acc_ref)
    r_ref[...] = jnp.zeros_like(r_ref)
    suffix = _suffix_incl(tq)
    row, col = _iota2(tq, tq)

    def kv(j):
        st = pl.multiple_of(j * tq, tq)
        return k_ref[pl.ds(st, tq), :], v_ref[pl.ds(st, tq), :]

    def block(s):
        q = q_ref[s * tq:(s + 1) * tq, :]
        acc, r = acc_ref.at[s], r_ref.at[s]

        def single(j, mask=None):
            _sb_accumulate(acc, r, *_sb_tile(q, *kv(j), suffix, mask))

        def pair(j, mask=None):
            pv_a, tot_a = _sb_tile(q, *kv(j), suffix, mask)
            pv_b, tot_b = _sb_tile(q, *kv(j - 1), suffix, None)
            _sb_accumulate(acc, r, pv_a + jnp.exp2(-tot_a) * pv_b, tot_a + tot_b)

        return single, pair, functools.partial(_sb_live, r)

    blocks = [block(s) for s in range(n_blocks)]
    diag = [n_blocks * qi + s for s in range(n_blocks)]

    @pl.when(qi == 0)
    def _():
        blocks[0][0](diag[0], col < row)
        for (_, pair, _), g in zip(blocks[1:], diag[1:]):
            pair(g, col < row)
        for s, (single, pair, live) in enumerate(blocks):
            if s >= 2:
                _for_past_tiles(s - 1, pair, single, live)

    @pl.when(qi > 0)
    def _():
        for (_, pair, _), g in zip(blocks, diag):
            pair(g, col < row)
        for (single, pair, live), g in zip(blocks, diag):
            _for_past_tiles(g - 1, pair, single, live)

    for s in range(n_blocks):
        o_ref[s * tq:(s + 1) * tq, :] = acc_ref[s].astype(o_ref.dtype)


def _sb_prompt(qkv, n_heads, tq, n_blocks):
    t = qkv.shape[0]
    rows = n_blocks * tq
    return pl.pallas_call(
        functools.partial(_sb_prompt_kernel, tq=tq),
        grid=(n_heads, t // rows),
        in_specs=[
            pl.BlockSpec((rows, HEAD_DIM), lambda h, i: (i, h)),
            pl.BlockSpec((t, HEAD_DIM), lambda h, i: (0, n_heads + h)),
            pl.BlockSpec((t, HEAD_DIM), lambda h, i: (0, 2 * n_heads + h)),
        ],
        out_specs=pl.BlockSpec((rows, HEAD_DIM), lambda h, i: (i, h)),
        out_shape=jax.ShapeDtypeStruct((t, n_heads * HEAD_DIM), BF16),
        scratch_shapes=[pltpu.VMEM((n_blocks, tq, HEAD_DIM), F32), pltpu.VMEM((n_blocks, tq, HEAD_DIM), F32)],
        compiler_params=_params(("arbitrary", "arbitrary")),
        name="sb_prompt",
    )(qkv, qkv, qkv)


def _sb_sample_kernel(q_ref, kn_ref, vn_ref, kc_ref, vc_ref, o_ref, acc_ref, r_ref, *, tk):
    group = kc_ref.shape[0]
    seq = q_ref.shape[0]
    n_tiles = kc_ref.shape[1] // tk
    acc_ref[...] = jnp.zeros_like(acc_ref)
    r_ref[...] = jnp.zeros_like(r_ref)
    row, col = _iota2(seq, seq)
    suffix_new = _suffix_incl(seq)
    suffix = _suffix_incl(tk)

    def head(s):
        cols = slice(s * HEAD_DIM, (s + 1) * HEAD_DIM)
        q = q_ref[:, cols]
        acc, r = acc_ref.at[s], r_ref.at[s]

        def kv(j):
            st = pl.multiple_of(j * tk, tk)
            return kc_ref[s, pl.ds(st, tk), :].astype(BF16), vc_ref[s, pl.ds(st, tk), :].astype(BF16)

        def new_rows():
            _sb_accumulate(acc, r, *_sb_tile(q, kn_ref[:, cols], vn_ref[:, cols], suffix_new, col < row))

        def single(j):
            _sb_accumulate(acc, r, *_sb_tile(q, *kv(j), suffix, None))

        def pair(j):
            pv_a, tot_a = _sb_tile(q, *kv(j), suffix, None)
            pv_b, tot_b = _sb_tile(q, *kv(j - 1), suffix, None)
            _sb_accumulate(acc, r, pv_a + jnp.exp2(-tot_a) * pv_b, tot_a + tot_b)

        return new_rows, single, pair, functools.partial(_sb_live, r)

    heads = [head(s) for s in range(group)]
    first = 2 if n_tiles >= 2 else n_tiles
    for new_rows, single, pair, _ in heads:
        new_rows()
        if first == 2:
            pair(n_tiles - 1)
        elif first == 1:
            single(0)
    for _, single, pair, live in heads:
        _for_past_tiles(n_tiles - first, pair, single, live)
    for s in range(group):
        o_ref[:, s * HEAD_DIM:(s + 1) * HEAD_DIM] = acc_ref[s].astype(o_ref.dtype)


def _sample_specs(layer, n_heads, group, seq, past, first_col):
    width = group * HEAD_DIM
    per_part = n_heads // group
    qkv = [pl.BlockSpec((seq, width), lambda b, h, k=k: (b, (first_col + k) * per_part + h)) for k in range(3)]
    cache = pl.BlockSpec((None, None, group, past, HEAD_DIM), lambda b, h: (layer, b, h, 0, 0))
    return qkv + [cache, cache]


def _sb_sample(qkv, cache_k, cache_v, layer, seq, tk, group):
    n = qkv.shape[0]
    nb = n // seq
    _, _, n_heads, past, _ = cache_k.shape
    assert n_heads % group == 0
    return pl.pallas_call(
        functools.partial(_sb_sample_kernel, tk=tk),
        grid=(nb, n_heads // group),
        in_specs=_sample_specs(layer, n_heads, group, seq, past, 0),
        out_specs=pl.BlockSpec((seq, group * HEAD_DIM), lambda b, h: (b, h)),
        out_shape=jax.ShapeDtypeStruct((n, n_heads * HEAD_DIM), BF16),
        scratch_shapes=[pltpu.VMEM((group, seq, HEAD_DIM), F32), pltpu.VMEM((group, seq, HEAD_DIM), F32)],
        compiler_params=_params(("arbitrary", "arbitrary")),
        name="sb_sample",
    )(qkv, qkv, qkv, cache_k, cache_v)


def _fox_init(m_ref, l_ref, acc_ref):
    m_ref[...] = jnp.full(m_ref.shape, NEG_BIG, F32)
    l_ref[...] = jnp.zeros_like(l_ref)
    acc_ref[...] = jnp.zeros_like(acc_ref)


def _fox_tile_t(qt, kb, vt, bias, mask, m_ref, l_ref, acc_ref):
    s = _dot(kb, qt) + jnp.concatenate([bias] * (qt.shape[1] // LANES), axis=1)
    if mask is not None:
        s = jnp.where(mask, s, NEG_BIG)
    m_prev = m_ref[...]
    m_new = jnp.maximum(m_prev, jnp.max(s, axis=0, keepdims=True))
    alpha = jnp.exp2(m_prev - m_new)
    p = jnp.exp2(s - m_new)
    l_ref[...] = alpha * l_ref[...] + jnp.sum(p, axis=0, keepdims=True)
    acc_ref[...] = alpha * acc_ref[...] + _dot(vt, p.astype(BF16))
    m_ref[...] = m_new


def _fox_prompt_kernel(q_ref, k_ref, v_ref, ck_ref, o_ref,
                       vt_ref, nb_ref, kmax_ref, edge_ref, m_ref, l_ref, acc_ref, *, tq, chunk):
    h = pl.program_id(0)
    qi = pl.program_id(1)
    t = k_ref.shape[0]
    reps = tq // LANES
    n_blocks = acc_ref.shape[0]

    @pl.when(qi == 0)
    def _():
        sel_r, _ = _iota2(LANES, LANES)
        sel = jnp.where(sel_r == h, 1.0, 0.0).astype(BF16)
        kmax_ref[...] = jnp.zeros_like(kmax_ref)

        def body(c, carry):
            st = pl.multiple_of(c * chunk, chunk)
            vt_ref[:, pl.ds(st, chunk)] = v_ref[pl.ds(st, chunk), :].astype(F32).T.astype(BF16)
            ck_h = sum(_dot(p, sel) for p in _split_bf16(ck_ref[pl.ds(st, chunk), :], 3))
            nb_ref[pl.ds(st, chunk), :] = ck_h * (-LOG2E)
            kf = k_ref[pl.ds(st, chunk), :].astype(F32)
            kmax_ref[...] = jnp.maximum(kmax_ref[...], jnp.max(jnp.sum(kf * kf, axis=1, keepdims=True)))
            return carry

        lax.fori_loop(0, t // chunk, body, 0)

    def block(s):
        g = n_blocks * qi + s
        qtf = q_ref[s * tq:(s + 1) * tq, :].astype(F32).T
        qt = qtf.astype(BF16)
        m, l, acc, edge = m_ref.at[s], l_ref.at[s], acc_ref.at[s], edge_ref.at[s]
        _fox_init(m, l, acc)
        qs = pl.multiple_of(g * tq, tq)
        c0 = nb_ref[pl.ds(qs, 1), :]
        edge[...] = c0

        def tile(st, tk, mask):
            bias = nb_ref[pl.ds(st, tk), :] - c0
            _fox_tile_t(qt, k_ref[pl.ds(st, tk), :], vt_ref[:, pl.ds(st, tk)], bias, mask, m, l, acc)
            edge[...] = nb_ref[pl.ds(st, 1), :]

        q_norm2 = jnp.sum(qtf * qtf, axis=0, keepdims=True)
        reach = jnp.sqrt(q_norm2 * jnp.concatenate([kmax_ref[...]] * reps, axis=1)) * 1.001

        def live():
            left = jnp.concatenate([edge[...] - c0] * reps, axis=1)
            return (jnp.max(reach + left - m[...]) > -FOX_DEAD).astype(jnp.int32)

        def first(with_left):
            if with_left:
                krow, qcol = _iota2(2 * tq, tq)
                tile(pl.multiple_of(qs - tq, tq), 2 * tq, krow - tq <= qcol)
            else:
                krow, qcol = _iota2(tq, tq)
                tile(qs, tq, krow <= qcol)

        def rest():
            _for_past_tiles(g - 1,
                            lambda j: tile(pl.multiple_of((j - 1) * tq, tq), 2 * tq, None),
                            lambda j: tile(pl.multiple_of(j * tq, tq), tq, None),
                            live)

        return first, rest

    blocks = [block(s) for s in range(n_blocks)]

    @pl.when(qi == 0)
    def _():
        for s, (first, _) in enumerate(blocks):
            first(s > 0)
        for _, rest in blocks[2:]:
            rest()

    @pl.when(qi > 0)
    def _():
        for first, _ in blocks:
            first(True)
        for _, rest in blocks:
            rest()

    for s in range(n_blocks):
        o_ref[s * tq:(s + 1) * tq, :] = (acc_ref[s] / l_ref[s]).T.astype(o_ref.dtype)


def _fox_prompt(qkv, ck, n_heads, tq, n_blocks):
    t = qkv.shape[0]
    rows = n_blocks * tq
    return pl.pallas_call(
        functools.partial(_fox_prompt_kernel, tq=tq, chunk=512),
        grid=(n_heads, t // rows),
        in_specs=[
            pl.BlockSpec((rows, HEAD_DIM), lambda h, i: (i, 3 * n_heads + h)),
            pl.BlockSpec((t, HEAD_DIM), lambda h, i: (0, 4 * n_heads + h)),
            pl.BlockSpec((t, HEAD_DIM), lambda h, i: (0, 5 * n_heads + h)),
            pl.BlockSpec((t, LANES), lambda h, i: (0, 0)),
        ],
        out_specs=pl.BlockSpec((rows, HEAD_DIM), lambda h, i: (i, h)),
        out_shape=jax.ShapeDtypeStruct((t, n_heads * HEAD_DIM), BF16),
        scratch_shapes=[pltpu.VMEM((HEAD_DIM, t), BF16), pltpu.VMEM((t, LANES), F32),
                        pltpu.VMEM((1, LANES), F32), pltpu.VMEM((n_blocks, 1, LANES), F32),
                        pltpu.VMEM((n_blocks, 1, tq), F32), pltpu.VMEM((n_blocks, 1, tq), F32),
                        pltpu.VMEM((n_blocks, HEAD_DIM, tq), F32)],
        compiler_params=_params(("arbitrary", "arbitrary")),
        name="fox_prompt",
    )(qkv, qkv, qkv, ck)


def _fox_tile(q, kb, vb, bias, mask, m_ref, l_ref, acc_ref):
    s = _dot_nt(q, kb) + bias
    if mask is not None:
        s = jnp.where(mask, s, NEG_BIG)
    m_prev = m_ref[...]
    m_new = jnp.maximum(m_prev, jnp.max(s, axis=1, keepdims=True))
    alpha = jnp.exp2(m_prev - m_new)
    p = jnp.exp2(s - m_new)
    l_ref[...] = alpha * l_ref[...] + jnp.sum(p, axis=1, keepdims=True)
    acc_ref[...] = alpha * acc_ref[...] + _dot(p.astype(BF16), vb)
    m_ref[...] = m_new


def _fox_sample_kernel(q_ref, kn_ref, vn_ref, kc_ref, vc_ref, ckp_ref, nmax_ref, ckl_ref, o_ref,
                       m_ref, l_ref, acc_ref, *, tk):
    group = kc_ref.shape[0]
    seq = q_ref.shape[0]
    past = kc_ref.shape[1]
    n_tiles = past // tk
    _fox_init(m_ref, l_ref, acc_ref)
    row, col = _iota2(seq, seq)

    def head(s):
        cols = slice(s * HEAD_DIM, (s + 1) * HEAD_DIM)
        q = q_ref[:, cols]
        m, l, acc = m_ref.at[s], l_ref.at[s], acc_ref.at[s]
        c0 = ckp_ref[s, :, past - 1:past]
        qf = q.astype(F32)
        k_abs = jnp.max(jnp.abs(kc_ref[s]))
        reach = jnp.sqrt(jnp.sum(qf * qf, axis=1, keepdims=True) * HEAD_DIM) * (k_abs * 1.01)

        def new_rows():
            _fox_tile(q, kn_ref[:, cols], vn_ref[:, cols], ckl_ref[s] * (-LOG2E), col <= row, m, l, acc)

        def tile(j):
            st = pl.multiple_of(j * tk, tk)
            kb = kc_ref[s, pl.ds(st, tk), :].astype(BF16)
            vb = vc_ref[s, pl.ds(st, tk), :].astype(BF16)
            bias = (c0 - ckp_ref[s, :, pl.ds(st, tk)]) * LOG2E
            _fox_tile(q, kb, vb, bias, None, m, l, acc)
            lo = pl.multiple_of(jnp.maximum(st - LANES, 0), LANES)
            left = nmax_ref[s, :, pl.ds(lo, LANES)][:, LANES - 1:LANES]
            return (jnp.max(reach + (c0 + left) * LOG2E - m[...]) > -FOX_DEAD).astype(jnp.int32)

        return new_rows, tile

    heads = [head(s) for s in range(group)]
    live = []
    for new_rows, tile in heads:
        new_rows()
        live.append(tile(n_tiles - 1))
    for (_, tile), live0 in zip(heads, live):
        lax.while_loop(lambda c: jnp.logical_and(c[0] >= 0, c[1] > 0),
                       lambda c, tile=tile: (c[0] - 1, tile(c[0])),
                       (jnp.int32(n_tiles - 2), live0))
    for s in range(group):
        o_ref[:, s * HEAD_DIM:(s + 1) * HEAD_DIM] = (acc_ref[s] / l_ref[s]).astype(o_ref.dtype)


def _fox_sample(qkv, cache_k, cache_v, ck_past, nmax_past, ck_local, layer, seq, tk, group):
    n = qkv.shape[0]
    nb = n // seq
    _, _, n_heads, past, _ = cache_k.shape
    assert n_heads % group == 0
    return pl.pallas_call(
        functools.partial(_fox_sample_kernel, tk=tk),
        grid=(nb, n_heads // group),
        in_specs=_sample_specs(layer, n_heads, group, seq, past, 3) + [
            pl.BlockSpec((group, None, 1, past), lambda b, h: (h, b, 0, 0)),
            pl.BlockSpec((group, None, 1, past), lambda b, h: (h, b, 0, 0)),
            pl.BlockSpec((group, None, 1, seq), lambda b, h: (h, b, 0, 0)),
        ],
        out_specs=pl.BlockSpec((seq, group * HEAD_DIM), lambda b, h: (b, h)),
        out_shape=jax.ShapeDtypeStruct((n, n_heads * HEAD_DIM), BF16),
        scratch_shapes=[pltpu.VMEM((group, seq, 1), F32), pltpu.VMEM((group, seq, 1), F32),
                        pltpu.VMEM((group, seq, HEAD_DIM), F32)],
        compiler_params=_params(("arbitrary", "arbitrary")),
        name="fox_sample",
    )(qkv, qkv, qkv, cache_k, cache_v, ck_past, nmax_past, ck_local)


def _out_proj_kernel(ya_ref, ysb_ref, yfox_ref, w_ref, x_ref, g_ref, xo_ref, h_ref):
    c = ya_ref.shape[1]
    s = ysb_ref.shape[1]
    acc = _dot(ya_ref[...], w_ref[0:c, :])
    acc += _dot(ysb_ref[...], w_ref[c:c + s, :])
    acc += _dot(yfox_ref[...], w_ref[c + s:, :])
    x = x_ref[...] + acc
    xo_ref[...] = x
    h_ref[...] = _rmsnorm_f32(x, g_ref[...]).astype(h_ref.dtype)


def _out_proj(ya, ysb, yfox, w, layer, x, g, tm):
    m, d = x.shape
    row = lambda width: pl.BlockSpec((tm, width), lambda i: (i, 0))
    return pl.pallas_call(
        _out_proj_kernel,
        grid=(m // tm,),
        in_specs=[row(ya.shape[1]), row(ysb.shape[1]), row(yfox.shape[1]),
                  pl.BlockSpec((None,) + w.shape[1:], lambda i: (layer, 0, 0)), row(d),
                  pl.BlockSpec((1, d), lambda i: (0, 0))],
        out_specs=[row(d), row(d)],
        out_shape=[jax.ShapeDtypeStruct((m, d), F32), jax.ShapeDtypeStruct((m, d), BF16)],
        compiler_params=_params(("arbitrary",)),
        name="out_proj",
    )(ya, ysb, yfox, w, x, g.reshape(1, d))


def _ffn_kernel(h_ref, wg_ref, wu_ref, wd_ref, x_ref, g_ref, *rest, emit_x):
    out_refs, acc_ref = rest[:-1], rest[-1]
    f = pl.program_id(1)

    @pl.when(f == 0)
    def _():
        acc_ref[...] = jnp.zeros_like(acc_ref)

    h = h_ref[...]
    gate = _dot(h, wg_ref[...])
    up = _dot(h, wu_ref[...])
    act = gate * jax.nn.sigmoid(gate) * up
    acc_ref[...] += _dot(act.astype(BF16), wd_ref[...])

    @pl.when(f == pl.num_programs(1) - 1)
    def _():
        x = x_ref[...] + acc_ref[...]
        normed = _rmsnorm_f32(x, g_ref[...])
        if emit_x:
            out_refs[0][...] = x
            out_refs[1][...] = normed.astype(out_refs[1].dtype)
        else:
            out_refs[0][...] = normed


def _ffn(h, wg, wu, wd, layer, x, g, tm, tf, emit_x):
    m, d = x.shape
    ff = wg.shape[2]
    row = pl.BlockSpec((tm, d), lambda i, f: (i, 0))
    if emit_x:
        out_specs = [row, row]
        out_shape = [jax.ShapeDtypeStruct((m, d), F32), jax.ShapeDtypeStruct((m, d), BF16)]
    else:
        out_specs = [row]
        out_shape = [jax.ShapeDtypeStruct((m, d), F32)]
    return pl.pallas_call(
        functools.partial(_ffn_kernel, emit_x=emit_x),
        grid=(m // tm, ff // tf),
        in_specs=[row,
                  pl.BlockSpec((None, d, tf), lambda i, f: (layer, 0, f)),
                  pl.BlockSpec((None, d, tf), lambda i, f: (layer, 0, f)),
                  pl.BlockSpec((None, tf, d), lambda i, f: (layer, f, 0)),
                  row,
                  pl.BlockSpec((1, d), lambda i, f: (0, 0))],
        out_specs=out_specs,
        out_shape=out_shape,
        scratch_shapes=[pltpu.VMEM((tm, d), F32)],
        compiler_params=_params(("arbitrary", "arbitrary")),
        name="ffn",
    )(h, wg, wu, wd, x, g.reshape(1, d))


def kernel(x_prompt, x_sample, state_conv, cache_sb_k, cache_sb_v, cache_fox_k, cache_fox_v, cache_fox_logf,
           norm1_g, w_in, b_f, conv_w, w_out, norm2_g, w_gate, w_up, w_down, final_g):
    depth = w_in.shape[0]
    bp, t, d = x_prompt.shape
    nb, seq, _ = x_sample.shape
    assert bp == 1, "prompt kernels carry conv rows and forget sums across row tiles of one stream"
    c = conv_w.shape[2]
    n_sb = cache_sb_k.shape[3]
    n_fox = cache_fox_k.shape[3]
    assert n_sb == n_fox and n_fox <= 8
    past = cache_sb_k.shape[2]
    sb_dim = n_sb * HEAD_DIM
    qkv0 = 3 * c
    qkv1 = qkv0 + 3 * sb_dim + 3 * n_fox * HEAD_DIM
    n_s = nb * seq
    per_head = lambda a: jnp.transpose(a, (0, 1, 3, 2, 4))
    csk, csv, cfk, cfv = (per_head(a) for a in (cache_sb_k, cache_sb_v, cache_fox_k, cache_fox_v))
    w_ndk = jnp.transpose(w_in, (2, 0, 1))
    wt = _wprep(w_ndk, qkv1, W_PREP_TILE)
    wo, wg, wu, wd = (w.astype(BF16) for w in (w_out, w_gate, w_up, w_down))

    xp = x_prompt.reshape(t, d)
    xs = x_sample.reshape(n_s, d)
    hp = _rmsnorm(xp, norm1_g[0], ROW_TILE)
    hs = _rmsnorm(xs, norm1_g[0], n_s)
    p_small, s_small = [], []
    p_kv, s_kv = [], []
    for l in range(depth):
        bf_row = jnp.pad(b_f[l], (0, LANES - n_fox)).reshape(1, LANES)
        bf_col = jnp.pad(b_f[l], (0, 8 - n_fox)).reshape(8, 1)
        last = l == depth - 1
        g_next = final_g if last else norm1_g[l + 1]

        ya, conv_p, logf_p, ck_p = _mix_a_prompt(hp, wt, w_ndk, qkv1, l, bf_row, conv_w[l], ROW_TILE)
        stacking = last and l > 0
        qkv, *kv = _qkv_proj(hp, wt, qkv0, QKV_STACK_ROW_TILE if stacking else QKV_ROW_TILE, l, bp, n_sb,
                             p_kv if stacking else ())
        p_kv.append(kv)
        ysb = _sb_prompt(qkv, n_sb, ATTN_TILE, ATTN_BLOCKS)
        yfox = _fox_prompt(qkv, ck_p, n_fox, ATTN_TILE, ATTN_BLOCKS)
        xp, h2 = _out_proj(ya, ysb, yfox, wo, l, xp, norm2_g[l], ROW_TILE)
        outs = _ffn(h2, wg, wu, wd, l, xp, g_next, ROW_TILE, FF_TILE, not last)
        if last:
            y_prompt = outs[0]
        else:
            xp, hp = outs
        p_small.append((conv_p, logf_p[:, :n_fox]))

        logf_past = jnp.transpose(cache_fox_logf[l], (2, 0, 1)).reshape(n_fox * nb, past)
        ck_past, nmax_past = (a.reshape(n_fox, nb, 1, past) for a in _cumsum_rows(logf_past))
        ya, conv_s, logf_s, ckl = _mix_a_sample(hs, wt, w_ndk, qkv1, l, bf_col, conv_w[l],
                                                state_conv[l].reshape(nb * (CONV_W - 1), c), seq)
        ckl = ckl[:n_fox].reshape(n_fox, nb, 1, seq)
        qkv, *kv = _qkv_proj(hs, wt, qkv0, n_s, l, nb, n_sb, s_kv if stacking else ())
        s_kv.append(kv)
        ysb = _sb_sample(qkv, csk, csv, l, seq, DECODE_SB_TILE, DECODE_HEADS)
        yfox = _fox_sample(qkv, cfk, cfv, ck_past, nmax_past, ckl, l, seq, DECODE_FOX_TILE, DECODE_HEADS)
        xs, h2 = _out_proj(ya, ysb, yfox, wo, l, xs, norm2_g[l], n_s)
        outs = _ffn(h2, wg, wu, wd, l, xs, g_next, n_s, DECODE_FF_TILE, not last)
        if last:
            y_sample = outs[0]
        else:
            xs, hs = outs
        logf_s = jnp.transpose(logf_s[:n_fox].reshape(n_fox, nb, seq), (1, 2, 0))
        s_small.append((conv_s, logf_s))

    stack = lambda news, i, shape: jnp.stack([n[i] for n in news]).reshape((depth,) + shape)
    stack_kv = lambda news: [per_head(a if depth > 1 else a[None]) for a in news[-1]]
    return (
        y_prompt.reshape(bp, t, d),
        y_sample.reshape(nb, seq, d),
        stack(p_small, 0, (bp, CONV_W - 1, c)),
        *stack_kv(p_kv),
        stack(p_small, 1, (bp, t, n_fox)),
        stack(s_small, 0, (nb, CONV_W - 1, c)),
        *stack_kv(s_kv),
        stack(s_small, 1, (nb, seq, n_fox)),
    )
```
